```python
import math
import jax, jax.numpy as jnp
from jax import lax
import numpy as np


D_MODEL = 1024
BATCH = 2
SEQ = 8192
DEPTH = 1
DEC_BATCH = 8
DEC_SEQ = 32
PAST_LEN = 2048

CHUNK = 64
EPS = 1e-5
SSD_EXPAND = 2
SSD_D_INNER = SSD_EXPAND * D_MODEL
SSD_HEAD_DIM = 64
SSD_N_HEADS = SSD_D_INNER // SSD_HEAD_DIM
SSD_GROUPS = 4
SSD_D_STATE = 128
SSD_CONV_WIDTH = 4
SSD_CONV_CH = SSD_D_INNER + 2 * SSD_GROUPS * SSD_D_STATE
SSD_BLOCK = CHUNK
DT_MIN = 0.001
DT_MAX = 0.1
ATT_HEAD_DIM = 64
ATT_N_HEADS = D_MODEL // (2 * ATT_HEAD_DIM)
ATT_QK_DIM = ATT_N_HEADS * 2 * ATT_HEAD_DIM
ATT_V_HEAD = 2 * ATT_HEAD_DIM
ATT_V_DIM = ATT_N_HEADS * ATT_V_HEAD
Q_BLOCK = 128
ALIBI_MAX_BIAS = 8.0
IN_DIM = SSD_D_INNER + SSD_CONV_CH + SSD_N_HEADS + 2 * ATT_QK_DIM + ATT_V_DIM + 2 * D_MODEL
N_EXPERTS = 32
TOP_K = 4
D_FF = D_MODEL
SWIGLU_LIMIT = 7.0
SWIGLU_ALPHA = 1.702
MOE_BLOCK = 128

kernel_name = 'hybrid_ssd_diffattn_moe_stream_step'


def rmsnorm(x, g):
    xf = x.astype(jnp.float32)
    y = xf * lax.rsqrt(jnp.mean(xf * xf, axis=-1, keepdims=True) + EPS)
    return y.astype(x.dtype) * g


def split_proj(p):
    sizes = (SSD_D_INNER, SSD_CONV_CH, SSD_N_HEADS, ATT_QK_DIM, ATT_QK_DIM, ATT_V_DIM, D_MODEL, D_MODEL)
    idx = [int(i) for i in np.cumsum(sizes)[:-1]]
    return jnp.split(p, idx, axis=-1)


def causal_conv(xbc, prev, w, b):
    L = xbc.shape[1]
    xpad = jnp.concatenate([prev.astype(xbc.dtype), xbc], axis=1)
    y = b
    for k in range(SSD_CONV_WIDTH):
        y = y + xpad[:, k:k + L] * w[k]
    return jax.nn.silu(y), xpad[:, -(SSD_CONV_WIDTH - 1):]


def ssd_scan(x, a, B, C, h0, block):
    b, L, H, P = x.shape
    G, N = B.shape[2], B.shape[3]
    R = H // G
    nc = L // block
    x = x.reshape(b, nc, block, G, R, P)
    a = a.reshape(b, nc, block, G, R).astype(jnp.float32)
    B = B.reshape(b, nc, block, G, N)
    C = C.reshape(b, nc, block, G, N)
    a_cs = jnp.cumsum(a, axis=2)
    diff = a_cs[:, :, :, None] - a_cs[:, :, None]
    causal = jnp.tril(jnp.ones((block, block), bool))[None, None, :, :, None, None]
    decay = jnp.exp(jnp.where(causal, diff, -jnp.inf))
    cb = jnp.einsum('bctgn,bcsgn->bctsg', C, B)
    scores = cb[..., None] * decay
    y_diag = jnp.einsum('bctsgr,bcsgrp->bctgrp', scores, x)
    x_dec = x * jnp.exp(a_cs[:, :, -1:] - a_cs)[..., None]
    states = jnp.einsum('bclgn,bclgrp->bcgrpn', B, x_dec)
    block_decay = jnp.exp(a_cs[:, :, -1])

    def step(h, inp):
        dec, st = inp
        return dec[..., None, None] * h + st, h

    h_init = h0.reshape(b, G, R, P, N).astype(jnp.float32)
    h_fin, h_in = lax.scan(step, h_init, (jnp.moveaxis(block_decay, 1, 0), jnp.moveaxis(states, 1, 0)))
    h_in = jnp.moveaxis(h_in, 0, 1)
    y_off = jnp.einsum('bclgn,bcgrpn->bclgrp', C, h_in) * jnp.exp(a_cs)[..., None]
    y = (y_diag + y_off).reshape(b, L, H, P)
    return y, h_fin.reshape(b, H, P, N)


def ssd_branch(z, xbc, dt_raw, conv_prev, h0, conv_w, conv_b, dt_bias, a_log, d_skip, ssd_norm, w_o_ssd, block):
    b, L, _ = z.shape
    xc, new_conv = causal_conv(xbc, conv_prev, conv_w, conv_b)
    xs, Bs, Cs = jnp.split(xc, [SSD_D_INNER, SSD_D_INNER + SSD_GROUPS * SSD_D_STATE], axis=-1)
    xs = xs.reshape(b, L, SSD_N_HEADS, SSD_HEAD_DIM)
    Bs = Bs.reshape(b, L, SSD_GROUPS, SSD_D_STATE)
    Cs = Cs.reshape(b, L, SSD_GROUPS, SSD_D_STATE)
    dt = jax.nn.softplus(dt_raw.astype(jnp.float32) + dt_bias)
    A = -jnp.exp(a_log.astype(jnp.float32))
    y, h_new = ssd_scan(xs * dt[..., None], dt * A, Bs, Cs, h0, block)
    y = y + xs * d_skip[:, None]
    y = y.reshape(b, L, SSD_D_INNER) * jax.nn.silu(z)
    y = rmsnorm(y.reshape(b, L, SSD_GROUPS, -1), ssd_norm.reshape(SSD_GROUPS, -1)).reshape(b, L, SSD_D_INNER)
    return y @ w_o_ssd, new_conv, h_new


def alibi_slopes():
    return jnp.exp2(-ALIBI_MAX_BIAS * jnp.arange(1, ATT_N_HEADS + 1, dtype=jnp.float32) / ATT_N_HEADS)


def diff_attention(q, k, v, qpos, kpos, lam, slopes):
    allowed = (kpos[None, :] // CHUNK) <= (qpos[:, None] // CHUNK)
    dist = jnp.abs(qpos[:, None] - kpos[None, :]).astype(jnp.float32)
    bias = jnp.where(allowed[None], -slopes[:, None, None] * dist[None], -jnp.inf)
    scale = ATT_HEAD_DIM ** -0.5

    def probs(qh, kh):
        s = jnp.einsum('bqhd,bkhd->bhqk', qh, kh).astype(jnp.float32) * scale + bias
        return jax.nn.softmax(s, axis=-1)

    p = probs(q[..., :ATT_HEAD_DIM], k[..., :ATT_HEAD_DIM]) - lam * probs(q[..., ATT_HEAD_DIM:], k[..., ATT_HEAD_DIM:])
    return jnp.einsum('bhqk,bkhe->bqhe', p.astype(v.dtype), v)


def diff_attention_prompt(q, k, v, lam, slopes):
    b, S = q.shape[0], q.shape[1]
    nblk = S // Q_BLOCK
    qb = jnp.moveaxis(q.reshape(b, nblk, Q_BLOCK, ATT_N_HEADS, 2 * ATT_HEAD_DIM), 1, 0)
    starts = jnp.arange(nblk, dtype=jnp.int32) * Q_BLOCK
    kpos = jnp.arange(S, dtype=jnp.int32)

    def one_block(args):
        qblk, st = args
        return diff_attention(qblk, k, v, st + jnp.arange(Q_BLOCK, dtype=jnp.int32), kpos, lam, slopes)

    o = lax.map(one_block, (qb, starts))
    return jnp.moveaxis(o, 0, 1).reshape(b, S, ATT_N_HEADS, ATT_V_HEAD)


def attn_out(o, subln, lambda_init, w_o_att):
    b, L = o.shape[0], o.shape[1]
    o = rmsnorm(o, subln) * (1.0 - lambda_init)
    return o.reshape(b, L, ATT_V_DIM) @ w_o_att


def merge_branches(x, o_ssd, o_att, g_ssd, g_att, w_out):
    return x + (jax.nn.sigmoid(g_ssd) * o_ssd + jax.nn.sigmoid(g_att) * o_att) @ w_out


def clamped_swiglu(gu):
    gate, up = jnp.split(gu, 2, axis=-1)
    gate = jnp.minimum(gate, SWIGLU_LIMIT)
    up = jnp.clip(up, -SWIGLU_LIMIT, SWIGLU_LIMIT)
    return (up + 1.0) * gate * jax.nn.sigmoid(SWIGLU_ALPHA * gate)


def moe_ffn(h, router_w, router_b, w_gu, b_gu, w_down, b_down):
    T, D = h.shape
    TK = T * TOP_K
    NB = (TK + N_EXPERTS * (MOE_BLOCK - 1) + MOE_BLOCK - 1) // MOE_BLOCK
    logits = (h @ router_w + router_b).astype(jnp.float32)
    top_logit, top_e = lax.top_k(logits, TOP_K)
    gates = jax.nn.softmax(top_logit, axis=-1)
    flat_e = top_e.reshape(-1)
    flat_tok = jnp.arange(TK, dtype=jnp.int32) // TOP_K
    flat_gate = gates.reshape(-1)
    order = jnp.argsort(flat_e)
    sorted_e = flat_e[order]
    counts = jnp.zeros((N_EXPERTS,), jnp.int32).at[flat_e].add(1)
    padded = (counts + MOE_BLOCK - 1) // MOE_BLOCK * MOE_BLOCK
    start_sorted = jnp.cumsum(counts) - counts
    ends_padded = jnp.cumsum(padded)
    start_padded = ends_padded - padded
    rank = jnp.arange(TK, dtype=jnp.int32) - start_sorted[sorted_e]
    dest = start_padded[sorted_e] + rank
    rows_tok = jnp.full((NB * MOE_BLOCK,), T, jnp.int32).at[dest].set(flat_tok[order])
    rows_gate = jnp.zeros((NB * MOE_BLOCK,), jnp.float32).at[dest].set(flat_gate[order])
    block_e = jnp.clip(jnp.searchsorted(ends_padded, jnp.arange(NB, dtype=jnp.int32) * MOE_BLOCK, side='right'), 0, N_EXPERTS - 1)
    h_pad = jnp.concatenate([h, jnp.zeros((1, D), h.dtype)], axis=0)
    xs = h_pad[rows_tok].reshape(NB, MOE_BLOCK, D)

    def expert_block(args):
        xb, e = args
        act = clamped_swiglu(xb @ w_gu[e] + b_gu[e])
        return act @ w_down[e] + b_down[e]

    ys = lax.map(expert_block, (xs, block_e)).reshape(NB * MOE_BLOCK, D)
    out = jnp.zeros((T + 1, D), jnp.float32).at[rows_tok].add(ys * rows_gate[:, None])
    return out[:T].astype(h.dtype)


def setup_inputs(seed: int = 0) -> dict:
    key = jax.random.key(seed)
    ks = jax.random.split(key, 32)
    L = DEPTH

    def nrm(k, shape, scale):
        return jax.random.normal(k, shape, jnp.float32) * scale

    dt = jnp.exp(jax.random.uniform(ks[10], (L, SSD_N_HEADS), jnp.float32, math.log(DT_MIN), math.log(DT_MAX)))
    return {
        'x_prompt': nrm(ks[0], (BATCH, SEQ, D_MODEL), 1.0),
        'x_sample': nrm(ks[1], (DEC_BATCH, DEC_SEQ, D_MODEL), 1.0),
        'cache_k': nrm(ks[2], (L, DEC_BATCH, PAST_LEN, ATT_N_HEADS, 2 * ATT_HEAD_DIM), 1.0),
        'cache_v': nrm(ks[3], (L, DEC_BATCH, PAST_LEN, ATT_N_HEADS, ATT_V_HEAD), 1.0),
        'state_ssm': nrm(ks[4], (L, DEC_BATCH, SSD_N_HEADS, SSD_HEAD_DIM, SSD_D_STATE), 0.1),
        'state_conv': nrm(ks[5], (L, DEC_BATCH, SSD_CONV_WIDTH - 1, SSD_CONV_CH), 1.0),
        'norm_mix': 1.0 + nrm(ks[6], (L, D_MODEL), 0.01),
        'w_in': nrm(ks[7], (L, D_MODEL, IN_DIM), D_MODEL ** -0.5),
        'conv_w': nrm(ks[8], (L, SSD_CONV_WIDTH, SSD_CONV_CH), SSD_CONV_WIDTH ** -0.5),
        'conv_b': nrm(ks[9], (L, SSD_CONV_CH), 0.01),
        'dt_bias': dt + jnp.log(-jnp.expm1(-dt)),
        'a_log': jnp.log(jax.random.uniform(ks[11], (L, SSD_N_HEADS), jnp.float32, 1.0, 16.0)),
        'd_skip': 1.0 + nrm(ks[12], (L, SSD_N_HEADS), 0.01),
        'ssd_norm': 1.0 + nrm(ks[13], (L, SSD_D_INNER), 0.01),
        'w_o_ssd': nrm(ks[14], (L, SSD_D_INNER, D_MODEL), SSD_D_INNER ** -0.5),
        'lambda_q1': nrm(ks[15], (L, ATT_HEAD_DIM), 0.1),
        'lambda_k1': nrm(ks[16], (L, ATT_HEAD_DIM), 0.1),
        'lambda_q2': nrm(ks[17], (L, ATT_HEAD_DIM), 0.1),
        'lambda_k2': nrm(ks[18], (L, ATT_HEAD_DIM), 0.1),
        'subln': 1.0 + nrm(ks[19], (L, ATT_V_HEAD), 0.01),
        'w_o_att': nrm(ks[20], (L, ATT_V_DIM, D_MODEL), ATT_V_DIM ** -0.5),
        'w_out': nrm(ks[21], (L, D_MODEL, D_MODEL), D_MODEL ** -0.5),
        'norm_ffn': 1.0 + nrm(ks[22], (L, D_MODEL), 0.01),
        'router_w': nrm(ks[23], (L, D_MODEL, N_EXPERTS), D_MODEL ** -0.5),
        'router_b': nrm(ks[24], (L, N_EXPERTS), 0.01),
        'w_gu': nrm(ks[25], (L, N_EXPERTS, D_MODEL, 2 * D_FF), D_MODEL ** -0.5),
        'b_gu': nrm(ks[26], (L, N_EXPERTS, 2 * D_FF), 0.01),
        'w_down': nrm(ks[27], (L, N_EXPERTS, D_FF, D_MODEL), D_FF ** -0.5),
        'b_down': nrm(ks[28], (L, N_EXPERTS, D_MODEL), 0.01),
        'norm_final': 1.0 + nrm(ks[29], (D_MODEL,), 0.01),
    }


def reference(x_prompt, x_sample, cache_k, cache_v, state_ssm, state_conv, norm_mix, w_in, conv_w, conv_b,
              dt_bias, a_log, d_skip, ssd_norm, w_o_ssd, lambda_q1, lambda_k1, lambda_q2, lambda_k2, subln,
              w_o_att, w_out, norm_ffn, router_w, router_b, w_gu, b_gu, w_down, b_down, norm_final):
    bp, S = x_prompt.shape[0], x_prompt.shape[1]
    bs, Ts = x_sample.shape[0], x_sample.shape[1]
    P = cache_k.shape[2]
    slopes = alibi_slopes()
    xp, xs = x_prompt, x_sample
    kp_l, vp_l, sp_l, cp_l, ks_l, vs_l, ss_l, cs_l = [], [], [], [], [], [], [], []
    for layer in range(DEPTH):
        lambda_init = 0.8 - 0.6 * math.exp(-0.3 * layer)
        lam = (jnp.exp(jnp.sum((lambda_q1[layer] * lambda_k1[layer]).astype(jnp.float32)))
               - jnp.exp(jnp.sum((lambda_q2[layer] * lambda_k2[layer]).astype(jnp.float32))) + lambda_init)
        ssd_w = (conv_w[layer], conv_b[layer], dt_bias[layer], a_log[layer], d_skip[layer], ssd_norm[layer], w_o_ssd[layer])

        z, xbc, dt_raw, q, k, v, g_ssd, g_att = split_proj(rmsnorm(xp, norm_mix[layer]) @ w_in[layer])
        conv0 = jnp.zeros((bp, SSD_CONV_WIDTH - 1, SSD_CONV_CH), xp.dtype)
        h0 = jnp.zeros((bp, SSD_N_HEADS, SSD_HEAD_DIM, SSD_D_STATE), jnp.float32)
        o_ssd, conv_p, ssm_p = ssd_branch(z, xbc, dt_raw, conv0, h0, *ssd_w, SSD_BLOCK)
        qh = q.reshape(bp, S, ATT_N_HEADS, 2 * ATT_HEAD_DIM)
        kh = k.reshape(bp, S, ATT_N_HEADS, 2 * ATT_HEAD_DIM)
        vh = v.reshape(bp, S, ATT_N_HEADS, ATT_V_HEAD)
        o_att = attn_out(diff_attention_prompt(qh, kh, vh, lam, slopes), subln[layer], lambda_init, w_o_att[layer])
        xp = merge_branches(xp, o_ssd, o_att, g_ssd, g_att, w_out[layer])
        kp_l.append(kh); vp_l.append(vh); sp_l.append(ssm_p); cp_l.append(conv_p)

        z, xbc, dt_raw, q, k, v, g_ssd, g_att = split_proj(rmsnorm(xs, norm_mix[layer]) @ w_in[layer])
        o_ssd, conv_s, ssm_s = ssd_branch(z, xbc, dt_raw, state_conv[layer], state_ssm[layer], *ssd_w, Ts)
        qh = q.reshape(bs, Ts, ATT_N_HEADS, 2 * ATT_HEAD_DIM)
        kh = k.reshape(bs, Ts, ATT_N_HEADS, 2 * ATT_HEAD_DIM)
        vh = v.reshape(bs, Ts, ATT_N_HEADS, ATT_V_HEAD)
        k_all = jnp.concatenate([cache_k[layer].astype(kh.dtype), kh], axis=1)
        v_all = jnp.concatenate([cache_v[layer].astype(vh.dtype), vh], axis=1)
        qpos = P + jnp.arange(Ts, dtype=jnp.int32)
        kpos = jnp.arange(P + Ts, dtype=jnp.int32)
        o_att = attn_out(diff_attention(qh, k_all, v_all, qpos, kpos, lam, slopes), subln[layer], lambda_init, w_o_att[layer])
        xs = merge_branches(xs, o_ssd, o_att, g_ssd, g_att, w_out[layer])
        ks_l.append(kh); vs_l.append(vh); ss_l.append(ssm_s); cs_l.append(conv_s)

        tok = jnp.concatenate([xp.reshape(-1, D_MODEL), xs.reshape(-1, D_MODEL)], axis=0)
        tok = tok + moe_ffn(rmsnorm(tok, norm_ffn[layer]), router_w[layer], router_b[layer],
                            w_gu[layer], b_gu[layer], w_down[layer], b_down[layer])
        xp = tok[:bp * S].reshape(bp, S, D_MODEL)
        xs = tok[bp * S:].reshape(bs, Ts, D_MODEL)

    y_prompt = rmsnorm(xp, norm_final)
    y_sample = rmsnorm(xs, norm_final)
    return (y_prompt, y_sample, jnp.stack(kp_l), jnp.stack(vp_l), jnp.stack(sp_l), jnp.stack(cp_l),
            jnp.stack(ks_l), jnp.stack(vs_l), jnp.stack(ss_l), jnp.stack(cs_l))
```

```python
import functools
import math

import jax
import jax.numpy as jnp
from jax import lax
from jax.experimental import pallas as pl
from jax.experimental.pallas import tpu as pltpu

EPS = 1e-5
CHUNK = 64
D_MODEL = 1024
SSD_D_INNER = 2048
SSD_HEAD_DIM = 64
SSD_N_HEADS = 32
SSD_GROUPS = 4
SSD_D_STATE = 128
SSD_CONV_WIDTH = 4
SSD_CONV_CH = 3072
ATT_HEAD_DIM = 64
ATT_N_HEADS = 8
ATT_V_HEAD = 128
ALIBI_MAX_BIAS = 8.0
N_EXPERTS = 32
TOP_K = 4
SWIGLU_LIMIT = 7.0
SWIGLU_ALPHA = 1.702

LANES = 128
SUBLANES = 8
VMEM_LIMIT = 56 * 1024 * 1024
NEG_BIG = -1e30

BF16 = jnp.bfloat16
F32 = jnp.float32


def _cparams(sem):
    return pltpu.CompilerParams(dimension_semantics=sem, vmem_limit_bytes=VMEM_LIMIT)


def _split3(x):
    h1 = x.astype(BF16)
    r1 = x - h1.astype(F32)
    h2 = r1.astype(BF16)
    h3 = (r1 - h2.astype(F32)).astype(BF16)
    return h1, h2, h3


def _dot(a, b):
    return jnp.dot(a, b, preferred_element_type=F32)


def _dot_exact_rhs(x, m):
    h1, h2, h3 = _split3(x)
    return _dot(h1, m) + _dot(h2, m) + _dot(h3, m)


def _dot_exact_lhs(m, x):
    h1, h2, h3 = _split3(x)
    return _dot(m, h1) + _dot(m, h2) + _dot(m, h3)


def _norm_matmul_kernel(x_ref, g_ref, w_ref, o_ref, xn_ref):
    @pl.when(pl.program_id(1) == 0)
    def _():
        x = x_ref[...]
        y = x * lax.rsqrt(jnp.mean(x * x, axis=-1, keepdims=True) + EPS)
        xn_ref[...] = (y * g_ref[...]).astype(BF16)

    o_ref[...] = _dot(xn_ref[...], w_ref[...]).astype(o_ref.dtype)


def norm_matmul(x, gain, w, out_dtype, tm, tn):
    t, d = x.shape
    n = w.shape[1]
    assert t % tm == 0 and n % tn == 0
    return pl.pallas_call(
        _norm_matmul_kernel,
        grid=(t // tm, n // tn),
        in_specs=[
            pl.BlockSpec((tm, d), lambda i, j: (i, 0)),
            pl.BlockSpec((1, d), lambda i, j: (0, 0)),
            pl.BlockSpec((d, tn), lambda i, j: (0, j)),
        ],
        out_specs=pl.BlockSpec((tm, tn), lambda i, j: (i, j)),
        out_shape=jax.ShapeDtypeStruct((t, n), out_dtype),
        scratch_shapes=[pltpu.VMEM((tm, d), BF16)],
        compiler_params=_cparams(("arbitrary", "arbitrary")),
        name="norm_matmul",
    )(x, gain.reshape(1, d), w)


def _ssd_kernel(xbc_ref, z_ref, prev_ref, h0_ref, cw_ref, cb_ref, dtb_ref, alog_ref, dsk_ref, gn_ref,
                y_ref, hout_ref, xpad_ref, xc_ref, tail_ref, st_ref, *, tl, valid_len):
    t = pl.program_id(1)
    nt = pl.num_programs(1)
    nch = tl // CHUNK
    dinner = SSD_D_INNER
    gw = dinner // SSD_GROUPS
    ns = SSD_D_STATE

    @pl.when(t == 0)
    def _():
        tail_ref[...] = prev_ref[0]
        st_ref[...] = h0_ref[0]

    xpad_ref[0:SUBLANES, :] = tail_ref[...]
    xpad_ref[SUBLANES:SUBLANES + tl, :] = xbc_ref[:, 0:SSD_CONV_CH]
    tail_ref[...] = xbc_ref[tl - SUBLANES:tl, 0:SSD_CONV_CH]
    acc = cb_ref[...]
    for k in range(SSD_CONV_WIDTH):
        off = SUBLANES - (SSD_CONV_WIDTH - 1) + k
        acc = acc + xpad_ref[off:off + tl, :] * cw_ref[k:k + 1, :]
    xc_ref[...] = acc * jax.nn.sigmoid(acc)

    head_of_lane = lax.broadcasted_iota(jnp.int32, (SSD_N_HEADS, dinner), 1) // SSD_HEAD_DIM
    expand = (head_of_lane == lax.broadcasted_iota(jnp.int32, (SSD_N_HEADS, dinner), 0)).astype(BF16)
    ti = lax.broadcasted_iota(jnp.int32, (CHUNK, CHUNK), 0)
    si = lax.broadcasted_iota(jnp.int32, (CHUNK, CHUNK), 1)
    tril = (si <= ti).astype(BF16)
    row_c = lax.broadcasted_iota(jnp.int32, (CHUNK, dinner), 0)
    pos_in_head = lax.broadcasted_iota(jnp.int32, (CHUNK, dinner), 1) % CHUNK
    upper = row_c <= pos_in_head
    row_p = lax.broadcasted_iota(jnp.int32, (CHUNK, LANES), 0)
    lane_p = lax.broadcasted_iota(jnp.int32, (CHUNK, LANES), 1)
    causal_pair = (lane_p % CHUNK) <= row_p
    left_half = lane_p < SSD_HEAD_DIM
    a_neg = -jnp.exp(alog_ref[...])
    dsk_e = _dot_exact_rhs(dsk_ref[...], expand)

    def chunk_body(c, carry):
        r0 = pl.multiple_of(c * CHUNK, CHUNK)
        xs = xc_ref[pl.ds(r0, CHUNK), 0:dinner]
        dt_raw = xbc_ref[pl.ds(r0, CHUNK), SSD_CONV_CH:SSD_CONV_CH + SSD_N_HEADS]
        dtv = dt_raw + dtb_ref[...]
        dt = jnp.maximum(dtv, 0.0) + jnp.log1p(jnp.exp(-jnp.abs(dtv)))
        if valid_len is not None:
            rows = t * tl + r0 + lax.broadcasted_iota(jnp.int32, (CHUNK, SSD_N_HEADS), 0)
            dt = jnp.where(rows < valid_len, dt, 0.0)
        a = dt * a_neg
        a_e = _dot_exact_rhs(a, expand)
        dt_e = _dot_exact_rhs(dt, expand)
        acs_e = _dot_exact_lhs(tril, a_e)
        rowterm = jnp.sum(jnp.where(upper, a_e, 0.0), axis=0, keepdims=True)
        acs_last = acs_e[CHUNK - 1:CHUNK, :]
        xdt = xs * dt_e
        x_dec = (xdt * jnp.exp(acs_last - acs_e)).astype(BF16)
        e_acs = jnp.exp(acs_e)
        e_last = jnp.exp(acs_last)

        y_parts = []
        for g in range(SSD_GROUPS):
            lo = g * gw
            bm = xc_ref[pl.ds(r0, CHUNK), dinner + g * ns:dinner + (g + 1) * ns].astype(BF16)
            cm = xc_ref[pl.ds(r0, CHUNK), dinner + SSD_GROUPS * ns + g * ns:
                        dinner + SSD_GROUPS * ns + (g + 1) * ns].astype(BF16)
            b2 = jnp.concatenate([bm, bm], axis=0)
            cb2 = lax.dot_general(cm, b2, (((1,), (1,)), ((), ())), preferred_element_type=F32)
            st_g = st_ref[:, lo:lo + gw]
            y_off = _dot(cm, st_g.astype(BF16)) * e_acs[:, lo:lo + gw]
            pieces = []
            for j in range(gw // LANES):
                l0 = lo + j * LANES
                diff = acs_e[:, l0:l0 + LANES] - rowterm[:, l0:l0 + LANES]
                dec = jnp.where(causal_pair, jnp.exp(diff), 0.0)
                scores = (cb2 * dec).astype(BF16)
                xp = xdt[:, l0:l0 + LANES]
                xblk = jnp.concatenate([jnp.where(left_half, xp, 0.0), jnp.where(left_half, 0.0, xp)],
                                       axis=0).astype(BF16)
                pieces.append(_dot(scores, xblk))
            y_diag = jnp.concatenate(pieces, axis=1)
            upd = lax.dot_general(bm, x_dec[:, lo:lo + gw], (((0,), (0,)), ((), ())),
                                  preferred_element_type=F32)
            st_ref[:, lo:lo + gw] = e_last[:, lo:lo + gw] * st_g + upd
            yg = y_diag + y_off + xs[:, lo:lo + gw] * dsk_e[:, lo:lo + gw]
            zg = z_ref[pl.ds(r0, CHUNK), lo:lo + gw].astype(F32)
            yg = yg * (zg * jax.nn.sigmoid(zg))
            yn = yg * lax.rsqrt(jnp.mean(yg * yg, axis=-1, keepdims=True) + EPS)
            y_parts.append((yn * gn_ref[:, lo:lo + gw]).astype(y_ref.dtype))
        y_ref[pl.ds(r0, CHUNK), :] = jnp.concatenate(y_parts, axis=1)
        return carry

    lax.fori_loop(0, nch, chunk_body, 0)

    @pl.when(t == nt - 1)
    def _():
        hout_ref[0] = st_ref[...]


def ssd_mixer(xbc, z_src, z_col, conv_prev8, h0_t, conv_w, conv_b, dt_bias, a_log, d_skip, ssd_norm,
              b, l, tl, valid_len, out_rows):
    wx = xbc.shape[1]
    assert l % tl == 0 and tl % CHUNK == 0 and CHUNK == SSD_HEAD_DIM
    nt = l // tl
    kern = functools.partial(_ssd_kernel, tl=tl, valid_len=valid_len)
    full = lambda shape: pl.BlockSpec(shape, lambda i, j: (0,) * len(shape))
    return pl.pallas_call(
        kern,
        grid=(b, nt),
        in_specs=[
            pl.BlockSpec((tl, wx), lambda i, j: (i * nt + j, 0)),
            pl.BlockSpec((tl, SSD_D_INNER), lambda i, j: (i * nt + j, z_col)),
            pl.BlockSpec((1, SUBLANES, SSD_CONV_CH), lambda i, j: (i, 0, 0)),
            pl.BlockSpec((1, SSD_D_STATE, SSD_D_INNER), lambda i, j: (i, 0, 0)),
            full((SSD_CONV_WIDTH, SSD_CONV_CH)),
            full((1, SSD_CONV_CH)),
            full((1, SSD_N_HEADS)),
            full((1, SSD_N_HEADS)),
            full((1, SSD_N_HEADS)),
            full((1, SSD_D_INNER)),
        ],
        out_specs=[
            pl.BlockSpec((tl, SSD_D_INNER), lambda i, j: (i * nt + j, 0)),
            pl.BlockSpec((1, SSD_D_STATE, SSD_D_INNER), lambda i, j: (i, 0, 0)),
        ],
        out_shape=[
            jax.ShapeDtypeStruct((out_rows, SSD_D_INNER), BF16),
            jax.ShapeDtypeStruct((b, SSD_D_STATE, SSD_D_INNER), F32),
        ],
        scratch_shapes=[
            pltpu.VMEM((tl + SUBLANES, SSD_CONV_CH), F32),
            pltpu.VMEM((tl, SSD_CONV_CH), F32),
            pltpu.VMEM((SUBLANES, SSD_CONV_CH), F32),
            pltpu.VMEM((SSD_D_STATE, SSD_D_INNER), F32),
        ],
        compiler_params=_cparams(("arbitrary", "arbitrary")),
        name="ssd_mixer",
    )(xbc, z_src, conv_prev8, h0_t, conv_w, conv_b.reshape(1, -1), dt_bias.reshape(1, -1),
      a_log.reshape(1, -1), d_skip.reshape(1, -1), ssd_norm.reshape(1, -1))


def _kv_blocks_needed(qi, tq, tk, q_off, kv_len):
    last_q = q_off + (qi + 1) * tq - 1
    kend = jnp.minimum((last_q // CHUNK + 1) * CHUNK, kv_len)
    return (kend + tk - 1) // tk


def _attn_kernel(q_ref, k_ref, v_ref, slope_ref, lam_ref, sub_ref, o_ref, m_ref, l_ref, acc_ref,
                 *, tq, tk, q_off, kv_len, lambda_init):
    qi = pl.program_id(2)
    ki = pl.program_id(3)
    nkv = _kv_blocks_needed(qi, tq, tk, q_off, kv_len)

    @pl.when(ki == 0)
    def _():
        m_ref[...] = jnp.full(m_ref.shape, NEG_BIG, F32)
        l_ref[...] = jnp.zeros(l_ref.shape, F32)
        acc_ref[...] = jnp.zeros(acc_ref.shape, F32)

    @pl.when(ki < nkv)
    def _():
        q = q_ref[...]
        k = k_ref[...]
        v = v_ref[...]
        lane = lax.broadcasted_iota(jnp.int32, (tq, LANES), 1)
        zero = jnp.zeros_like(q)
        slope = slope_ref[0, :, 0:1]
        qpos = q_off + qi * tq + lax.broadcasted_iota(jnp.int32, (tq, tk), 0)
        kpos = ki * tk + lax.broadcasted_iota(jnp.int32, (tq, tk), 1)
        allowed = jnp.logical_and(kpos // CHUNK <= qpos // CHUNK, kpos < kv_len)
        bias = -slope * jnp.abs(qpos - kpos).astype(F32)
        scale = ATT_HEAD_DIM ** -0.5
        for idx in range(2):
            qh = jnp.where(lane < ATT_HEAD_DIM, q, zero) if idx == 0 else jnp.where(lane < ATT_HEAD_DIM, zero, q)
            s = lax.dot_general(qh, k, (((1,), (1,)), ((), ())), preferred_element_type=F32)
            s = jnp.where(allowed, s * scale + bias, NEG_BIG)
            m_prev = m_ref[idx]
            m_new = jnp.maximum(m_prev, jnp.max(s, axis=-1, keepdims=True))
            alpha = jnp.exp(m_prev - m_new)
            p = jnp.exp(s - m_new)
            l_ref[idx] = alpha * l_ref[idx] + jnp.sum(p, axis=-1, keepdims=True)
            acc_ref[idx] = alpha * acc_ref[idx] + _dot(p.astype(BF16), v)
            m_ref[idx] = m_new

    @pl.when(ki == nkv - 1)
    def _():
        lp = lam_ref[...]
        lam = (jnp.exp(jnp.sum(lp[0:1] * lp[1:2], axis=-1, keepdims=True))
               - jnp.exp(jnp.sum(lp[2:3] * lp[3:4], axis=-1, keepdims=True)) + lambda_init)
        o = acc_ref[0] / l_ref[0] - lam * (acc_ref[1] / l_ref[1])
        on = o * lax.rsqrt(jnp.mean(o * o, axis=-1, keepdims=True) + EPS)
        o_ref[...] = ((on * sub_ref[...]) * (1.0 - lambda_init)).astype(o_ref.dtype)


def diff_attention(q_src, q_blk0, q_col0, k_src, k_col0, v_src, v_col0, lam_rows, subln, *, b, lq, lk, tq, tk,
                   q_off, kv_len, lambda_init, out_rows):
    assert lq % tq == 0 and lk % tk == 0
    nq, nk = lq // tq, lk // tk
    kern = functools.partial(_attn_kernel, tq=tq, tk=tk, q_off=q_off, kv_len=kv_len, lambda_init=lambda_init)
    slopes = jnp.exp2(-ALIBI_MAX_BIAS * jnp.arange(1, ATT_N_HEADS + 1, dtype=F32) / ATT_N_HEADS)
    slopes = jnp.broadcast_to(slopes[:, None, None], (ATT_N_HEADS, 1, LANES))

    def kv_map(col0):
        def m(bi, h, qi, ki):
            last = _kv_blocks_needed(qi, tq, tk, q_off, kv_len) - 1
            return (bi * nk + jnp.minimum(ki, last), col0 + h)
        return m

    return pl.pallas_call(
        kern,
        grid=(b, ATT_N_HEADS, nq, nk),
        in_specs=[
            pl.BlockSpec((tq, LANES), lambda bi, h, qi, ki: (q_blk0 + bi * nq + qi, q_col0 + h)),
            pl.BlockSpec((tk, LANES), kv_map(k_col0)),
            pl.BlockSpec((tk, LANES), kv_map(v_col0)),
            pl.BlockSpec((1, 1, LANES), lambda bi, h, qi, ki: (h, 0, 0)),
            pl.BlockSpec((SUBLANES, ATT_HEAD_DIM), lambda bi, h, qi, ki: (0, 0)),
            pl.BlockSpec((1, ATT_V_HEAD), lambda bi, h, qi, ki: (0, 0)),
        ],
        out_specs=pl.BlockSpec((tq, LANES), lambda bi, h, qi, ki: (bi * nq + qi, h)),
        out_shape=jax.ShapeDtypeStruct((out_rows, ATT_N_HEADS * ATT_V_HEAD), BF16),
        scratch_shapes=[
            pltpu.VMEM((2, tq, 1), F32),
            pltpu.VMEM((2, tq, 1), F32),
            pltpu.VMEM((2, tq, ATT_V_HEAD), F32),
        ],
        compiler_params=_cparams(("arbitrary", "arbitrary", "arbitrary", "arbitrary")),
        name="diff_attention",
    )(q_src, k_src, v_src, slopes, lam_rows, subln.reshape(1, -1))


def _mixer_out_kernel(y_ref, o_ref, gs_ref, ga_ref, x_ref, wos_ref, woa_ref, wout_ref, nf_ref, rw_ref, rb_ref,
                      x2_ref, h_ref, gate_ref, sel_ref, cnt_ref, run_ref, *, tm):
    i = pl.program_id(0)

    @pl.when(i == 0)
    def _():
        run_ref[...] = jnp.zeros(run_ref.shape, F32)

    o_ssd = _dot(y_ref[...], wos_ref[...])
    o_att = _dot(o_ref[...], woa_ref[...])
    merged = (jax.nn.sigmoid(gs_ref[...].astype(F32)) * o_ssd
              + jax.nn.sigmoid(ga_ref[...].astype(F32)) * o_att)
    x2 = x_ref[...] + _dot(merged.astype(BF16), wout_ref[...])
    x2_ref[...] = x2
    hn = x2 * lax.rsqrt(jnp.mean(x2 * x2, axis=-1, keepdims=True) + EPS) * nf_ref[...]
    h_ref[...] = hn

    a1, a2, a3 = _split3(hn)
    w1, w2, w3 = _split3(rw_ref[...])
    logits = (_dot(a1, w1) + _dot(a1, w2) + _dot(a2, w1) + _dot(a2, w2) + _dot(a1, w3) + _dot(a3, w1)
              + rb_ref[...])
    lane = lax.broadcasted_iota(jnp.int32, (tm, LANES), 1)
    work = jnp.where(lane < N_EXPERTS, logits, -jnp.inf)
    tops, idxs = [], []
    for _ in range(TOP_K):
        mx = jnp.max(work, axis=-1, keepdims=True)
        ix = jnp.min(jnp.where(work == mx, lane, LANES), axis=-1, keepdims=True)
        tops.append(mx)
        idxs.append(ix)
        work = jnp.where(lane == ix, -jnp.inf, work)
    es = [jnp.exp(tv - tops[0]) for tv in tops]
    den = es[0] + es[1] + es[2] + es[3]
    gates = jnp.zeros((tm, LANES), F32)
    for k in range(TOP_K):
        gates = jnp.where(lane == k, es[k] / den, gates)
    gate_ref[...] = gates

    chosen = jnp.zeros((tm, LANES), jnp.bool_)
    for k in range(TOP_K):
        chosen = jnp.logical_or(chosen, lane == idxs[k])
    multihot = jnp.where(chosen, 1.0, 0.0).astype(BF16)
    ri = lax.broadcasted_iota(jnp.int32, (tm, tm), 0)
    ci = lax.broadcasted_iota(jnp.int32, (tm, tm), 1)
    strict = jnp.where(ci < ri, 1.0, 0.0).astype(BF16)
    prefix = _dot(strict, multihot) + run_ref[...]
    sel = jnp.zeros((tm, LANES), jnp.int32)
    for k in range(TOP_K):
        rank = jnp.sum(jnp.where(lane == idxs[k], prefix, 0.0), axis=-1, keepdims=True)
        sel = jnp.where(lane == k, idxs[k], sel)
        sel = jnp.where(lane == TOP_K + k, rank.astype(jnp.int32), sel)
    sel_ref[...] = sel
    run_ref[...] = run_ref[...] + jnp.sum(multihot.astype(F32), axis=0, keepdims=True)
    cnt_ref[...] = run_ref[...].astype(jnp.int32)


def mixer_out(y_ssd, o_att, gates_src, gs_col, ga_col, x, w_o_ssd, w_o_att, w_out, norm_ffn, router_w, router_b, tm):
    t, d = x.shape
    assert t % tm == 0
    kern = functools.partial(_mixer_out_kernel, tm=tm)
    rw = jnp.zeros((d, LANES), F32).at[:, :N_EXPERTS].set(router_w)
    rb = jnp.zeros((1, LANES), F32).at[0, :N_EXPERTS].set(router_b)
    const = lambda shape: pl.BlockSpec(shape, lambda i: (0, 0))
    return pl.pallas_call(
        kern,
        grid=(t // tm,),
        in_specs=[
            pl.BlockSpec((tm, SSD_D_INNER), lambda i: (i, 0)),
            pl.BlockSpec((tm, d), lambda i: (i, 0)),
            pl.BlockSpec((tm, d), lambda i: (i, gs_col)),
            pl.BlockSpec((tm, d), lambda i: (i, ga_col)),
            pl.BlockSpec((tm, d), lambda i: (i, 0)),
            const((SSD_D_INNER, d)),
            const((d, d)),
            const((d, d)),
            const((1, d)),
            const((d, LANES)),
            const((1, LANES)),
        ],
        out_specs=[
            pl.BlockSpec((tm, d), lambda i: (i, 0)),
            pl.BlockSpec((tm, d), lambda i: (i, 0)),
            pl.BlockSpec((tm, LANES), lambda i: (i, 0)),
            pl.BlockSpec((tm, LANES), lambda i: (i, 0)),
            pl.BlockSpec((1, LANES), lambda i: (0, 0)),
        ],
        out_shape=[
            jax.ShapeDtypeStruct((t, d), F32),
            jax.ShapeDtypeStruct((t, d), F32),
            jax.ShapeDtypeStruct((t, LANES), F32),
            jax.ShapeDtypeStruct((t, LANES), jnp.int32),
            jax.ShapeDtypeStruct((1, LANES), jnp.int32),
        ],
        scratch_shapes=[pltpu.VMEM((1, LANES), F32)],
        compiler_params=_cparams(("arbitrary",)),
        name="mixer_out",
    )(y_ssd, o_att, gates_src, gates_src, x, w_o_ssd, w_o_att, w_out, norm_ffn.reshape(1, d), rw, rb)


def _moe_kernel(be_ref, nb_ref, x_ref, wgu_ref, bgu_ref, wd_ref, bd_ref, o_ref):
    b = pl.program_id(0)

    @pl.when(b < nb_ref[0])
    def _():
        x = x_ref[...].astype(BF16)
        gu = _dot(x, wgu_ref[0]) + bgu_ref[0]
        d_ff = gu.shape[1] // 2
        gate = jnp.minimum(gu[:, :d_ff], SWIGLU_LIMIT)
        up = jnp.clip(gu[:, d_ff:], -SWIGLU_LIMIT, SWIGLU_LIMIT)
        act = (up + 1.0) * gate * jax.nn.sigmoid(SWIGLU_ALPHA * gate)
        o_ref[...] = _dot(act.astype(BF16), wd_ref[0]) + bd_ref[0]


def moe_experts(xs, block_e, n_used, w_gu, b_gu, w_down, b_down, rb):
    rows, d = xs.shape
    nb = rows // rb
    e, _, gu_w = w_gu.shape
    return pl.pallas_call(
        _moe_kernel,
        grid_spec=pltpu.PrefetchScalarGridSpec(
            num_scalar_prefetch=2,
            grid=(nb,),
            in_specs=[
                pl.BlockSpec((rb, d), lambda b, be, nu: (b, 0)),
                pl.BlockSpec((1, d, gu_w), lambda b, be, nu: (be[b], 0, 0)),
                pl.BlockSpec((1, 1, gu_w), lambda b, be, nu: (be[b], 0, 0)),
                pl.BlockSpec((1, gu_w // 2, d), lambda b, be, nu: (be[b], 0, 0)),
                pl.BlockSpec((1, 1, d), lambda b, be, nu: (be[b], 0, 0)),
            ],
            out_specs=pl.BlockSpec((rb, d), lambda b, be, nu: (b, 0)),
        ),
        out_shape=jax.ShapeDtypeStruct((rows, d), F32),
        compiler_params=_cparams(("arbitrary",)),
        name="moe_experts",
    )(block_e, n_used, xs, w_gu, b_gu.reshape(e, 1, gu_w), w_down, b_down.reshape(e, 1, d))


def _combine_kernel(x2_ref, ys_ref, gate_ref, g_ref, o_ref):
    acc = x2_ref[...]
    gates = gate_ref[...]
    moe = jnp.zeros_like(acc)
    d = acc.shape[1]
    for k in range(TOP_K):
        moe = moe + ys_ref[:, k * d:(k + 1) * d] * gates[:, k:k + 1]
    tok = acc + moe
    y = tok * lax.rsqrt(jnp.mean(tok * tok, axis=-1, keepdims=True) + EPS)
    o_ref[...] = y * g_ref[...]


def combine(x2, ys4, gates, norm_final, tm):
    t, d = x2.shape
    return pl.pallas_call(
        _combine_kernel,
        grid=(t // tm,),
        in_specs=[
            pl.BlockSpec((tm, d), lambda i: (i, 0)),
            pl.BlockSpec((tm, TOP_K * d), lambda i: (i, 0)),
            pl.BlockSpec((tm, LANES), lambda i: (i, 0)),
            pl.BlockSpec((1, d), lambda i: (0, 0)),
        ],
        out_specs=pl.BlockSpec((tm, d), lambda i: (i, 0)),
        out_shape=jax.ShapeDtypeStruct((t, d), F32),
        compiler_params=_cparams(("arbitrary",)),
        name="combine",
    )(x2, ys4, gates, norm_final.reshape(1, d))


MOE_ROWS = 256


def kernel(x_prompt, x_sample, cache_k, cache_v, state_ssm, state_conv, norm_mix, w_in, conv_w, conv_b, dt_bias, a_log, d_skip, ssd_norm, w_o_ssd, lambda_q1, lambda_k1, lambda_q2, lambda_k2, subln, w_o_att, w_out, norm_ffn, router_w, router_b, w_gu, b_gu, w_down, b_down, norm_final):
    bp, s, d = x_prompt.shape
    bs, ts, _ = x_sample.shape
    past = cache_k.shape[2]
    depth = w_in.shape[0]
    assert depth == 1
    layer = 0
    lambda_init = 0.8 - 0.6 * math.exp(-0.3 * layer)
    tp, tsn = bp * s, bs * ts
    t_all = tp + tsn

    x_all = jnp.concatenate([x_prompt.reshape(tp, d), x_sample.reshape(tsn, d)], axis=0)

    sizes = (SSD_D_INNER, SSD_CONV_CH, SSD_N_HEADS, d, d, d, d, d)
    offs = [0]
    for sz in sizes:
        offs.append(offs[-1] + sz)
    w = w_in[layer]
    seg = lambda i: w[:, offs[i]:offs[i + 1]]
    w_kv = jnp.concatenate([seg(4), seg(5)], axis=1).astype(BF16)
    w_zq = jnp.concatenate([seg(0), seg(3), seg(6), seg(7)], axis=1).astype(BF16)
    w_xd = jnp.concatenate([seg(1), seg(2), jnp.zeros((d, LANES - SSD_N_HEADS), F32)], axis=1).astype(BF16)
    tm = 1280
    kv = norm_matmul(x_all, norm_mix[layer], w_kv, F32, tm, 512)
    zq = norm_matmul(x_all, norm_mix[layer], w_zq, BF16, tm, 512)
    xd = norm_matmul(x_all, norm_mix[layer], w_xd, F32, tm, 640)
    kv16 = kv.astype(BF16)

    prev_p = jnp.zeros((bp, SUBLANES, SSD_CONV_CH), F32)
    h0_p = jnp.zeros((bp, SSD_D_STATE, SSD_D_INNER), F32)
    ssd_w = (conv_w[layer], conv_b[layer], dt_bias[layer], a_log[layer], d_skip[layer], ssd_norm[layer])
    y_p, hT_p = ssd_mixer(xd, zq, 0, prev_p, h0_p, *ssd_w, b=bp, l=s, tl=256, valid_len=None, out_rows=t_all)

    pad_rows = CHUNK - ts
    xd_s = jnp.pad(xd[tp:].reshape(bs, ts, -1), ((0, 0), (0, pad_rows), (0, 0))).reshape(bs * CHUNK, -1)
    z_s = jnp.pad(zq[tp:, :SSD_D_INNER].reshape(bs, ts, -1), ((0, 0), (0, pad_rows), (0, 0))).reshape(bs * CHUNK, -1)
    prev_s = jnp.pad(state_conv[layer], ((0, 0), (SUBLANES - (SSD_CONV_WIDTH - 1), 0), (0, 0)))
    h0_s = jnp.swapaxes(state_ssm[layer].reshape(bs, SSD_D_INNER, SSD_D_STATE), 1, 2)
    y_s, hT_s = ssd_mixer(xd_s, z_s, 0, prev_s, h0_s, *ssd_w, b=bs, l=CHUNK, tl=CHUNK, valid_len=ts,
                          out_rows=bs * CHUNK)
    y_ssd = lax.dynamic_update_slice(y_p, y_s.reshape(bs, CHUNK, -1)[:, :ts].reshape(tsn, -1), (tp, 0))

    lam_rows = jnp.concatenate([lambda_q1[layer][None], lambda_k1[layer][None], lambda_q2[layer][None],
                                lambda_k2[layer][None], jnp.zeros((4, ATT_HEAD_DIM), F32)], axis=0)
    heads = ATT_N_HEADS
    q_col0 = SSD_D_INNER // LANES
    o_p = diff_attention(zq, 0, q_col0, kv16, 0, kv16, heads, lam_rows, subln[layer], b=bp, lq=s, lk=s,
                         tq=256, tk=512, q_off=0, kv_len=s, lambda_init=lambda_init, out_rows=t_all)
    kv_s = kv16[tp:].reshape(bs, ts, -1)
    kv_len = past + ts
    kv_pad = (-kv_len) % LANES
    lk_s = kv_len + kv_pad
    k_all = jnp.concatenate([cache_k[layer].reshape(bs, past, -1).astype(BF16), kv_s[:, :, :d],
                             jnp.zeros((bs, kv_pad, d), BF16)], axis=1).reshape(bs * lk_s, d)
    v_all = jnp.concatenate([cache_v[layer].reshape(bs, past, -1).astype(BF16), kv_s[:, :, d:],
                             jnp.zeros((bs, kv_pad, d), BF16)], axis=1).reshape(bs * lk_s, d)
    o_s = diff_attention(zq, tp // ts, q_col0, k_all, 0, v_all, 0, lam_rows, subln[layer], b=bs, lq=ts, lk=lk_s,
                         tq=ts, tk=lk_s, q_off=past, kv_len=kv_len, lambda_init=lambda_init, out_rows=tsn)
    o_att = lax.dynamic_update_slice(o_p, o_s, (tp, 0))

    x2, hn, gates, sel, counts = mixer_out(
        y_ssd, o_att, zq, 3, 4, x_all, w_o_ssd[layer].astype(BF16), w_o_att[layer].astype(BF16),
        w_out[layer].astype(BF16), norm_ffn[layer], router_w[layer], router_b[layer], tm=640)

    top_e = sel[:, :TOP_K]
    rank = sel[:, TOP_K:2 * TOP_K]
    cnt = counts[0, :N_EXPERTS]
    padded = (cnt + MOE_ROWS - 1) // MOE_ROWS * MOE_ROWS
    ends = jnp.cumsum(padded)
    starts = ends - padded
    dest = starts[top_e] + rank
    nb = (t_all * TOP_K + N_EXPERTS * (MOE_ROWS - 1) + MOE_ROWS - 1) // MOE_ROWS
    block_start = jnp.arange(nb, dtype=jnp.int32) * MOE_ROWS
    block_e = jnp.minimum(jnp.sum((ends[None, :] <= block_start[:, None]).astype(jnp.int32), axis=1),
                          N_EXPERTS - 1)
    n_used = (ends[-1] // MOE_ROWS).astype(jnp.int32).reshape(1)
    rows_tok = jnp.full((nb * MOE_ROWS,), t_all, jnp.int32).at[dest.reshape(-1)].set(
        jnp.arange(t_all * TOP_K, dtype=jnp.int32) // TOP_K)
    h_pad = jnp.concatenate([hn, jnp.zeros((1, d), F32)], axis=0)
    xs_sorted = h_pad[rows_tok]
    ys = moe_experts(xs_sorted, block_e, n_used, w_gu[layer].astype(BF16), b_gu[layer],
                     w_down[layer].astype(BF16), b_down[layer], MOE_ROWS)
    ys4 = ys[dest.reshape(-1)].reshape(t_all, TOP_K * d)
    y_all = combine(x2, ys4, gates, norm_final, tm=320)

    y_prompt = y_all[:tp].reshape(bp, s, d)
    y_sample = y_all[tp:].reshape(bs, ts, d)
    k_f, v_f = kv[:, :d], kv[:, d:]
    new_k_p = k_f[:tp].reshape(1, bp, s, heads, 2 * ATT_HEAD_DIM)
    new_v_p = v_f[:tp].reshape(1, bp, s, heads, ATT_V_HEAD)
    new_k_s = k_f[tp:].reshape(1, bs, ts, heads, 2 * ATT_HEAD_DIM)
    new_v_s = v_f[tp:].reshape(1, bs, ts, heads, ATT_V_HEAD)
    ssm_p = jnp.swapaxes(hT_p, 1, 2).reshape(1, bp, SSD_N_HEADS, SSD_HEAD_DIM, SSD_D_STATE)
    ssm_s = jnp.swapaxes(hT_s, 1, 2).reshape(1, bs, SSD_N_HEADS, SSD_HEAD_DIM, SSD_D_STATE)
    keep = SSD_CONV_WIDTH - 1
    conv_p = xd[:tp].reshape(bp, s, -1)[:, s - keep:, :SSD_CONV_CH][None]
    raw_s = jnp.concatenate([state_conv[layer], xd[tp:].reshape(bs, ts, -1)[:, :, :SSD_CONV_CH]], axis=1)
    conv_s = raw_s[:, -keep:][None]
    return (y_prompt, y_sample, new_k_p, new_v_p, ssm_p, conv_p, new_k_s, new_v_s, ssm_s, conv_s)
```

```python
import functools
import math

import jax
import jax.numpy as jnp
from jax import lax
from jax.experimental import pallas as pl
from jax.experimental.pallas import tpu as pltpu

EPS = 1e-5
CHUNK = 64
D_MODEL = 1024
SSD_D_INNER = 2048
SSD_HEAD_DIM = 64
SSD_N_HEADS = 32
SSD_GROUPS = 4
SSD_D_STATE = 128
SSD_CONV_WIDTH = 4
SSD_CONV_CH = 3072
ATT_HEAD_DIM = 64
ATT_N_HEADS = 8
ATT_V_HEAD = 128
ALIBI_MAX_BIAS = 8.0
N_EXPERTS = 32
TOP_K = 4
SWIGLU_LIMIT = 7.0
SWIGLU_ALPHA = 1.702

LANES = 128
SUBLANES = 8
VMEM_LIMIT = 56 * 1024 * 1024
NEG_BIG = -1e30

BF16 = jnp.bfloat16
F32 = jnp.float32


def _cparams(sem, flags=None):
    return pltpu.CompilerParams(dimension_semantics=sem, vmem_limit_bytes=VMEM_LIMIT, flags=flags)


def _split3(x):
    h1 = x.astype(BF16)
    r1 = x - h1.astype(F32)
    h2 = r1.astype(BF16)
    h3 = (r1 - h2.astype(F32)).astype(BF16)
    return h1, h2, h3


def _dot(a, b):
    return jnp.dot(a, b, preferred_element_type=F32)


def _dot_exact_rhs(x, m):
    h1, h2, h3 = _split3(x)
    return _dot(h1, m) + _dot(h2, m) + _dot(h3, m)


def _dot_exact_lhs(m, x):
    h1, h2, h3 = _split3(x)
    return _dot(m, h1) + _dot(m, h2) + _dot(m, h3)


def _norm_matmul_kernel(x_ref, g_ref, w_ref, o_ref, xn_ref):
    @pl.when(pl.program_id(1) == 0)
    def _():
        x = x_ref[...]
        y = x * lax.rsqrt(jnp.mean(x * x, axis=-1, keepdims=True) + EPS)
        xn_ref[...] = (y * g_ref[...]).astype(BF16)

    o_ref[...] = _dot(xn_ref[...], w_ref[...]).astype(o_ref.dtype)


def norm_matmul(x, gain, w, out_dtype, tm, tn):
    t, d = x.shape
    n = w.shape[1]
    assert t % tm == 0 and n % tn == 0
    return pl.pallas_call(
        _norm_matmul_kernel,
        grid=(t // tm, n // tn),
        in_specs=[
            pl.BlockSpec((tm, d), lambda i, j: (i, 0)),
            pl.BlockSpec((1, d), lambda i, j: (0, 0)),
            pl.BlockSpec((d, tn), lambda i, j: (0, j)),
        ],
        out_specs=pl.BlockSpec((tm, tn), lambda i, j: (i, j)),
        out_shape=jax.ShapeDtypeStruct((t, n), out_dtype),
        scratch_shapes=[pltpu.VMEM((tm, d), BF16)],
        compiler_params=_cparams(("arbitrary", "arbitrary")),
        name="norm_matmul",
    )(x, gain.reshape(1, d), w)


def _ssd_kernel(xbc_ref, z_ref, prev_ref, h0_ref, cw_ref, cb_ref, dtb_ref, alog_ref, dsk_ref, gn_ref,
                y_ref, hout_ref, xpad_ref, xc_ref, tail_ref, st_ref, *, tl, valid_len):
    t = pl.program_id(1)
    nt = pl.num_programs(1)
    nch = tl // CHUNK
    dinner = SSD_D_INNER
    gw = dinner // SSD_GROUPS
    ns = SSD_D_STATE

    @pl.when(t == 0)
    def _():
        tail_ref[...] = prev_ref[0]
        st_ref[...] = h0_ref[0]

    xpad_ref[0:SUBLANES, :] = tail_ref[...]
    xpad_ref[SUBLANES:SUBLANES + tl, :] = xbc_ref[:, 0:SSD_CONV_CH]
    tail_ref[...] = xbc_ref[tl - SUBLANES:tl, 0:SSD_CONV_CH]
    acc = cb_ref[...]
    for k in range(SSD_CONV_WIDTH):
        off = SUBLANES - (SSD_CONV_WIDTH - 1) + k
        acc = acc + xpad_ref[off:off + tl, :] * cw_ref[k:k + 1, :]
    xc_ref[...] = acc * jax.nn.sigmoid(acc)

    head_of_lane = lax.broadcasted_iota(jnp.int32, (SSD_N_HEADS, dinner), 1) // SSD_HEAD_DIM
    expand = (head_of_lane == lax.broadcasted_iota(jnp.int32, (SSD_N_HEADS, dinner), 0)).astype(BF16)
    ti = lax.broadcasted_iota(jnp.int32, (CHUNK, CHUNK), 0)
    si = lax.broadcasted_iota(jnp.int32, (CHUNK, CHUNK), 1)
    tril = (si <= ti).astype(BF16)
    row_c = lax.broadcasted_iota(jnp.int32, (CHUNK, dinner), 0)
    pos_in_head = lax.broadcasted_iota(jnp.int32, (CHUNK, dinner), 1) % CHUNK
    upper = row_c <= pos_in_head
    row_p = lax.broadcasted_iota(jnp.int32, (CHUNK, LANES), 0)
    lane_p = lax.broadcasted_iota(jnp.int32, (CHUNK, LANES), 1)
    causal_pair = (lane_p % CHUNK) <= row_p
    left_half = lane_p < SSD_HEAD_DIM
    a_neg = -jnp.exp(alog_ref[...])
    dsk_e = _dot_exact_rhs(dsk_ref[...], expand)

    def chunk_body(c, carry):
        r0 = pl.multiple_of(c * CHUNK, CHUNK)
        xs = xc_ref[pl.ds(r0, CHUNK), 0:dinner]
        dt_raw = xbc_ref[pl.ds(r0, CHUNK), SSD_CONV_CH:SSD_CONV_CH + SSD_N_HEADS]
        dtv = dt_raw + dtb_ref[...]
        dt = jnp.maximum(dtv, 0.0) + jnp.log1p(jnp.exp(-jnp.abs(dtv)))
        if valid_len is not None:
            rows = t * tl + r0 + lax.broadcasted_iota(jnp.int32, (CHUNK, SSD_N_HEADS), 0)
            dt = jnp.where(rows < valid_len, dt, 0.0)
        a = dt * a_neg
        a_e = _dot_exact_rhs(a, expand)
        dt_e = _dot_exact_rhs(dt, expand)
        acs_e = _dot_exact_lhs(tril, a_e)
        rowterm = jnp.sum(jnp.where(upper, a_e, 0.0), axis=0, keepdims=True)
        acs_last = acs_e[CHUNK - 1:CHUNK, :]
        xdt = xs * dt_e
        x_dec = (xdt * jnp.exp(acs_last - acs_e)).astype(BF16)
        e_acs = jnp.exp(acs_e)
        e_last = jnp.exp(acs_last)

        y_parts = []
        for g in range(SSD_GROUPS):
            lo = g * gw
            bm = xc_ref[pl.ds(r0, CHUNK), dinner + g * ns:dinner + (g + 1) * ns].astype(BF16)
            cm = xc_ref[pl.ds(r0, CHUNK), dinner + SSD_GROUPS * ns + g * ns:
                        dinner + SSD_GROUPS * ns + (g + 1) * ns].astype(BF16)
            b2 = jnp.concatenate([bm, bm], axis=0)
            cb2 = lax.dot_general(cm, b2, (((1,), (1,)), ((), ())), preferred_element_type=F32)
            st_g = st_ref[:, lo:lo + gw]
            y_off = _dot(cm, st_g.astype(BF16)) * e_acs[:, lo:lo + gw]
            pieces = []
            for j in range(gw // LANES):
                l0 = lo + j * LANES
                diff = acs_e[:, l0:l0 + LANES] - rowterm[:, l0:l0 + LANES]
                dec = jnp.where(causal_pair, jnp.exp(diff), 0.0)
                scores = (cb2 * dec).astype(BF16)
                xp = xdt[:, l0:l0 + LANES]
                xblk = jnp.concatenate([jnp.where(left_half, xp, 0.0), jnp.where(left_half, 0.0, xp)],
                                       axis=0).astype(BF16)
                pieces.append(_dot(scores, xblk))
            y_diag = jnp.concatenate(pieces, axis=1)
            upd = lax.dot_general(bm, x_dec[:, lo:lo + gw], (((0,), (0,)), ((), ())),
                                  preferred_element_type=F32)
            st_ref[:, lo:lo + gw] = e_last[:, lo:lo + gw] * st_g + upd
            yg = y_diag + y_off + xs[:, lo:lo + gw] * dsk_e[:, lo:lo + gw]
            zg = z_ref[pl.ds(r0, CHUNK), lo:lo + gw].astype(F32)
            yg = yg * (zg * jax.nn.sigmoid(zg))
            yn = yg * lax.rsqrt(jnp.mean(yg * yg, axis=-1, keepdims=True) + EPS)
            y_parts.append((yn * gn_ref[:, lo:lo + gw]).astype(y_ref.dtype))
        y_ref[pl.ds(r0, CHUNK), :] = jnp.concatenate(y_parts, axis=1)
        return carry

    lax.fori_loop(0, nch, chunk_body, 0)

    @pl.when(t == nt - 1)
    def _():
        hout_ref[0] = st_ref[...]


def ssd_mixer(xbc, z_src, z_col, conv_prev8, h0_t, conv_w, conv_b, dt_bias, a_log, d_skip, ssd_norm,
              b, l, tl, valid_len, out_rows):
    wx = xbc.shape[1]
    assert l % tl == 0 and tl % CHUNK == 0 and CHUNK == SSD_HEAD_DIM
    nt = l // tl
    kern = functools.partial(_ssd_kernel, tl=tl, valid_len=valid_len)
    full = lambda shape: pl.BlockSpec(shape, lambda i, j: (0,) * len(shape))
    return pl.pallas_call(
        kern,
        grid=(b, nt),
        in_specs=[
            pl.BlockSpec((tl, wx), lambda i, j: (i * nt + j, 0)),
            pl.BlockSpec((tl, SSD_D_INNER), lambda i, j: (i * nt + j, z_col)),
            pl.BlockSpec((1, SUBLANES, SSD_CONV_CH), lambda i, j: (i, 0, 0)),
            pl.BlockSpec((1, SSD_D_STATE, SSD_D_INNER), lambda i, j: (i, 0, 0)),
            full((SSD_CONV_WIDTH, SSD_CONV_CH)),
            full((1, SSD_CONV_CH)),
            full((1, SSD_N_HEADS)),
            full((1, SSD_N_HEADS)),
            full((1, SSD_N_HEADS)),
            full((1, SSD_D_INNER)),
        ],
        out_specs=[
            pl.BlockSpec((tl, SSD_D_INNER), lambda i, j: (i * nt + j, 0)),
            pl.BlockSpec((1, SSD_D_STATE, SSD_D_INNER), lambda i, j: (i, 0, 0)),
        ],
        out_shape=[
            jax.ShapeDtypeStruct((out_rows, SSD_D_INNER), BF16),
            jax.ShapeDtypeStruct((b, SSD_D_STATE, SSD_D_INNER), F32),
        ],
        scratch_shapes=[
            pltpu.VMEM((tl + SUBLANES, SSD_CONV_CH), F32),
            pltpu.VMEM((tl, SSD_CONV_CH), F32),
            pltpu.VMEM((SUBLANES, SSD_CONV_CH), F32),
            pltpu.VMEM((SSD_D_STATE, SSD_D_INNER), F32),
        ],
        compiler_params=_cparams(("arbitrary", "arbitrary")),
        name="ssd_mixer",
    )(xbc, z_src, conv_prev8, h0_t, conv_w, conv_b.reshape(1, -1), dt_bias.reshape(1, -1),
      a_log.reshape(1, -1), d_skip.reshape(1, -1), ssd_norm.reshape(1, -1))


def _kv_blocks_needed(qi, tq, tk, q_off, kv_len):
    last_q = q_off + (qi + 1) * tq - 1
    kend = jnp.minimum((last_q // CHUNK + 1) * CHUNK, kv_len)
    return (kend + tk - 1) // tk


def _attn_kernel(q_ref, k_ref, v_ref, slope_ref, lam_ref, sub_ref, o_ref, m_ref, l_ref, acc_ref,
                 *, tq, tk, q_off, kv_len, lambda_init):
    qi = pl.program_id(2)
    ki = pl.program_id(3)
    nkv = _kv_blocks_needed(qi, tq, tk, q_off, kv_len)

    @pl.when(ki == 0)
    def _():
        m_ref[...] = jnp.full(m_ref.shape, NEG_BIG, F32)
        l_ref[...] = jnp.zeros(l_ref.shape, F32)
        acc_ref[...] = jnp.zeros(acc_ref.shape, F32)

    @pl.when(ki < nkv)
    def _():
        q = q_ref[...]
        k = k_ref[...]
        v = v_ref[...]
        lane = lax.broadcasted_iota(jnp.int32, (tq, LANES), 1)
        zero = jnp.zeros_like(q)
        slope = slope_ref[0, :, 0:1]
        qpos = q_off + qi * tq + lax.broadcasted_iota(jnp.int32, (tq, tk), 0)
        kpos = ki * tk + lax.broadcasted_iota(jnp.int32, (tq, tk), 1)
        allowed = jnp.logical_and(kpos // CHUNK <= qpos // CHUNK, kpos < kv_len)
        bias = -slope * jnp.abs(qpos - kpos).astype(F32)
        scale = ATT_HEAD_DIM ** -0.5
        for idx in range(2):
            qh = jnp.where(lane < ATT_HEAD_DIM, q, zero) if idx == 0 else jnp.where(lane < ATT_HEAD_DIM, zero, q)
            s = lax.dot_general(qh, k, (((1,), (1,)), ((), ())), preferred_element_type=F32)
            s = jnp.where(allowed, s * scale + bias, NEG_BIG)
            m_prev = m_ref[idx]
            m_new = jnp.maximum(m_prev, jnp.max(s, axis=-1, keepdims=True))
            alpha = jnp.exp(m_prev - m_new)
            p = jnp.exp(s - m_new)
            l_ref[idx] = alpha * l_ref[idx] + jnp.sum(p, axis=-1, keepdims=True)
            acc_ref[idx] = alpha * acc_ref[idx] + _dot(p.astype(BF16), v)
            m_ref[idx] = m_new

    @pl.when(ki == nkv - 1)
    def _():
        lp = lam_ref[...]
        lam = (jnp.exp(jnp.sum(lp[0:1] * lp[1:2], axis=-1, keepdims=True))
               - jnp.exp(jnp.sum(lp[2:3] * lp[3:4], axis=-1, keepdims=True)) + lambda_init)
        o = acc_ref[0] / l_ref[0] - lam * (acc_ref[1] / l_ref[1])
        on = o * lax.rsqrt(jnp.mean(o * o, axis=-1, keepdims=True) + EPS)
        o_ref[...] = ((on * sub_ref[...]) * (1.0 - lambda_init)).astype(o_ref.dtype)


def diff_attention(q_src, q_blk0, q_col0, k_src, k_col0, v_src, v_col0, lam_rows, subln, *, b, lq, lk, tq, tk,
                   q_off, kv_len, lambda_init, out_rows):
    assert lq % tq == 0 and lk % tk == 0
    nq, nk = lq // tq, lk // tk
    kern = functools.partial(_attn_kernel, tq=tq, tk=tk, q_off=q_off, kv_len=kv_len, lambda_init=lambda_init)
    slopes = jnp.exp2(-ALIBI_MAX_BIAS * jnp.arange(1, ATT_N_HEADS + 1, dtype=F32) / ATT_N_HEADS)
    slopes = jnp.broadcast_to(slopes[:, None, None], (ATT_N_HEADS, 1, LANES))

    def kv_map(col0):
        def m(bi, h, qi, ki):
            last = _kv_blocks_needed(qi, tq, tk, q_off, kv_len) - 1
            return (bi * nk + jnp.minimum(ki, last), col0 + h)
        return m

    return pl.pallas_call(
        kern,
        grid=(b, ATT_N_HEADS, nq, nk),
        in_specs=[
            pl.BlockSpec((tq, LANES), lambda bi, h, qi, ki: (q_blk0 + bi * nq + qi, q_col0 + h)),
            pl.BlockSpec((tk, LANES), kv_map(k_col0)),
            pl.BlockSpec((tk, LANES), kv_map(v_col0)),
            pl.BlockSpec((1, 1, LANES), lambda bi, h, qi, ki: (h, 0, 0)),
            pl.BlockSpec((SUBLANES, ATT_HEAD_DIM), lambda bi, h, qi, ki: (0, 0)),
            pl.BlockSpec((1, ATT_V_HEAD), lambda bi, h, qi, ki: (0, 0)),
        ],
        out_specs=pl.BlockSpec((tq, LANES), lambda bi, h, qi, ki: (bi * nq + qi, h)),
        out_shape=jax.ShapeDtypeStruct((out_rows, ATT_N_HEADS * ATT_V_HEAD), BF16),
        scratch_shapes=[
            pltpu.VMEM((2, tq, 1), F32),
            pltpu.VMEM((2, tq, 1), F32),
            pltpu.VMEM((2, tq, ATT_V_HEAD), F32),
        ],
        compiler_params=_cparams(("arbitrary", "arbitrary", "arbitrary", "arbitrary")),
        name="diff_attention",
    )(q_src, k_src, v_src, slopes, lam_rows, subln.reshape(1, -1))


BIAS_SPLIT = 32
FLAG_FIRST, FLAG_LAST, FLAG_MASK = 1, 2, 4
HEADS_PER_GROUP = 2


def _attn_t_kernel(qi_ref, ki_ref, fl_ref, q_ref, k_ref, v_ref, slope_ref, lam_ref, sub_ref, o_ref,
                   m_ref, l_ref, acc_ref, kb_ref, corr_ref, *, tq, tk, q_off, kv_len, lambda_init):
    p_idx = pl.program_id(1)
    qi = qi_ref[p_idx]
    ki = ki_ref[p_idx]
    flags = fl_ref[p_idx]
    qstart = q_off + qi * tq
    kstart = ki * tk
    half = ATT_HEAD_DIM
    scale = ATT_HEAD_DIM ** -0.5

    @pl.when((flags & FLAG_FIRST) != 0)
    def _():
        m_ref[...] = jnp.full(m_ref.shape, NEG_BIG, F32)
        l_ref[...] = jnp.zeros(l_ref.shape, F32)
        acc_ref[...] = jnp.zeros(acc_ref.shape, F32)

    lane_k = lax.broadcasted_iota(jnp.int32, (tk, LANES), 1)
    rel = kstart - qstart + lax.broadcasted_iota(jnp.int32, (tk, LANES), 0)
    hi = (rel // BIAS_SPLIT) * BIAS_SPLIT
    lo = rel - hi
    pos_lane = lane_k % half
    kb_ref[...] = jnp.where(pos_lane == 0, hi, jnp.where(pos_lane == 1, lo, 0)).astype(F32).astype(BF16)

    needs_mask = (flags & FLAG_MASK) != 0

    @pl.when(needs_mask)
    def _():
        kpos = kstart + lax.broadcasted_iota(jnp.int32, (tk, tq), 0)
        qpos = qstart + lax.broadcasted_iota(jnp.int32, (tk, tq), 1)
        allowed = jnp.logical_and(kpos // CHUNK <= qpos // CHUNK, kpos < kv_len)
        after = jnp.where(kpos > qpos, 2.0 * (qpos - kpos).astype(F32), 0.0)
        corr_ref[0] = after
        corr_ref[1] = jnp.where(allowed, 0.0, NEG_BIG)

    lane_q = lax.broadcasted_iota(jnp.int32, (tq, LANES), 1)

    def head_group(g, masked):
        chains = []
        for hh in range(HEADS_PER_GROUP):
            h = g * HEADS_PER_GROUP + hh
            c0 = pl.multiple_of(h * LANES, LANES)
            q = q_ref[:, pl.ds(c0, LANES)] * jnp.asarray(scale, BF16)
            k = k_ref[:, pl.ds(c0, LANES)]
            v = v_ref[:, pl.ds(c0, LANES)]
            kb = kb_ref[...]
            slope = slope_ref[pl.ds(h, 1), :]
            slope_b = jnp.broadcast_to(slope.astype(BF16), (tq, LANES))
            zero_q = jnp.zeros((tq, LANES), BF16)
            for idx in range(2):
                own = (lane_q < half) if idx == 0 else (lane_q >= half)
                q_aug = jnp.where(own, q, jnp.where((lane_q % half) < 2, slope_b, zero_q))
                own_k = (lane_k < half) if idx == 0 else (lane_k >= half)
                k_aug = jnp.where(own_k, k, kb)
                chains.append((h, idx, q_aug, k_aug, v, slope, m_ref[idx, h], l_ref[idx, h]))
        results = []
        for h, idx, q_aug, k_aug, v, slope, m_prev, l_prev in chains:
            s = lax.dot_general(k_aug, q_aug, (((1,), (1,)), ((), ())), preferred_element_type=F32)
            if masked:
                s = s + slope[:, 0:1] * corr_ref[0] + corr_ref[1]
            m_new = jnp.maximum(m_prev, jnp.max(s, axis=0, keepdims=True))
            alpha = jnp.exp(m_prev - m_new)
            p = jnp.exp(s - m_new)
            l_new = alpha * l_prev + jnp.sum(p, axis=0, keepdims=True)
            pv = lax.dot_general(v, p.astype(BF16), (((0,), (0,)), ((), ())), preferred_element_type=F32)
            results.append((h, idx, m_new, l_new, alpha, pv))
        for h, idx, m_new, l_new, alpha, pv in results:
            m_ref[idx, h] = m_new
            l_ref[idx, h] = l_new
            acc_ref[idx, h] = alpha * acc_ref[idx, h] + pv

    n_groups = ATT_N_HEADS // HEADS_PER_GROUP

    @pl.when(needs_mask)
    def _():
        lax.fori_loop(0, n_groups, lambda g, c: (head_group(g, True), c)[1], 0)

    @pl.when(jnp.logical_not(needs_mask))
    def _():
        lax.fori_loop(0, n_groups, lambda g, c: (head_group(g, False), c)[1], 0)

    @pl.when((flags & FLAG_LAST) != 0)
    def _():
        lp = lam_ref[...]
        lam = (jnp.exp(jnp.sum(lp[0:1] * lp[1:2], axis=-1, keepdims=True))
               - jnp.exp(jnp.sum(lp[2:3] * lp[3:4], axis=-1, keepdims=True)) + lambda_init)

        def fin(h, c):
            c0 = pl.multiple_of(h * LANES, LANES)
            o = acc_ref[0, h] / l_ref[0, h] - lam * (acc_ref[1, h] / l_ref[1, h])
            on = o * lax.rsqrt(jnp.mean(o * o, axis=0, keepdims=True) + EPS)
            on = (on * sub_ref[...]) * (1.0 - lambda_init)
            o_ref[:, pl.ds(c0, LANES)] = on.T.astype(o_ref.dtype)
            return c

        lax.fori_loop(0, ATT_N_HEADS, fin, 0)


def _attn_pairs(nq, nk, tq, tk, q_off, kv_len):
    qis, kis, fls = [], [], []
    for qi in range(nq):
        first_q = q_off + qi * tq
        last_q = first_q + tq - 1
        kend = min((last_q // CHUNK + 1) * CHUNK, kv_len)
        nkv = -(-kend // tk)
        for ki in range(nkv):
            block_end = (ki + 1) * tk
            plain = block_end <= (first_q // CHUNK) * CHUNK and block_end <= kv_len
            fl = (FLAG_FIRST if ki == 0 else 0) | (FLAG_LAST if ki == nkv - 1 else 0) | (0 if plain else FLAG_MASK)
            qis.append(qi)
            kis.append(ki)
            fls.append(fl)
    return qis, kis, fls


def diff_attention_t(q_src, q_blk0, q_colblk, k_src, k_colblk, v_src, v_colblk, lam_rows, subln, *, b, lq, lk,
                     tq, tk, q_off, kv_len, lambda_init, out_rows):
    assert lq % tq == 0 and lk % tk == 0 and tq % LANES == 0
    assert q_off + lq <= 256 * BIAS_SPLIT + tq
    nq, nk = lq // tq, lk // tk
    width = ATT_N_HEADS * LANES
    qis, kis, fls = _attn_pairs(nq, nk, tq, tk, q_off, kv_len)
    kern = functools.partial(_attn_t_kernel, tq=tq, tk=tk, q_off=q_off, kv_len=kv_len, lambda_init=lambda_init)
    slopes = jnp.exp2(-ALIBI_MAX_BIAS * jnp.arange(1, ATT_N_HEADS + 1, dtype=F32) / ATT_N_HEADS)
    slopes = jnp.broadcast_to(slopes[:, None], (ATT_N_HEADS, LANES))
    const = lambda shape: pl.BlockSpec(shape, lambda bi, p, qt, kt, ft: (0,) * len(shape))
    return pl.pallas_call(
        kern,
        grid_spec=pltpu.PrefetchScalarGridSpec(
            num_scalar_prefetch=3,
            grid=(b, len(qis)),
            in_specs=[
                pl.BlockSpec((tq, width), lambda bi, p, qt, kt, ft: (q_blk0 + bi * nq + qt[p], q_colblk)),
                pl.BlockSpec((tk, width), lambda bi, p, qt, kt, ft: (bi * nk + kt[p], k_colblk)),
                pl.BlockSpec((tk, width), lambda bi, p, qt, kt, ft: (bi * nk + kt[p], v_colblk)),
                const((ATT_N_HEADS, LANES)),
                const((SUBLANES, ATT_HEAD_DIM)),
                const((ATT_V_HEAD, 1)),
            ],
            out_specs=pl.BlockSpec((tq, width), lambda bi, p, qt, kt, ft: (bi * nq + qt[p], 0)),
            scratch_shapes=[
                pltpu.VMEM((2, ATT_N_HEADS, 1, tq), F32),
                pltpu.VMEM((2, ATT_N_HEADS, 1, tq), F32),
                pltpu.VMEM((2, ATT_N_HEADS, ATT_V_HEAD, tq), F32),
                pltpu.VMEM((tk, LANES), BF16),
                pltpu.VMEM((2, tk, tq), F32),
            ],
        ),
        out_shape=jax.ShapeDtypeStruct((out_rows, width), BF16),
        compiler_params=_cparams(("arbitrary", "arbitrary")),
        name="diff_attention_t",
    )(jnp.asarray(qis, jnp.int32), jnp.asarray(kis, jnp.int32), jnp.asarray(fls, jnp.int32),
      q_src, k_src, v_src, slopes, lam_rows, subln.reshape(-1, 1))


def _mixer_out_kernel(y_ref, o_ref, gs_ref, ga_ref, x_ref, wos_ref, woa_ref, wout_ref, nf_ref, rw_ref, rb_ref,
                      x2_ref, h_ref, gate_ref, sel_ref, cnt_ref, run_ref, *, tm):
    i = pl.program_id(0)

    @pl.when(i == 0)
    def _():
        run_ref[...] = jnp.zeros(run_ref.shape, F32)

    o_ssd = _dot(y_ref[...], wos_ref[...])
    o_att = _dot(o_ref[...], woa_ref[...])
    merged = (jax.nn.sigmoid(gs_ref[...].astype(F32)) * o_ssd
              + jax.nn.sigmoid(ga_ref[...].astype(F32)) * o_att)
    x2 = x_ref[...] + _dot(merged.astype(BF16), wout_ref[...])
    x2_ref[...] = x2
    hn = x2 * lax.rsqrt(jnp.mean(x2 * x2, axis=-1, keepdims=True) + EPS) * nf_ref[...]
    h_ref[...] = hn

    a1, a2, a3 = _split3(hn)
    w1, w2, w3 = _split3(rw_ref[...])
    logits = (_dot(a1, w1) + _dot(a1, w2) + _dot(a2, w1) + _dot(a2, w2) + _dot(a1, w3) + _dot(a3, w1)
              + rb_ref[...])
    lane = lax.broadcasted_iota(jnp.int32, (tm, LANES), 1)
    work = jnp.where(lane < N_EXPERTS, logits, -jnp.inf)
    tops, idxs = [], []
    for _ in range(TOP_K):
        mx = jnp.max(work, axis=-1, keepdims=True)
        ix = jnp.min(jnp.where(work == mx, lane, LANES), axis=-1, keepdims=True)
        tops.append(mx)
        idxs.append(ix)
        work = jnp.where(lane == ix, -jnp.inf, work)
    es = [jnp.exp(tv - tops[0]) for tv in tops]
    den = es[0] + es[1] + es[2] + es[3]
    gates = jnp.zeros((tm, LANES), F32)
    for k in range(TOP_K):
        gates = jnp.where(lane == k, es[k] / den, gates)
    gate_ref[...] = gates

    chosen = jnp.zeros((tm, LANES), jnp.bool_)
    for k in range(TOP_K):
        chosen = jnp.logical_or(chosen, lane == idxs[k])
    multihot = jnp.where(chosen, 1.0, 0.0).astype(BF16)
    ri = lax.broadcasted_iota(jnp.int32, (tm, tm), 0)
    ci = lax.broadcasted_iota(jnp.int32, (tm, tm), 1)
    strict = jnp.where(ci < ri, 1.0, 0.0).astype(BF16)
    prefix = _dot(strict, multihot) + run_ref[...]
    sel = jnp.zeros((tm, LANES), jnp.int32)
    for k in range(TOP_K):
        rank = jnp.sum(jnp.where(lane == idxs[k], prefix, 0.0), axis=-1, keepdims=True)
        sel = jnp.where(lane == k, idxs[k], sel)
        sel = jnp.where(lane == TOP_K + k, rank.astype(jnp.int32), sel)
    sel_ref[...] = sel
    run_ref[...] = run_ref[...] + jnp.sum(multihot.astype(F32), axis=0, keepdims=True)
    cnt_ref[...] = run_ref[...].astype(jnp.int32)


def mixer_out(y_ssd, o_att, gates_src, gs_col, ga_col, x, w_o_ssd, w_o_att, w_out, norm_ffn, router_w, router_b, tm):
    t, d = x.shape
    assert t % tm == 0
    kern = functools.partial(_mixer_out_kernel, tm=tm)
    rw = jnp.zeros((d, LANES), F32).at[:, :N_EXPERTS].set(router_w)
    rb = jnp.zeros((1, LANES), F32).at[0, :N_EXPERTS].set(router_b)
    const = lambda shape: pl.BlockSpec(shape, lambda i: (0, 0))
    return pl.pallas_call(
        kern,
        grid=(t // tm,),
        in_specs=[
            pl.BlockSpec((tm, SSD_D_INNER), lambda i: (i, 0)),
            pl.BlockSpec((tm, d), lambda i: (i, 0)),
            pl.BlockSpec((tm, d), lambda i: (i, gs_col)),
            pl.BlockSpec((tm, d), lambda i: (i, ga_col)),
            pl.BlockSpec((tm, d), lambda i: (i, 0)),
            const((SSD_D_INNER, d)),
            const((d, d)),
            const((d, d)),
            const((1, d)),
            const((d, LANES)),
            const((1, LANES)),
        ],
        out_specs=[
            pl.BlockSpec((tm, d), lambda i: (i, 0)),
            pl.BlockSpec((tm, d), lambda i: (i, 0)),
            pl.BlockSpec((tm, LANES), lambda i: (i, 0)),
            pl.BlockSpec((tm, LANES), lambda i: (i, 0)),
            pl.BlockSpec((1, LANES), lambda i: (0, 0)),
        ],
        out_shape=[
            jax.ShapeDtypeStruct((t, d), F32),
            jax.ShapeDtypeStruct((t, d), F32),
            jax.ShapeDtypeStruct((t, LANES), F32),
            jax.ShapeDtypeStruct((t, LANES), jnp.int32),
            jax.ShapeDtypeStruct((1, LANES), jnp.int32),
        ],
        scratch_shapes=[pltpu.VMEM((1, LANES), F32)],
        compiler_params=_cparams(("arbitrary",)),
        name="mixer_out",
    )(y_ssd, o_att, gates_src, gates_src, x, w_o_ssd, w_o_att, w_out, norm_ffn.reshape(1, d), rw, rb)


def _moe_kernel(be_ref, nb_ref, x_ref, wgu_ref, bgu_ref, wd_ref, bd_ref, o_ref):
    b = pl.program_id(0)

    @pl.when(b < nb_ref[0])
    def _():
        x = x_ref[...].astype(BF16)
        gu = _dot(x, wgu_ref[0]) + bgu_ref[0]
        d_ff = gu.shape[1] // 2
        gate = jnp.minimum(gu[:, :d_ff], SWIGLU_LIMIT)
        up = jnp.clip(gu[:, d_ff:], -SWIGLU_LIMIT, SWIGLU_LIMIT)
        act = (up + 1.0) * gate * jax.nn.sigmoid(SWIGLU_ALPHA * gate)
        o_ref[...] = _dot(act.astype(BF16), wd_ref[0]) + bd_ref[0]


def moe_experts(xs, block_e, n_used, w_gu, b_gu, w_down, b_down, rb):
    rows, d = xs.shape
    nb = rows // rb
    e, _, gu_w = w_gu.shape
    return pl.pallas_call(
        _moe_kernel,
        grid_spec=pltpu.PrefetchScalarGridSpec(
            num_scalar_prefetch=2,
            grid=(nb,),
            in_specs=[
                pl.BlockSpec((rb, d), lambda b, be, nu: (b, 0)),
                pl.BlockSpec((1, d, gu_w), lambda b, be, nu: (be[b], 0, 0)),
                pl.BlockSpec((1, 1, gu_w), lambda b, be, nu: (be[b], 0, 0)),
                pl.BlockSpec((1, gu_w // 2, d), lambda b, be, nu: (be[b], 0, 0)),
                pl.BlockSpec((1, 1, d), lambda b, be, nu: (be[b], 0, 0)),
            ],
            out_specs=pl.BlockSpec((rb, d), lambda b, be, nu: (b, 0)),
        ),
        out_shape=jax.ShapeDtypeStruct((rows, d), F32),
        compiler_params=_cparams(("arbitrary",)),
        name="moe_experts",
    )(block_e, n_used, xs, w_gu, b_gu.reshape(e, 1, gu_w), w_down, b_down.reshape(e, 1, d))


def _combine_kernel(x2_ref, ys_ref, gate_ref, g_ref, o_ref):
    acc = x2_ref[...]
    gates = gate_ref[...]
    moe = jnp.zeros_like(acc)
    d = acc.shape[1]
    for k in range(TOP_K):
        moe = moe + ys_ref[:, k * d:(k + 1) * d] * gates[:, k:k + 1]
    tok = acc + moe
    y = tok * lax.rsqrt(jnp.mean(tok * tok, axis=-1, keepdims=True) + EPS)
    o_ref[...] = y * g_ref[...]


def combine(x2, ys4, gates, norm_final, tm):
    t, d = x2.shape
    return pl.pallas_call(
        _combine_kernel,
        grid=(t // tm,),
        in_specs=[
            pl.BlockSpec((tm, d), lambda i: (i, 0)),
            pl.BlockSpec((tm, TOP_K * d), lambda i: (i, 0)),
            pl.BlockSpec((tm, LANES), lambda i: (i, 0)),
            pl.BlockSpec((1, d), lambda i: (0, 0)),
        ],
        out_specs=pl.BlockSpec((tm, d), lambda i: (i, 0)),
        out_shape=jax.ShapeDtypeStruct((t, d), F32),
        compiler_params=_cparams(("arbitrary",)),
        name="combine",
    )(x2, ys4, gates, norm_final.reshape(1, d))


MOE_ROWS = 256


def kernel(x_prompt, x_sample, cache_k, cache_v, state_ssm, state_conv, norm_mix, w_in, conv_w, conv_b, dt_bias, a_log, d_skip, ssd_norm, w_o_ssd, lambda_q1, lambda_k1, lambda_q2, lambda_k2, subln, w_o_att, w_out, norm_ffn, router_w, router_b, w_gu, b_gu, w_down, b_down, norm_final):
    bp, s, d = x_prompt.shape
    bs, ts, _ = x_sample.shape
    past = cache_k.shape[2]
    depth = w_in.shape[0]
    assert depth == 1
    layer = 0
    lambda_init = 0.8 - 0.6 * math.exp(-0.3 * layer)
    tp, tsn = bp * s, bs * ts
    t_all = tp + tsn

    x_all = jnp.concatenate([x_prompt.reshape(tp, d), x_sample.reshape(tsn, d)], axis=0)

    sizes = (SSD_D_INNER, SSD_CONV_CH, SSD_N_HEADS, d, d, d, d, d)
    offs = [0]
    for sz in sizes:
        offs.append(offs[-1] + sz)
    w = w_in[layer]
    seg = lambda i: w[:, offs[i]:offs[i + 1]]
    w_kv = jnp.concatenate([seg(4), seg(5)], axis=1).astype(BF16)
    w_zq = jnp.concatenate([seg(0), seg(3), seg(6), seg(7)], axis=1).astype(BF16)
    w_xd = jnp.concatenate([seg(1), seg(2), jnp.zeros((d, LANES - SSD_N_HEADS), F32)], axis=1).astype(BF16)
    tm = 1280
    kv = norm_matmul(x_all, norm_mix[layer], w_kv, F32, tm, 512)
    zq = norm_matmul(x_all, norm_mix[layer], w_zq, BF16, tm, 512)
    xd = norm_matmul(x_all, norm_mix[layer], w_xd, F32, tm, 640)
    kv16 = kv.astype(BF16)

    prev_p = jnp.zeros((bp, SUBLANES, SSD_CONV_CH), F32)
    h0_p = jnp.zeros((bp, SSD_D_STATE, SSD_D_INNER), F32)
    ssd_w = (conv_w[layer], conv_b[layer], dt_bias[layer], a_log[layer], d_skip[layer], ssd_norm[layer])
    y_p, hT_p = ssd_mixer(xd, zq, 0, prev_p, h0_p, *ssd_w, b=bp, l=s, tl=256, valid_len=None, out_rows=t_all)

    pad_rows = CHUNK - ts
    xd_s = jnp.pad(xd[tp:].reshape(bs, ts, -1), ((0, 0), (0, pad_rows), (0, 0))).reshape(bs * CHUNK, -1)
    z_s = jnp.pad(zq[tp:, :SSD_D_INNER].reshape(bs, ts, -1), ((0, 0), (0, pad_rows), (0, 0))).reshape(bs * CHUNK, -1)
    prev_s = jnp.pad(state_conv[layer], ((0, 0), (SUBLANES - (SSD_CONV_WIDTH - 1), 0), (0, 0)))
    h0_s = jnp.swapaxes(state_ssm[layer].reshape(bs, SSD_D_INNER, SSD_D_STATE), 1, 2)
    y_s, hT_s = ssd_mixer(xd_s, z_s, 0, prev_s, h0_s, *ssd_w, b=bs, l=CHUNK, tl=CHUNK, valid_len=ts,
                          out_rows=bs * CHUNK)
    y_ssd = lax.dynamic_update_slice(y_p, y_s.reshape(bs, CHUNK, -1)[:, :ts].reshape(tsn, -1), (tp, 0))

    lam_rows = jnp.concatenate([lambda_q1[layer][None], lambda_k1[layer][None], lambda_q2[layer][None],
                                lambda_k2[layer][None], jnp.zeros((4, ATT_HEAD_DIM), F32)], axis=0)
    heads = ATT_N_HEADS
    tq_p = 512
    o_p = diff_attention_t(zq, 0, SSD_D_INNER // d, kv16, 0, kv16, 1, lam_rows, subln[layer], b=bp, lq=s, lk=s,
                           tq=tq_p, tk=512, q_off=0, kv_len=s, lambda_init=lambda_init,
                           out_rows=-(-t_all // tq_p) * tq_p)
    kv_s = kv16[tp:].reshape(bs, ts, -1)
    kv_len = past + ts
    kv_pad = (-kv_len) % LANES
    lk_s = kv_len + kv_pad
    k_all = jnp.concatenate([cache_k[layer].reshape(bs, past, -1).astype(BF16), kv_s[:, :, :d],
                             jnp.zeros((bs, kv_pad, d), BF16)], axis=1).reshape(bs * lk_s, d)
    v_all = jnp.concatenate([cache_v[layer].reshape(bs, past, -1).astype(BF16), kv_s[:, :, d:],
                             jnp.zeros((bs, kv_pad, d), BF16)], axis=1).reshape(bs * lk_s, d)
    q_s = jnp.pad(zq[tp:, SSD_D_INNER:SSD_D_INNER + d].reshape(bs, ts, d), ((0, 0), (0, LANES - ts), (0, 0)))
    o_s = diff_attention_t(q_s.reshape(bs * LANES, d), 0, 0, k_all, 0, v_all, 0, lam_rows, subln[layer], b=bs,
                           lq=LANES, lk=lk_s, tq=LANES, tk=lk_s, q_off=past, kv_len=kv_len,
                           lambda_init=lambda_init, out_rows=bs * LANES)
    o_att = lax.dynamic_update_slice(o_p, o_s.reshape(bs, LANES, d)[:, :ts].reshape(tsn, d), (tp, 0))

    x2, hn, gates, sel, counts = mixer_out(
        y_ssd, o_att, zq, 3, 4, x_all, w_o_ssd[layer].astype(BF16), w_o_att[layer].astype(BF16),
        w_out[layer].astype(BF16), norm_ffn[layer], router_w[layer], router_b[layer], tm=640)

    top_e = sel[:, :TOP_K]
    rank = sel[:, TOP_K:2 * TOP_K]
    cnt = counts[0, :N_EXPERTS]
    padded = (cnt + MOE_ROWS - 1) // MOE_ROWS * MOE_ROWS
    ends = jnp.cumsum(padded)
    starts = ends - padded
    dest = starts[top_e] + rank
    nb = (t_all * TOP_K + N_EXPERTS * (MOE_ROWS - 1) + MOE_ROWS - 1) // MOE_ROWS
    block_start = jnp.arange(nb, dtype=jnp.int32) * MOE_ROWS
    block_e = jnp.minimum(jnp.sum((ends[None, :] <= block_start[:, None]).astype(jnp.int32), axis=1),
                          N_EXPERTS - 1)
    n_used = (ends[-1] // MOE_ROWS).astype(jnp.int32).reshape(1)
    rows_tok = jnp.full((nb * MOE_ROWS,), t_all, jnp.int32).at[dest.reshape(-1)].set(
        jnp.arange(t_all * TOP_K, dtype=jnp.int32) // TOP_K)
    h_pad = jnp.concatenate([hn, jnp.zeros((1, d), F32)], axis=0)
    xs_sorted = h_pad[rows_tok]
    ys = moe_experts(xs_sorted, block_e, n_used, w_gu[layer].astype(BF16), b_gu[layer],
                     w_down[layer].astype(BF16), b_down[layer], MOE_ROWS)
    ys4 = ys[dest.reshape(-1)].reshape(t_all, TOP_K * d)
    y_all = combine(x2, ys4, gates, norm_final, tm=320)

    y_prompt = y_all[:tp].reshape(bp, s, d)
    y_sample = y_all[tp:].reshape(bs, ts, d)
    k_f, v_f = kv[:, :d], kv[:, d:]
    new_k_p = k_f[:tp].reshape(1, bp, s, heads, 2 * ATT_HEAD_DIM)
    new_v_p = v_f[:tp].reshape(1, bp, s, heads, ATT_V_HEAD)
    new_k_s = k_f[tp:].reshape(1, bs, ts, heads, 2 * ATT_HEAD_DIM)
    new_v_s = v_f[tp:].reshape(1, bs, ts, heads, ATT_V_HEAD)
    ssm_p = jnp.swapaxes(hT_p, 1, 2).reshape(1, bp, SSD_N_HEADS, SSD_HEAD_DIM, SSD_D_STATE)
    ssm_s = jnp.swapaxes(hT_s, 1, 2).reshape(1, bs, SSD_N_HEADS, SSD_HEAD_DIM, SSD_D_STATE)
    keep = SSD_CONV_WIDTH - 1
    conv_p = xd[:tp].reshape(bp, s, -1)[:, s - keep:, :SSD_CONV_CH][None]
    raw_s = jnp.concatenate([state_conv[layer], xd[tp:].reshape(bs, ts, -1)[:, :, :SSD_CONV_CH]], axis=1)
    conv_s = raw_s[:, -keep:][None]
    return (y_prompt, y_sample, new_k_p, new_v_p, ssm_p, conv_p, new_k_s, new_v_s, ssm_s, conv_s)
```

```python
import functools
import math

import jax
import jax.numpy as jnp
from jax import lax
from jax.experimental import pallas as pl
from jax.experimental.pallas import tpu as pltpu

EPS = 1e-5
CHUNK = 64
D_MODEL = 1024
SSD_D_INNER = 2048
SSD_HEAD_DIM = 64
SSD_N_HEADS = 32
SSD_GROUPS = 4
SSD_D_STATE = 128
SSD_CONV_WIDTH = 4
SSD_CONV_CH = 3072
ATT_HEAD_DIM = 64
ATT_N_HEADS = 8
ATT_V_HEAD = 128
ALIBI_MAX_BIAS = 8.0
N_EXPERTS = 32
TOP_K = 4
SWIGLU_LIMIT = 7.0
SWIGLU_ALPHA = 1.702

LANES = 128
SUBLANES = 8
VMEM_LIMIT = 56 * 1024 * 1024
NEG_BIG = -1e30

BF16 = jnp.bfloat16
F32 = jnp.float32


def _cparams(sem, flags=None):
    return pltpu.CompilerParams(dimension_semantics=sem, vmem_limit_bytes=VMEM_LIMIT, flags=flags)


def _split3(x):
    h1 = x.astype(BF16)
    r1 = x - h1.astype(F32)
    h2 = r1.astype(BF16)
    h3 = (r1 - h2.astype(F32)).astype(BF16)
    return h1, h2, h3


def _dot(a, b):
    return jnp.dot(a, b, preferred_element_type=F32)


def _dot_exact_rhs(x, m):
    h1, h2, h3 = _split3(x)
    return _dot(h1, m) + _dot(h2, m) + _dot(h3, m)


def _dot_exact_lhs(m, x):
    h1, h2, h3 = _split3(x)
    return _dot(m, h1) + _dot(m, h2) + _dot(m, h3)


PROJ_TN = 512
PROJ_K = (0, D_MODEL // PROJ_TN)
PROJ_V = (PROJ_K[1], PROJ_K[1] + D_MODEL // PROJ_TN)
PROJ_ZQ_W = SSD_D_INNER + 3 * D_MODEL
PROJ_ZQ = (PROJ_V[1], PROJ_V[1] + PROJ_ZQ_W // PROJ_TN)
PROJ_XD_W = -(-(SSD_CONV_CH + SSD_N_HEADS) // PROJ_TN) * PROJ_TN
PROJ_XD = (PROJ_ZQ[1], PROJ_ZQ[1] + PROJ_XD_W // PROJ_TN)


def _in_proj_kernel(x_ref, g_ref, w_ref, k_ref, v_ref, kv16_ref, zq_ref, xd_ref, xn_ref):
    j = pl.program_id(1)

    @pl.when(j == 0)
    def _():
        x = x_ref[...]
        y = x * lax.rsqrt(jnp.mean(x * x, axis=-1, keepdims=True) + EPS)
        xn_ref[...] = (y * g_ref[...]).astype(BF16)

    acc = _dot(xn_ref[...], w_ref[...])

    @pl.when(j < PROJ_K[1])
    def _():
        k_ref[...] = acc
        kv16_ref[...] = acc.astype(BF16)

    @pl.when(jnp.logical_and(j >= PROJ_V[0], j < PROJ_V[1]))
    def _():
        v_ref[...] = acc
        kv16_ref[...] = acc.astype(BF16)

    @pl.when(jnp.logical_and(j >= PROJ_ZQ[0], j < PROJ_ZQ[1]))
    def _():
        zq_ref[...] = acc.astype(BF16)

    @pl.when(j >= PROJ_XD[0])
    def _():
        xd_ref[...] = acc


def in_proj(x, gain, w, tm):
    t, d = x.shape
    assert t % tm == 0 and w.shape[1] == PROJ_XD[1] * PROJ_TN
    tn = PROJ_TN

    def out_map(rng):
        return lambda i, j: (i, jnp.clip(j - rng[0], 0, rng[1] - rng[0] - 1))

    return pl.pallas_call(
        _in_proj_kernel,
        grid=(t // tm, PROJ_XD[1]),
        in_specs=[
            pl.BlockSpec((tm, d), lambda i, j: (i, 0)),
            pl.BlockSpec((1, d), lambda i, j: (0, 0)),
            pl.BlockSpec((d, tn), lambda i, j: (0, j)),
        ],
        out_specs=[
            pl.BlockSpec((tm, tn), out_map(PROJ_K)),
            pl.BlockSpec((tm, tn), out_map(PROJ_V)),
            pl.BlockSpec((tm, tn), out_map((PROJ_K[0], PROJ_V[1]))),
            pl.BlockSpec((tm, tn), out_map(PROJ_ZQ)),
            pl.BlockSpec((tm, tn), out_map(PROJ_XD)),
        ],
        out_shape=[
            jax.ShapeDtypeStruct((t, d), F32),
            jax.ShapeDtypeStruct((t, d), F32),
            jax.ShapeDtypeStruct((t, 2 * d), BF16),
            jax.ShapeDtypeStruct((t, PROJ_ZQ_W), BF16),
            jax.ShapeDtypeStruct((t, PROJ_XD_W), F32),
        ],
        scratch_shapes=[pltpu.VMEM((tm, d), BF16)],
        compiler_params=_cparams(("arbitrary", "arbitrary")),
        name="in_proj",
    )(x, gain.reshape(1, d), w)


def _ssd_kernel(xbc_ref, z_ref, prev_ref, h0_ref, cw_ref, cb_ref, dtb_ref, alog_ref, dsk_ref, gn_ref,
                y_ref, hout_ref, xpad_ref, xc_ref, tail_ref, st_ref, *, tl, valid_len):
    t = pl.program_id(1)
    nt = pl.num_programs(1)
    nch = tl // CHUNK
    dinner = SSD_D_INNER
    gw = dinner // SSD_GROUPS
    ns = SSD_D_STATE

    @pl.when(t == 0)
    def _():
        tail_ref[...] = prev_ref[0]
        st_ref[...] = h0_ref[0]

    xpad_ref[0:SUBLANES, :] = tail_ref[...]
    xpad_ref[SUBLANES:SUBLANES + tl, :] = xbc_ref[:, 0:SSD_CONV_CH]
    tail_ref[...] = xbc_ref[tl - SUBLANES:tl, 0:SSD_CONV_CH]
    acc = cb_ref[...]
    for k in range(SSD_CONV_WIDTH):
        off = SUBLANES - (SSD_CONV_WIDTH - 1) + k
        acc = acc + xpad_ref[off:off + tl, :] * cw_ref[k:k + 1, :]
    xc_ref[...] = acc * jax.nn.sigmoid(acc)

    head_of_lane = lax.broadcasted_iota(jnp.int32, (SSD_N_HEADS, dinner), 1) // SSD_HEAD_DIM
    expand = (head_of_lane == lax.broadcasted_iota(jnp.int32, (SSD_N_HEADS, dinner), 0)).astype(BF16)
    ti = lax.broadcasted_iota(jnp.int32, (CHUNK, CHUNK), 0)
    si = lax.broadcasted_iota(jnp.int32, (CHUNK, CHUNK), 1)
    tril = (si <= ti).astype(BF16)
    row_c = lax.broadcasted_iota(jnp.int32, (CHUNK, dinner), 0)
    pos_in_head = lax.broadcasted_iota(jnp.int32, (CHUNK, dinner), 1) % CHUNK
    upper = row_c <= pos_in_head
    row_p = lax.broadcasted_iota(jnp.int32, (CHUNK, LANES), 0)
    lane_p = lax.broadcasted_iota(jnp.int32, (CHUNK, LANES), 1)
    causal_pair = (lane_p % CHUNK) <= row_p
    left_half = lane_p < SSD_HEAD_DIM
    a_neg = -jnp.exp(alog_ref[...])
    dsk_e = _dot_exact_rhs(dsk_ref[...], expand)

    def chunk_body(c, carry):
        r0 = pl.multiple_of(c * CHUNK, CHUNK)
        xs = xc_ref[pl.ds(r0, CHUNK), 0:dinner]
        dt_raw = xbc_ref[pl.ds(r0, CHUNK), SSD_CONV_CH:SSD_CONV_CH + SSD_N_HEADS]
        dtv = dt_raw + dtb_ref[...]
        dt = jnp.maximum(dtv, 0.0) + jnp.log1p(jnp.exp(-jnp.abs(dtv)))
        if valid_len is not None:
            rows = t * tl + r0 + lax.broadcasted_iota(jnp.int32, (CHUNK, SSD_N_HEADS), 0)
            dt = jnp.where(rows < valid_len, dt, 0.0)
        a = dt * a_neg
        a_e = _dot_exact_rhs(a, expand)
        dt_e = _dot_exact_rhs(dt, expand)
        acs_e = _dot_exact_lhs(tril, a_e)
        rowterm = jnp.sum(jnp.where(upper, a_e, 0.0), axis=0, keepdims=True)
        acs_last = acs_e[CHUNK - 1:CHUNK, :]
        xdt = xs * dt_e
        x_dec = (xdt * jnp.exp(acs_last - acs_e)).astype(BF16)
        e_acs = jnp.exp(acs_e)
        e_last = jnp.exp(acs_last)

        y_parts = []
        for g in range(SSD_GROUPS):
            lo = g * gw
            bm = xc_ref[pl.ds(r0, CHUNK), dinner + g * ns:dinner + (g + 1) * ns].astype(BF16)
            cm = xc_ref[pl.ds(r0, CHUNK), dinner + SSD_GROUPS * ns + g * ns:
                        dinner + SSD_GROUPS * ns + (g + 1) * ns].astype(BF16)
            b2 = jnp.concatenate([bm, bm], axis=0)
            cb2 = lax.dot_general(cm, b2, (((1,), (1,)), ((), ())), preferred_element_type=F32)
            st_g = st_ref[:, lo:lo + gw]
            y_off = _dot(cm, st_g.astype(BF16)) * e_acs[:, lo:lo + gw]
            pieces = []
            for j in range(gw // LANES):
                l0 = lo + j * LANES
                diff = acs_e[:, l0:l0 + LANES] - rowterm[:, l0:l0 + LANES]
                dec = jnp.where(causal_pair, jnp.exp(diff), 0.0)
                scores = (cb2 * dec).astype(BF16)
                xp = xdt[:, l0:l0 + LANES]
                xblk = jnp.concatenate([jnp.where(left_half, xp, 0.0), jnp.where(left_half, 0.0, xp)],
                                       axis=0).astype(BF16)
                pieces.append(_dot(scores, xblk))
            y_diag = jnp.concatenate(pieces, axis=1)
            upd = lax.dot_general(bm, x_dec[:, lo:lo + gw], (((0,), (0,)), ((), ())),
                                  preferred_element_type=F32)
            st_ref[:, lo:lo + gw] = e_last[:, lo:lo + gw] * st_g + upd
            yg = y_diag + y_off + xs[:, lo:lo + gw] * dsk_e[:, lo:lo + gw]
            zg = z_ref[pl.ds(r0, CHUNK), lo:lo + gw].astype(F32)
            yg = yg * (zg * jax.nn.sigmoid(zg))
            yn = yg * lax.rsqrt(jnp.mean(yg * yg, axis=-1, keepdims=True) + EPS)
            y_parts.append((yn * gn_ref[:, lo:lo + gw]).astype(y_ref.dtype))
        y_ref[pl.ds(r0, CHUNK), :] = jnp.concatenate(y_parts, axis=1)
        return carry

    lax.fori_loop(0, nch, chunk_body, 0)

    @pl.when(t == nt - 1)
    def _():
        hout_ref[0] = st_ref[...]


def ssd_mixer(xbc, z_src, z_col, conv_prev8, h0_t, conv_w, conv_b, dt_bias, a_log, d_skip, ssd_norm,
              b, l, tl, valid_len, out_rows):
    wx = xbc.shape[1]
    assert l % tl == 0 and tl % CHUNK == 0 and CHUNK == SSD_HEAD_DIM
    nt = l // tl
    kern = functools.partial(_ssd_kernel, tl=tl, valid_len=valid_len)
    full = lambda shape: pl.BlockSpec(shape, lambda i, j: (0,) * len(shape))
    return pl.pallas_call(
        kern,
        grid=(b, nt),
        in_specs=[
            pl.BlockSpec((tl, wx), lambda i, j: (i * nt + j, 0)),
            pl.BlockSpec((tl, SSD_D_INNER), lambda i, j: (i * nt + j, z_col)),
            pl.BlockSpec((1, SUBLANES, SSD_CONV_CH), lambda i, j: (i, 0, 0)),
            pl.BlockSpec((1, SSD_D_STATE, SSD_D_INNER), lambda i, j: (i, 0, 0)),
            full((SSD_CONV_WIDTH, SSD_CONV_CH)),
            full((1, SSD_CONV_CH)),
            full((1, SSD_N_HEADS)),
            full((1, SSD_N_HEADS)),
            full((1, SSD_N_HEADS)),
            full((1, SSD_D_INNER)),
        ],
        out_specs=[
            pl.BlockSpec((tl, SSD_D_INNER), lambda i, j: (i * nt + j, 0)),
            pl.BlockSpec((1, SSD_D_STATE, SSD_D_INNER), lambda i, j: (i, 0, 0)),
        ],
        out_shape=[
            jax.ShapeDtypeStruct((out_rows, SSD_D_INNER), BF16),
            jax.ShapeDtypeStruct((b, SSD_D_STATE, SSD_D_INNER), F32),
        ],
        scratch_shapes=[
            pltpu.VMEM((tl + SUBLANES, SSD_CONV_CH), F32),
            pltpu.VMEM((tl, SSD_CONV_CH), F32),
            pltpu.VMEM((SUBLANES, SSD_CONV_CH), F32),
            pltpu.VMEM((SSD_D_STATE, SSD_D_INNER), F32),
        ],
        compiler_params=_cparams(("arbitrary", "arbitrary")),
        name="ssd_mixer",
    )(xbc, z_src, conv_prev8, h0_t, conv_w, conv_b.reshape(1, -1), dt_bias.reshape(1, -1),
      a_log.reshape(1, -1), d_skip.reshape(1, -1), ssd_norm.reshape(1, -1))


BIAS_SPLIT = 32
FLAG_FIRST, FLAG_LAST, FLAG_MASK = 1, 2, 4
HEADS_PER_GROUP = 2


def _attn_t_kernel(qi_ref, ki_ref, fl_ref, q_ref, k_ref, v_ref, slope_ref, lam_ref, sub_ref, o_ref,
                   m_ref, l_ref, acc_ref, kb_ref, corr_ref, *, tq, tk, q_off, kv_len, lambda_init):
    p_idx = pl.program_id(1)
    qi = qi_ref[p_idx]
    ki = ki_ref[p_idx]
    flags = fl_ref[p_idx]
    qstart = q_off + qi * tq
    kstart = ki * tk
    half = ATT_HEAD_DIM
    scale = ATT_HEAD_DIM ** -0.5

    @pl.when((flags & FLAG_FIRST) != 0)
    def _():
        m_ref[...] = jnp.full(m_ref.shape, NEG_BIG, F32)
        l_ref[...] = jnp.zeros(l_ref.shape, F32)
        acc_ref[...] = jnp.zeros(acc_ref.shape, F32)

    lane_k = lax.broadcasted_iota(jnp.int32, (tk, LANES), 1)
    rel = kstart - qstart + lax.broadcasted_iota(jnp.int32, (tk, LANES), 0)
    hi = (rel // BIAS_SPLIT) * BIAS_SPLIT
    lo = rel - hi
    pos_lane = lane_k % half
    kb_ref[...] = jnp.where(pos_lane == 0, hi, jnp.where(pos_lane == 1, lo, 0)).astype(F32).astype(BF16)

    needs_mask = (flags & FLAG_MASK) != 0

    @pl.when(needs_mask)
    def _():
        kpos = kstart + lax.broadcasted_iota(jnp.int32, (tk, tq), 0)
        qpos = qstart + lax.broadcasted_iota(jnp.int32, (tk, tq), 1)
        allowed = jnp.logical_and(kpos // CHUNK <= qpos // CHUNK, kpos < kv_len)
        after = jnp.where(kpos > qpos, 2.0 * (qpos - kpos).astype(F32), 0.0)
        corr_ref[0] = after
        corr_ref[1] = jnp.where(allowed, 0.0, NEG_BIG)

    lane_q = lax.broadcasted_iota(jnp.int32, (tq, LANES), 1)

    def head_group(g, masked):
        chains = []
        for hh in range(HEADS_PER_GROUP):
            h = g * HEADS_PER_GROUP + hh
            c0 = pl.multiple_of(h * LANES, LANES)
            q = q_ref[:, pl.ds(c0, LANES)] * jnp.asarray(scale, BF16)
            k = k_ref[:, pl.ds(c0, LANES)]
            v = v_ref[:, pl.ds(c0, LANES)]
            kb = kb_ref[...]
            slope = slope_ref[pl.ds(h, 1), :]
            slope_b = jnp.broadcast_to(slope.astype(BF16), (tq, LANES))
            zero_q = jnp.zeros((tq, LANES), BF16)
            for idx in range(2):
                own = (lane_q < half) if idx == 0 else (lane_q >= half)
                q_aug = jnp.where(own, q, jnp.where((lane_q % half) < 2, slope_b, zero_q))
                own_k = (lane_k < half) if idx == 0 else (lane_k >= half)
                k_aug = jnp.where(own_k, k, kb)
                chains.append((h, idx, q_aug, k_aug, v, slope, m_ref[idx, h], l_ref[idx, h]))
        results = []
        for h, idx, q_aug, k_aug, v, slope, m_prev, l_prev in chains:
            s = lax.dot_general(k_aug, q_aug, (((1,), (1,)), ((), ())), preferred_element_type=F32)
            if masked:
                s = s + slope[:, 0:1] * corr_ref[0] + corr_ref[1]
            m_new = jnp.maximum(m_prev, jnp.max(s, axis=0, keepdims=True))
            alpha = jnp.exp(m_prev - m_new)
            p = jnp.exp(s - m_new)
            l_new = alpha * l_prev + jnp.sum(p, axis=0, keepdims=True)
            pv = lax.dot_general(v, p.astype(BF16), (((0,), (0,)), ((), ())), preferred_element_type=F32)
            results.append((h, idx, m_new, l_new, alpha, pv))
        for h, idx, m_new, l_new, alpha, pv in results:
            m_ref[idx, h] = m_new
            l_ref[idx, h] = l_new
            acc_ref[idx, h] = alpha * acc_ref[idx, h] + pv

    n_groups = ATT_N_HEADS // HEADS_PER_GROUP

    @pl.when(needs_mask)
    def _():
        lax.fori_loop(0, n_groups, lambda g, c: (head_group(g, True), c)[1], 0)

    @pl.when(jnp.logical_not(needs_mask))
    def _():
        lax.fori_loop(0, n_groups, lambda g, c: (head_group(g, False), c)[1], 0)

    @pl.when((flags & FLAG_LAST) != 0)
    def _():
        lp = lam_ref[...]
        lam = (jnp.exp(jnp.sum(lp[0:1] * lp[1:2], axis=-1, keepdims=True))
               - jnp.exp(jnp.sum(lp[2:3] * lp[3:4], axis=-1, keepdims=True)) + lambda_init)

        def fin(h, c):
            c0 = pl.multiple_of(h * LANES, LANES)
            o = acc_ref[0, h] / l_ref[0, h] - lam * (acc_ref[1, h] / l_ref[1, h])
            on = o * lax.rsqrt(jnp.mean(o * o, axis=0, keepdims=True) + EPS)
            on = (on * sub_ref[...]) * (1.0 - lambda_init)
            o_ref[:, pl.ds(c0, LANES)] = on.T.astype(o_ref.dtype)
            return c

        lax.fori_loop(0, ATT_N_HEADS, fin, 0)


def _attn_pairs(nq, nk, tq, tk, q_off, kv_len):
    qis, kis, fls = [], [], []
    for qi in range(nq):
        first_q = q_off + qi * tq
        last_q = first_q + tq - 1
        kend = min((last_q // CHUNK + 1) * CHUNK, kv_len)
        nkv = -(-kend // tk)
        for ki in range(nkv):
            block_end = (ki + 1) * tk
            plain = block_end <= (first_q // CHUNK) * CHUNK and block_end <= kv_len
            fl = (FLAG_FIRST if ki == 0 else 0) | (FLAG_LAST if ki == nkv - 1 else 0) | (0 if plain else FLAG_MASK)
            qis.append(qi)
            kis.append(ki)
            fls.append(fl)
    return qis, kis, fls


def diff_attention_t(q_src, q_blk0, q_colblk, k_src, k_colblk, v_src, v_colblk, lam_rows, subln, *, b, lq, lk,
                     tq, tk, q_off, kv_len, lambda_init, out_rows):
    assert lq % tq == 0 and lk % tk == 0 and tq % LANES == 0
    assert q_off + lq <= 256 * BIAS_SPLIT + tq
    nq, nk = lq // tq, lk // tk
    width = ATT_N_HEADS * LANES
    qis, kis, fls = _attn_pairs(nq, nk, tq, tk, q_off, kv_len)
    kern = functools.partial(_attn_t_kernel, tq=tq, tk=tk, q_off=q_off, kv_len=kv_len, lambda_init=lambda_init)
    slopes = jnp.exp2(-ALIBI_MAX_BIAS * jnp.arange(1, ATT_N_HEADS + 1, dtype=F32) / ATT_N_HEADS)
    slopes = jnp.broadcast_to(slopes[:, None], (ATT_N_HEADS, LANES))
    const = lambda shape: pl.BlockSpec(shape, lambda bi, p, qt, kt, ft: (0,) * len(shape))
    return pl.pallas_call(
        kern,
        grid_spec=pltpu.PrefetchScalarGridSpec(
            num_scalar_prefetch=3,
            grid=(b, len(qis)),
            in_specs=[
                pl.BlockSpec((tq, width), lambda bi, p, qt, kt, ft: (q_blk0 + bi * nq + qt[p], q_colblk)),
                pl.BlockSpec((tk, width), lambda bi, p, qt, kt, ft: (bi * nk + kt[p], k_colblk)),
                pl.BlockSpec((tk, width), lambda bi, p, qt, kt, ft: (bi * nk + kt[p], v_colblk)),
                const((ATT_N_HEADS, LANES)),
                const((SUBLANES, ATT_HEAD_DIM)),
                const((ATT_V_HEAD, 1)),
            ],
            out_specs=pl.BlockSpec((tq, width), lambda bi, p, qt, kt, ft: (bi * nq + qt[p], 0)),
            scratch_shapes=[
                pltpu.VMEM((2, ATT_N_HEADS, 1, tq), F32),
                pltpu.VMEM((2, ATT_N_HEADS, 1, tq), F32),
                pltpu.VMEM((2, ATT_N_HEADS, ATT_V_HEAD, tq), F32),
                pltpu.VMEM((tk, LANES), BF16),
                pltpu.VMEM((2, tk, tq), F32),
            ],
        ),
        out_shape=jax.ShapeDtypeStruct((out_rows, width), BF16),
        compiler_params=_cparams(("arbitrary", "arbitrary")),
        name="diff_attention_t",
    )(jnp.asarray(qis, jnp.int32), jnp.asarray(kis, jnp.int32), jnp.asarray(fls, jnp.int32),
      q_src, k_src, v_src, slopes, lam_rows, subln.reshape(-1, 1))


ROUTE_W = 2 * TOP_K


def _mixer_out_kernel(y_ref, o_ref, gs_ref, ga_ref, x_ref, wos_ref, woa_ref, wout_ref, nf_ref, rw_ref, rb_ref,
                      run0_ref, x2_ref, h_ref, gate_ref, sel_ref, cnt_ref, run_ref, *, tm):
    i = pl.program_id(0)

    @pl.when(i == 0)
    def _():
        run_ref[...] = run0_ref[...]

    o_ssd = _dot(y_ref[...], wos_ref[...])
    o_att = _dot(o_ref[...], woa_ref[...])
    merged = (jax.nn.sigmoid(gs_ref[...].astype(F32)) * o_ssd
              + jax.nn.sigmoid(ga_ref[...].astype(F32)) * o_att)
    x2 = x_ref[...] + _dot(merged.astype(BF16), wout_ref[...])
    x2_ref[...] = x2
    hn = x2 * lax.rsqrt(jnp.mean(x2 * x2, axis=-1, keepdims=True) + EPS) * nf_ref[...]
    h_ref[...] = hn

    a1, a2, a3 = _split3(hn)
    w1, w2, w3 = _split3(rw_ref[...])
    logits = (_dot(a1, w1) + _dot(a1, w2) + _dot(a2, w1) + _dot(a2, w2) + _dot(a1, w3) + _dot(a3, w1)
              + rb_ref[...])
    lane = lax.broadcasted_iota(jnp.int32, (tm, LANES), 1)
    work = jnp.where(lane < N_EXPERTS, logits, -jnp.inf)
    tops, idxs = [], []
    for _ in range(TOP_K):
        mx = jnp.max(work, axis=-1, keepdims=True)
        ix = jnp.min(jnp.where(work == mx, lane, LANES), axis=-1, keepdims=True)
        tops.append(mx)
        idxs.append(ix)
        work = jnp.where(lane == ix, -jnp.inf, work)
    es = [jnp.exp(tv - tops[0]) for tv in tops]
    den = es[0] + es[1] + es[2] + es[3]
    gates = jnp.zeros((tm, LANES), F32)
    for k in range(TOP_K):
        gates = jnp.where(lane == k, es[k] / den, gates)
    gate_ref[...] = gates[:, :ROUTE_W]

    chosen = jnp.zeros((tm, LANES), jnp.bool_)
    for k in range(TOP_K):
        chosen = jnp.logical_or(chosen, lane == idxs[k])
    multihot = jnp.where(chosen, 1.0, 0.0).astype(BF16)
    ri = lax.broadcasted_iota(jnp.int32, (tm, tm), 0)
    ci = lax.broadcasted_iota(jnp.int32, (tm, tm), 1)
    strict = jnp.where(ci < ri, 1.0, 0.0).astype(BF16)
    prefix = _dot(strict, multihot) + run_ref[...]
    sel = jnp.zeros((tm, LANES), jnp.int32)
    for k in range(TOP_K):
        rank = jnp.sum(jnp.where(lane == idxs[k], prefix, 0.0), axis=-1, keepdims=True)
        sel = jnp.where(lane == k, idxs[k], sel)
        sel = jnp.where(lane == TOP_K + k, rank.astype(jnp.int32), sel)
    sel_ref[...] = sel[:, :ROUTE_W]
    run_ref[...] = run_ref[...] + jnp.sum(multihot.astype(F32), axis=0, keepdims=True)
    cnt_ref[...] = run_ref[...]


def mixer_out(y_ssd, o_att, gates_src, gs_col, ga_col, x, w_o_ssd, w_o_att, w_out, norm_ffn, rw, rb, counts0, tm):
    t, d = x.shape
    assert t % tm == 0
    kern = functools.partial(_mixer_out_kernel, tm=tm)
    const = lambda shape: pl.BlockSpec(shape, lambda i: (0, 0))
    return pl.pallas_call(
        kern,
        grid=(t // tm,),
        in_specs=[
            pl.BlockSpec((tm, SSD_D_INNER), lambda i: (i, 0)),
            pl.BlockSpec((tm, d), lambda i: (i, 0)),
            pl.BlockSpec((tm, d), lambda i: (i, gs_col)),
            pl.BlockSpec((tm, d), lambda i: (i, ga_col)),
            pl.BlockSpec((tm, d), lambda i: (i, 0)),
            const((SSD_D_INNER, d)),
            const((d, d)),
            const((d, d)),
            const((1, d)),
            const((d, LANES)),
            const((1, LANES)),
            const((1, LANES)),
        ],
        out_specs=[
            pl.BlockSpec((tm, d), lambda i: (i, 0)),
            pl.BlockSpec((tm, d), lambda i: (i, 0)),
            pl.BlockSpec((tm, ROUTE_W), lambda i: (i, 0)),
            pl.BlockSpec((tm, ROUTE_W), lambda i: (i, 0)),
            pl.BlockSpec((1, LANES), lambda i: (0, 0)),
        ],
        out_shape=[
            jax.ShapeDtypeStruct((t, d), F32),
            jax.ShapeDtypeStruct((t, d), F32),
            jax.ShapeDtypeStruct((t, ROUTE_W), F32),
            jax.ShapeDtypeStruct((t, ROUTE_W), jnp.int32),
            jax.ShapeDtypeStruct((1, LANES), F32),
        ],
        scratch_shapes=[pltpu.VMEM((1, LANES), F32)],
        compiler_params=_cparams(("arbitrary",)),
        name="mixer_out",
    )(y_ssd, o_att, gates_src, gates_src, x, w_o_ssd, w_o_att, w_out, norm_ffn.reshape(1, d), rw, rb, counts0)


MOE_ROWS = 256
ROUTE_TOK = 256


def _dispatch_kernel(ps_ref, pl_ref, nu_ref, dest_ref, hp_ref, hs_ref, xs_ref, zero_ref, sem,
                     *, prompt_tiles, n_tiles, n_blocks):
    i = pl.program_id(0)
    rows_per_step = ROUTE_TOK * TOP_K

    def zero_copies(action):
        def per_expert(e, c):
            def row(r, c2):
                action(pltpu.make_async_copy(zero_ref.at[pl.ds(0, 1)], xs_ref.at[pl.ds(ps_ref[e] + r, 1)],
                                             sem.at[1]))
                return c2
            return lax.fori_loop(0, pl_ref[e], row, c)

        lax.fori_loop(0, N_EXPERTS, per_expert, 0)

        def tail(b, c):
            row0 = pl.multiple_of(b * MOE_ROWS, MOE_ROWS)
            action(pltpu.make_async_copy(zero_ref, xs_ref.at[pl.ds(row0, MOE_ROWS)], sem.at[1]))
            return c

        lax.fori_loop(nu_ref[0], n_blocks, tail, 0)

    @pl.when(i == 0)
    def _():
        zero_ref[...] = jnp.zeros(zero_ref.shape, F32)
        zero_copies(lambda cp: cp.start())

    def issue(src_ref, row0):
        def body(t, c):
            for k in range(TOP_K):
                d = dest_ref[0, 0, t * TOP_K + k]
                pltpu.make_async_copy(src_ref.at[pl.ds(row0 + t, 1)], xs_ref.at[pl.ds(d, 1)], sem.at[0]).start()
            return c
        lax.fori_loop(0, ROUTE_TOK, body, 0)

    @pl.when(i < prompt_tiles)
    def _():
        issue(hp_ref, i * ROUTE_TOK)

    @pl.when(i >= prompt_tiles)
    def _():
        issue(hs_ref, (i - prompt_tiles) * ROUTE_TOK)

    def wait_step():
        pltpu.make_async_copy(hp_ref.at[pl.ds(0, rows_per_step)], xs_ref.at[pl.ds(0, rows_per_step)],
                              sem.at[0]).wait()

    @pl.when(i > 0)
    def _():
        wait_step()

    @pl.when(i == n_tiles - 1)
    def _():
        wait_step()
        zero_copies(lambda cp: cp.wait())


def moe_dispatch(dest, hn_p, hn_s, pad_start, pad_len, n_used, n_blocks):
    tp, d = hn_p.shape
    tsn = hn_s.shape[0]
    assert tp % ROUTE_TOK == 0 and tsn % ROUTE_TOK == 0 and tp >= ROUTE_TOK * TOP_K
    prompt_tiles = tp // ROUTE_TOK
    n_tiles = prompt_tiles + tsn // ROUTE_TOK
    kern = functools.partial(_dispatch_kernel, prompt_tiles=prompt_tiles, n_tiles=n_tiles, n_blocks=n_blocks)
    return pl.pallas_call(
        kern,
        grid_spec=pltpu.PrefetchScalarGridSpec(
            num_scalar_prefetch=3,
            grid=(n_tiles,),
            in_specs=[
                pl.BlockSpec((1, 1, ROUTE_TOK * TOP_K), lambda i, a, b, c: (i, 0, 0), memory_space=pltpu.SMEM),
                pl.BlockSpec(memory_space=pl.ANY),
                pl.BlockSpec(memory_space=pl.ANY),
            ],
            out_specs=pl.BlockSpec(memory_space=pl.ANY),
            scratch_shapes=[pltpu.VMEM((MOE_ROWS, d), F32), pltpu.SemaphoreType.DMA((2,))],
        ),
        out_shape=jax.ShapeDtypeStruct((n_blocks * MOE_ROWS, d), F32),
        compiler_params=_cparams(("arbitrary",)),
        name="moe_dispatch",
    )(pad_start, pad_len, n_used, dest.reshape(n_tiles, 1, ROUTE_TOK * TOP_K), hn_p, hn_s)


def _moe_kernel(be_ref, nu_ref, x_ref, wgu_ref, bgu_ref, wd_ref, bd_ref, o_ref, wgu16_ref, wd16_ref):
    b = pl.program_id(0)
    used = b < nu_ref[0]
    new_expert = jnp.logical_or(b == 0, be_ref[b] != be_ref[jnp.maximum(b - 1, 0)])

    @pl.when(jnp.logical_and(used, new_expert))
    def _():
        wgu16_ref[...] = wgu_ref[0].astype(BF16)
        wd16_ref[...] = wd_ref[0].astype(BF16)

    @pl.when(used)
    def _():
        x = x_ref[...].astype(BF16)
        gu = _dot(x, wgu16_ref[...]) + bgu_ref[0]
        d_ff = gu.shape[1] // 2
        gate = jnp.minimum(gu[:, :d_ff], SWIGLU_LIMIT)
        up = jnp.clip(gu[:, d_ff:], -SWIGLU_LIMIT, SWIGLU_LIMIT)
        act = (up + 1.0) * gate * jax.nn.sigmoid(SWIGLU_ALPHA * gate)
        o_ref[...] = _dot(act.astype(BF16), wd16_ref[...]) + bd_ref[0]

    @pl.when(jnp.logical_not(used))
    def _():
        o_ref[...] = jnp.zeros(o_ref.shape, F32)


def moe_experts(xs, block_e, n_used, w_gu, b_gu, w_down, b_down):
    rows, d = xs.shape
    nb = rows // MOE_ROWS
    e, _, gu_w = w_gu.shape
    last = lambda b, nu: jnp.minimum(b, nu[0] - 1)
    return pl.pallas_call(
        _moe_kernel,
        grid_spec=pltpu.PrefetchScalarGridSpec(
            num_scalar_prefetch=2,
            grid=(nb,),
            in_specs=[
                pl.BlockSpec((MOE_ROWS, d), lambda b, be, nu: (last(b, nu), 0)),
                pl.BlockSpec((1, d, gu_w), lambda b, be, nu: (be[last(b, nu)], 0, 0)),
                pl.BlockSpec((1, 1, gu_w), lambda b, be, nu: (be[last(b, nu)], 0, 0)),
                pl.BlockSpec((1, gu_w // 2, d), lambda b, be, nu: (be[last(b, nu)], 0, 0)),
                pl.BlockSpec((1, 1, d), lambda b, be, nu: (be[last(b, nu)], 0, 0)),
            ],
            out_specs=pl.BlockSpec((MOE_ROWS, d), lambda b, be, nu: (b, 0)),
            scratch_shapes=[pltpu.VMEM((d, gu_w), BF16), pltpu.VMEM((gu_w // 2, d), BF16)],
        ),
        out_shape=jax.ShapeDtypeStruct((rows, d), F32),
        compiler_params=_cparams(("arbitrary",)),
        name="moe_experts",
    )(block_e, n_used, xs, w_gu, b_gu.reshape(e, 1, gu_w), w_down, b_down.reshape(e, 1, d))


def _combine_kernel(dfirst_ref, dnext_ref, x2_ref, gate_ref, g_ref, ys_ref, o_ref, buf_ref, sem, *, n_tiles):
    i = pl.program_id(0)
    slot = i % 2

    def issue(dref, sl):
        def body(t, c):
            for k in range(TOP_K):
                d = dref[0, 0, t * TOP_K + k]
                pltpu.make_async_copy(ys_ref.at[pl.ds(d, 1)], buf_ref.at[sl, k, pl.ds(t, 1)], sem.at[sl]).start()
            return c
        lax.fori_loop(0, ROUTE_TOK, body, 0)

    @pl.when(i == 0)
    def _():
        issue(dfirst_ref, 0)

    @pl.when(i + 1 < n_tiles)
    def _():
        issue(dnext_ref, 1 - slot)

    for k in range(TOP_K):
        pltpu.make_async_copy(ys_ref.at[pl.ds(0, ROUTE_TOK)], buf_ref.at[slot, k], sem.at[slot]).wait()

    gates = gate_ref[...]
    moe = buf_ref[slot, 0] * gates[:, 0:1]
    for k in range(1, TOP_K):
        moe = moe + buf_ref[slot, k] * gates[:, k:k + 1]
    tok = x2_ref[...] + moe
    y = tok * lax.rsqrt(jnp.mean(tok * tok, axis=-1, keepdims=True) + EPS)
    o_ref[...] = y * g_ref[...]


def combine(x2, ys, dest, gates, norm_final):
    t, d = x2.shape
    assert t % ROUTE_TOK == 0
    n_tiles = t // ROUTE_TOK
    dest3 = dest.reshape(n_tiles, 1, ROUTE_TOK * TOP_K)
    kern = functools.partial(_combine_kernel, n_tiles=n_tiles)
    smem = lambda imap: pl.BlockSpec((1, 1, ROUTE_TOK * TOP_K), imap, memory_space=pltpu.SMEM)
    return pl.pallas_call(
        kern,
        grid=(n_tiles,),
        in_specs=[
            smem(lambda i: (0, 0, 0)),
            smem(lambda i: (jnp.minimum(i + 1, n_tiles - 1), 0, 0)),
            pl.BlockSpec((ROUTE_TOK, d), lambda i: (i, 0)),
            pl.BlockSpec((ROUTE_TOK, ROUTE_W), lambda i: (i, 0)),
            pl.BlockSpec((1, d), lambda i: (0, 0)),
            pl.BlockSpec(memory_space=pl.ANY),
        ],
        out_specs=pl.BlockSpec((ROUTE_TOK, d), lambda i: (i, 0)),
        out_shape=jax.ShapeDtypeStruct((t, d), F32),
        scratch_shapes=[pltpu.VMEM((2, TOP_K, ROUTE_TOK, d), F32), pltpu.SemaphoreType.DMA((2,))],
        compiler_params=_cparams(("arbitrary",)),
        name="combine",
    )(dest3, dest3, x2, gates, norm_final.reshape(1, d), ys)


def kernel(x_prompt, x_sample, cache_k, cache_v, state_ssm, state_conv, norm_mix, w_in, conv_w, conv_b, dt_bias, a_log, d_skip, ssd_norm, w_o_ssd, lambda_q1, lambda_k1, lambda_q2, lambda_k2, subln, w_o_att, w_out, norm_ffn, router_w, router_b, w_gu, b_gu, w_down, b_down, norm_final):
    bp, s, d = x_prompt.shape
    bs, ts, _ = x_sample.shape
    past = cache_k.shape[2]
    depth = w_in.shape[0]
    assert depth == 1
    layer = 0
    lambda_init = 0.8 - 0.6 * math.exp(-0.3 * layer)
    tp, tsn = bp * s, bs * ts
    t_all = tp + tsn

    heads = ATT_N_HEADS
    xp2 = x_prompt.reshape(tp, d)
    xs2 = x_sample.reshape(tsn, d)

    sizes = (SSD_D_INNER, SSD_CONV_CH, SSD_N_HEADS, d, d, d, d, d)
    offs = [0]
    for sz in sizes:
        offs.append(offs[-1] + sz)
    w = w_in[layer]
    seg = lambda i: w[:, offs[i]:offs[i + 1]]
    dt_pad = jnp.zeros((d, PROJ_XD_W - SSD_CONV_CH - SSD_N_HEADS), F32)
    w_all = jnp.concatenate([seg(4), seg(5), seg(0), seg(3), seg(6), seg(7), seg(1), seg(2), dt_pad],
                            axis=1).astype(BF16)
    k_p, v_p, kv16_p, zq_p, xd_p = in_proj(xp2, norm_mix[layer], w_all, tm=1024)
    k_s, v_s, kv16_s, zq_s, xd_s = in_proj(xs2, norm_mix[layer], w_all, tm=tsn)

    prev_p = jnp.zeros((bp, SUBLANES, SSD_CONV_CH), F32)
    h0_p = jnp.zeros((bp, SSD_D_STATE, SSD_D_INNER), F32)
    ssd_w = (conv_w[layer], conv_b[layer], dt_bias[layer], a_log[layer], d_skip[layer], ssd_norm[layer])
    y_p, hT_p = ssd_mixer(xd_p, zq_p, 0, prev_p, h0_p, *ssd_w, b=bp, l=s, tl=256, valid_len=None, out_rows=tp)

    pad_rows = CHUNK - ts
    pad_seq = lambda a: jnp.pad(a.reshape(bs, ts, -1), ((0, 0), (0, pad_rows), (0, 0))).reshape(bs * CHUNK, -1)
    prev_s = jnp.pad(state_conv[layer], ((0, 0), (SUBLANES - (SSD_CONV_WIDTH - 1), 0), (0, 0)))
    h0_s = jnp.swapaxes(state_ssm[layer].reshape(bs, SSD_D_INNER, SSD_D_STATE), 1, 2)
    y_s, hT_s = ssd_mixer(pad_seq(xd_s), pad_seq(zq_s[:, :SSD_D_INNER]), 0, prev_s, h0_s, *ssd_w, b=bs, l=CHUNK,
                          tl=CHUNK, valid_len=ts, out_rows=bs * CHUNK)
    y_s = y_s.reshape(bs, CHUNK, -1)[:, :ts].reshape(tsn, -1)

    lam_rows = jnp.concatenate([lambda_q1[layer][None], lambda_k1[layer][None], lambda_q2[layer][None],
                                lambda_k2[layer][None], jnp.zeros((4, ATT_HEAD_DIM), F32)], axis=0)
    o_p = diff_attention_t(zq_p, 0, SSD_D_INNER // d, kv16_p, 0, kv16_p, 1, lam_rows, subln[layer], b=bp, lq=s,
                           lk=s, tq=512, tk=512, q_off=0, kv_len=s, lambda_init=lambda_init, out_rows=tp)
    kv_new = kv16_s.reshape(bs, ts, -1)
    kv_len = past + ts
    kv_pad = (-kv_len) % LANES
    lk_s = kv_len + kv_pad
    k_all = jnp.concatenate([cache_k[layer].reshape(bs, past, -1).astype(BF16), kv_new[:, :, :d],
                             jnp.zeros((bs, kv_pad, d), BF16)], axis=1).reshape(bs * lk_s, d)
    v_all = jnp.concatenate([cache_v[layer].reshape(bs, past, -1).astype(BF16), kv_new[:, :, d:],
                             jnp.zeros((bs, kv_pad, d), BF16)], axis=1).reshape(bs * lk_s, d)
    q_s = jnp.pad(zq_s[:, SSD_D_INNER:SSD_D_INNER + d].reshape(bs, ts, d), ((0, 0), (0, LANES - ts), (0, 0)))
    o_s = diff_attention_t(q_s.reshape(bs * LANES, d), 0, 0, k_all, 0, v_all, 0, lam_rows, subln[layer], b=bs,
                           lq=LANES, lk=lk_s, tq=LANES, tk=lk_s, q_off=past, kv_len=kv_len,
                           lambda_init=lambda_init, out_rows=bs * LANES)
    o_s = o_s.reshape(bs, LANES, d)[:, :ts].reshape(tsn, d)

    rw = jnp.zeros((d, LANES), F32).at[:, :N_EXPERTS].set(router_w[layer])
    rb = jnp.zeros((1, LANES), F32).at[0, :N_EXPERTS].set(router_b[layer])
    mix_w = (w_o_ssd[layer].astype(BF16), w_o_att[layer].astype(BF16), w_out[layer].astype(BF16), norm_ffn[layer],
             rw, rb)
    x2_p, hn_p, gate_p, sel_p, cnt_p = mixer_out(y_p, o_p, zq_p, 3, 4, xp2, *mix_w,
                                                 jnp.zeros((1, LANES), F32), tm=512)
    x2_s, hn_s, gate_s, sel_s, cnt_all = mixer_out(y_s, o_s, zq_s, 3, 4, xs2, *mix_w, cnt_p, tm=tsn)

    sel = jnp.concatenate([sel_p, sel_s], axis=0)
    top_e = sel[:, :TOP_K]
    rank = sel[:, TOP_K:]
    cnt = cnt_all[0, :N_EXPERTS].astype(jnp.int32)
    padded = (cnt + MOE_ROWS - 1) // MOE_ROWS * MOE_ROWS
    ends = jnp.cumsum(padded)
    starts = ends - padded
    dest = (starts[top_e] + rank).reshape(-1)
    nb = (t_all * TOP_K + N_EXPERTS * (MOE_ROWS - 1) + MOE_ROWS - 1) // MOE_ROWS
    block_start = jnp.arange(nb, dtype=jnp.int32) * MOE_ROWS
    block_e = jnp.minimum(jnp.sum((ends[None, :] <= block_start[:, None]).astype(jnp.int32), axis=1),
                          N_EXPERTS - 1)
    n_used = (ends[-1] // MOE_ROWS).astype(jnp.int32).reshape(1)

    xs_sorted = moe_dispatch(dest, hn_p, hn_s, starts + cnt, padded - cnt, n_used, nb)
    ys = moe_experts(xs_sorted, block_e, n_used, w_gu[layer], b_gu[layer], w_down[layer], b_down[layer])
    y_prompt = combine(x2_p, ys, dest[:tp * TOP_K], gate_p, norm_final).reshape(bp, s, d)
    y_sample = combine(x2_s, ys, dest[tp * TOP_K:], gate_s, norm_final).reshape(bs, ts, d)

    new_k_p = k_p.reshape(1, bp, s, heads, 2 * ATT_HEAD_DIM)
    new_v_p = v_p.reshape(1, bp, s, heads, ATT_V_HEAD)
    new_k_s = k_s.reshape(1, bs, ts, heads, 2 * ATT_HEAD_DIM)
    new_v_s = v_s.reshape(1, bs, ts, heads, ATT_V_HEAD)
    ssm_p = jnp.swapaxes(hT_p, 1, 2).reshape(1, bp, SSD_N_HEADS, SSD_HEAD_DIM, SSD_D_STATE)
    ssm_s = jnp.swapaxes(hT_s, 1, 2).reshape(1, bs, SSD_N_HEADS, SSD_HEAD_DIM, SSD_D_STATE)
    keep = SSD_CONV_WIDTH - 1
    conv_p = xd_p.reshape(bp, s, -1)[:, s - keep:, :SSD_CONV_CH][None]
    raw_s = jnp.concatenate([state_conv[layer], xd_s.reshape(bs, ts, -1)[:, :, :SSD_CONV_CH]], axis=1)
    conv_s = raw_s[:, -keep:][None]
    return (y_prompt, y_sample, new_k_p, new_v_p, ssm_p, conv_p, new_k_s, new_v_s, ssm_s, conv_s)
```

```python
import functools
import math

import jax
import jax.numpy as jnp
from jax import lax
from jax.experimental import pallas as pl
from jax.experimental.pallas import tpu as pltpu

EPS = 1e-5
CHUNK = 64
D_MODEL = 1024
SSD_D_INNER = 2048
SSD_HEAD_DIM = 64
SSD_N_HEADS = 32
SSD_GROUPS = 4
SSD_D_STATE = 128
SSD_CONV_WIDTH = 4
SSD_CONV_CH = 3072
ATT_HEAD_DIM = 64
ATT_N_HEADS = 8
ATT_V_HEAD = 128
ALIBI_MAX_BIAS = 8.0
N_EXPERTS = 32
TOP_K = 4
SWIGLU_LIMIT = 7.0
SWIGLU_ALPHA = 1.702

LANES = 128
SUBLANES = 8
VMEM_LIMIT = 56 * 1024 * 1024
NEG_BIG = -1e30

BF16 = jnp.bfloat16
F32 = jnp.float32


def _cparams(sem, flags=None):
    return pltpu.CompilerParams(dimension_semantics=sem, vmem_limit_bytes=VMEM_LIMIT, flags=flags)


def _split3(x):
    h1 = x.astype(BF16)
    r1 = x - h1.astype(F32)
    h2 = r1.astype(BF16)
    h3 = (r1 - h2.astype(F32)).astype(BF16)
    return h1, h2, h3


def _dot(a, b):
    return jnp.dot(a, b, preferred_element_type=F32)


def _dot_exact_rhs(x, m):
    h1, h2, h3 = _split3(x)
    return _dot(h1, m) + _dot(h2, m) + _dot(h3, m)


def _dot_exact_lhs(m, x):
    h1, h2, h3 = _split3(x)
    return _dot(m, h1) + _dot(m, h2) + _dot(m, h3)


PROJ_TN = 512
PROJ_K = (0, D_MODEL // PROJ_TN)
PROJ_V = (PROJ_K[1], PROJ_K[1] + D_MODEL // PROJ_TN)
PROJ_ZQ_W = SSD_D_INNER + 3 * D_MODEL
PROJ_ZQ = (PROJ_V[1], PROJ_V[1] + PROJ_ZQ_W // PROJ_TN)
PROJ_XD_W = -(-(SSD_CONV_CH + SSD_N_HEADS) // PROJ_TN) * PROJ_TN
PROJ_XD = (PROJ_ZQ[1], PROJ_ZQ[1] + PROJ_XD_W // PROJ_TN)


def _in_proj_kernel(x_ref, g_ref, w_ref, k_ref, v_ref, kv16_ref, zq_ref, xd_ref, xn_ref):
    j = pl.program_id(1)

    @pl.when(j == 0)
    def _():
        x = x_ref[...]
        y = x * lax.rsqrt(jnp.mean(x * x, axis=-1, keepdims=True) + EPS)
        xn_ref[...] = (y * g_ref[...]).astype(BF16)

    @pl.when(j < PROJ_K[1])
    def _():
        acc = _dot(xn_ref[...], w_ref[...])
        k_ref[...] = acc
        kv16_ref[...] = acc.astype(BF16)

    @pl.when(jnp.logical_and(j >= PROJ_V[0], j < PROJ_V[1]))
    def _():
        acc = _dot(xn_ref[...], w_ref[...])
        v_ref[...] = acc
        kv16_ref[...] = acc.astype(BF16)

    @pl.when(jnp.logical_and(j >= PROJ_ZQ[0], j < PROJ_ZQ[1]))
    def _():
        zq_ref[...] = _dot(xn_ref[...], w_ref[...]).astype(BF16)

    @pl.when(j >= PROJ_XD[0])
    def _():
        xd_ref[...] = _dot(xn_ref[...], w_ref[...])


def in_proj(x, gain, w, tm):
    t, d = x.shape
    assert t % tm == 0 and w.shape[1] == PROJ_XD[1] * PROJ_TN
    tn = PROJ_TN

    def out_map(rng):
        return lambda i, j: (i, jnp.clip(j - rng[0], 0, rng[1] - rng[0] - 1))

    return pl.pallas_call(
        _in_proj_kernel,
        grid=(t // tm, PROJ_XD[1]),
        in_specs=[
            pl.BlockSpec((tm, d), lambda i, j: (i, 0)),
            pl.BlockSpec((1, d), lambda i, j: (0, 0)),
            pl.BlockSpec((d, tn), lambda i, j: (0, j)),
        ],
        out_specs=[
            pl.BlockSpec((tm, tn), out_map(PROJ_K)),
            pl.BlockSpec((tm, tn), out_map(PROJ_V)),
            pl.BlockSpec((tm, tn), out_map((PROJ_K[0], PROJ_V[1]))),
            pl.BlockSpec((tm, tn), out_map(PROJ_ZQ)),
            pl.BlockSpec((tm, tn), out_map(PROJ_XD)),
        ],
        out_shape=[
            jax.ShapeDtypeStruct((t, d), F32),
            jax.ShapeDtypeStruct((t, d), F32),
            jax.ShapeDtypeStruct((t, 2 * d), BF16),
            jax.ShapeDtypeStruct((t, PROJ_ZQ_W), BF16),
            jax.ShapeDtypeStruct((t, PROJ_XD_W), F32),
        ],
        scratch_shapes=[pltpu.VMEM((tm, d), BF16)],
        compiler_params=_cparams(("arbitrary", "arbitrary")),
        name="in_proj",
    )(x, gain.reshape(1, d), w)


CONV_ROWS, CONV_COLS = 64, 512


def _ssd_kernel(xbc_ref, z_ref, prev_ref, h0_ref, cw_ref, cb_ref, dtb_ref, alog_ref, dsk_ref, gn_ref,
                y_ref, hout_ref, xpad_ref, xc_ref, tail_ref, st_ref, *, tl, valid_len):
    t = pl.program_id(1)
    nt = pl.num_programs(1)
    nch = tl // CHUNK
    dinner = SSD_D_INNER
    gw = dinner // SSD_GROUPS
    ns = SSD_D_STATE

    @pl.when(t == 0)
    def _():
        tail_ref[...] = prev_ref[0]
        st_ref[...] = h0_ref[0]

    xpad_ref[0:SUBLANES, :] = tail_ref[...]
    xpad_ref[SUBLANES:SUBLANES + tl, :] = xbc_ref[:, 0:SSD_CONV_CH]
    tail_ref[...] = xbc_ref[tl - SUBLANES:tl, 0:SSD_CONV_CH]
    for r in range(0, tl, CONV_ROWS):
        for c in range(0, SSD_CONV_CH, CONV_COLS):
            acc = cb_ref[:, c:c + CONV_COLS]
            for k in range(SSD_CONV_WIDTH):
                off = r + SUBLANES - (SSD_CONV_WIDTH - 1) + k
                acc = acc + xpad_ref[off:off + CONV_ROWS, c:c + CONV_COLS] * cw_ref[k:k + 1, c:c + CONV_COLS]
            xc_ref[r:r + CONV_ROWS, c:c + CONV_COLS] = acc * jax.nn.sigmoid(acc)

    head_of_lane = lax.broadcasted_iota(jnp.int32, (SSD_N_HEADS, dinner), 1) // SSD_HEAD_DIM
    expand = (head_of_lane == lax.broadcasted_iota(jnp.int32, (SSD_N_HEADS, dinner), 0)).astype(BF16)
    ti = lax.broadcasted_iota(jnp.int32, (CHUNK, CHUNK), 0)
    si = lax.broadcasted_iota(jnp.int32, (CHUNK, CHUNK), 1)
    tril = (si <= ti).astype(BF16)
    row_c = lax.broadcasted_iota(jnp.int32, (CHUNK, dinner), 0)
    pos_in_head = lax.broadcasted_iota(jnp.int32, (CHUNK, dinner), 1) % CHUNK
    upper = row_c <= pos_in_head
    row_p = lax.broadcasted_iota(jnp.int32, (CHUNK, LANES), 0)
    lane_p = lax.broadcasted_iota(jnp.int32, (CHUNK, LANES), 1)
    causal_pair = (lane_p % CHUNK) <= row_p
    left_half = lane_p < SSD_HEAD_DIM
    a_neg_e = _dot_exact_rhs(-jnp.exp(alog_ref[...]), expand)
    dsk_e = _dot_exact_rhs(dsk_ref[...], expand)

    def chunk_body(c, carry):
        r0 = pl.multiple_of(c * CHUNK, CHUNK)
        xs = xc_ref[pl.ds(r0, CHUNK), 0:dinner]
        dt_raw = xbc_ref[pl.ds(r0, CHUNK), SSD_CONV_CH:SSD_CONV_CH + SSD_N_HEADS]
        dtv = dt_raw + dtb_ref[...]
        dt = jnp.maximum(dtv, 0.0) + jnp.log1p(jnp.exp(-jnp.abs(dtv)))
        if valid_len is not None:
            rows = t * tl + r0 + lax.broadcasted_iota(jnp.int32, (CHUNK, SSD_N_HEADS), 0)
            dt = jnp.where(rows < valid_len, dt, 0.0)
        dt_e = _dot_exact_rhs(dt, expand)
        a_e = dt_e * a_neg_e
        acs_e = _dot_exact_lhs(tril, a_e)
        rowterm = jnp.sum(jnp.where(upper, a_e, 0.0), axis=0, keepdims=True)
        acs_last = acs_e[CHUNK - 1:CHUNK, :]
        xdt = xs * dt_e
        x_dec = (xdt * jnp.exp(acs_last - acs_e)).astype(BF16)
        e_acs = jnp.exp(acs_e)
        e_last = jnp.exp(acs_last)

        y_parts = []
        for g in range(SSD_GROUPS):
            lo = g * gw
            bm = xc_ref[pl.ds(r0, CHUNK), dinner + g * ns:dinner + (g + 1) * ns].astype(BF16)
            cm = xc_ref[pl.ds(r0, CHUNK), dinner + SSD_GROUPS * ns + g * ns:
                        dinner + SSD_GROUPS * ns + (g + 1) * ns].astype(BF16)
            b2 = jnp.concatenate([bm, bm], axis=0)
            cb2 = lax.dot_general(cm, b2, (((1,), (1,)), ((), ())), preferred_element_type=F32)
            st_g = st_ref[:, lo:lo + gw]
            y_off = _dot(cm, st_g.astype(BF16)) * e_acs[:, lo:lo + gw]
            pieces = []
            for j in range(gw // LANES):
                l0 = lo + j * LANES
                diff = acs_e[:, l0:l0 + LANES] - rowterm[:, l0:l0 + LANES]
                dec = jnp.where(causal_pair, jnp.exp(diff), 0.0)
                scores = (cb2 * dec).astype(BF16)
                xp = xdt[:, l0:l0 + LANES]
                xblk = jnp.concatenate([jnp.where(left_half, xp, 0.0), jnp.where(left_half, 0.0, xp)],
                                       axis=0).astype(BF16)
                pieces.append(_dot(scores, xblk))
            y_diag = jnp.concatenate(pieces, axis=1)
            upd = lax.dot_general(bm, x_dec[:, lo:lo + gw], (((0,), (0,)), ((), ())),
                                  preferred_element_type=F32)
            st_ref[:, lo:lo + gw] = e_last[:, lo:lo + gw] * st_g + upd
            yg = y_diag + y_off + xs[:, lo:lo + gw] * dsk_e[:, lo:lo + gw]
            zg = z_ref[pl.ds(r0, CHUNK), lo:lo + gw].astype(F32)
            yg = yg * (zg * jax.nn.sigmoid(zg))
            yn = yg * lax.rsqrt(jnp.mean(yg * yg, axis=-1, keepdims=True) + EPS)
            y_parts.append((yn * gn_ref[:, lo:lo + gw]).astype(y_ref.dtype))
        y_ref[pl.ds(r0, CHUNK), :] = jnp.concatenate(y_parts, axis=1)
        return carry

    lax.fori_loop(0, nch, chunk_body, 0)

    @pl.when(t == nt - 1)
    def _():
        hout_ref[0] = st_ref[...]


def ssd_mixer(xbc, z_src, z_col, conv_prev8, h0_t, conv_w, conv_b, dt_bias, a_log, d_skip, ssd_norm,
              b, l, tl, valid_len, out_rows):
    wx = xbc.shape[1]
    assert l % tl == 0 and tl % CHUNK == 0 and CHUNK == SSD_HEAD_DIM
    nt = l // tl
    kern = functools.partial(_ssd_kernel, tl=tl, valid_len=valid_len)
    full = lambda shape: pl.BlockSpec(shape, lambda i, j: (0,) * len(shape))
    return pl.pallas_call(
        kern,
        grid=(b, nt),
        in_specs=[
            pl.BlockSpec((tl, wx), lambda i, j: (i * nt + j, 0)),
            pl.BlockSpec((tl, SSD_D_INNER), lambda i, j: (i * nt + j, z_col)),
            pl.BlockSpec((1, SUBLANES, SSD_CONV_CH), lambda i, j: (i, 0, 0)),
            pl.BlockSpec((1, SSD_D_STATE, SSD_D_INNER), lambda i, j: (i, 0, 0)),
            full((SSD_CONV_WIDTH, SSD_CONV_CH)),
            full((1, SSD_CONV_CH)),
            full((1, SSD_N_HEADS)),
            full((1, SSD_N_HEADS)),
            full((1, SSD_N_HEADS)),
            full((1, SSD_D_INNER)),
        ],
        out_specs=[
            pl.BlockSpec((tl, SSD_D_INNER), lambda i, j: (i * nt + j, 0)),
            pl.BlockSpec((1, SSD_D_STATE, SSD_D_INNER), lambda i, j: (i, 0, 0)),
        ],
        out_shape=[
            jax.ShapeDtypeStruct((out_rows, SSD_D_INNER), BF16),
            jax.ShapeDtypeStruct((b, SSD_D_STATE, SSD_D_INNER), F32),
        ],
        scratch_shapes=[
            pltpu.VMEM((tl + SUBLANES, SSD_CONV_CH), F32),
            pltpu.VMEM((tl, SSD_CONV_CH), F32),
            pltpu.VMEM((SUBLANES, SSD_CONV_CH), F32),
            pltpu.VMEM((SSD_D_STATE, SSD_D_INNER), F32),
        ],
        compiler_params=_cparams(("arbitrary", "arbitrary")),
        name="ssd_mixer",
    )(xbc, z_src, conv_prev8, h0_t, conv_w, conv_b.reshape(1, -1), dt_bias.reshape(1, -1),
      a_log.reshape(1, -1), d_skip.reshape(1, -1), ssd_norm.reshape(1, -1))


BIAS_SPLIT = 32
FLAG_FIRST, FLAG_LAST, FLAG_MASK = 1, 2, 4
HEADS_PER_GROUP = 2


def _attn_t_kernel(qi_ref, ki_ref, fl_ref, q_ref, k_ref, v_ref, slope_ref, lam_ref, sub_ref, o_ref,
                   m_ref, l_ref, acc_ref, kb_ref, corr_ref, *, tq, tk, q_off, kv_len, lambda_init):
    p_idx = pl.program_id(1)
    qi = qi_ref[p_idx]
    ki = ki_ref[p_idx]
    flags = fl_ref[p_idx]
    qstart = q_off + qi * tq
    kstart = ki * tk
    half = ATT_HEAD_DIM
    scale = ATT_HEAD_DIM ** -0.5

    @pl.when((flags & FLAG_FIRST) != 0)
    def _():
        m_ref[...] = jnp.full(m_ref.shape, NEG_BIG, F32)
        l_ref[...] = jnp.zeros(l_ref.shape, F32)
        acc_ref[...] = jnp.zeros(acc_ref.shape, F32)

    lane_k = lax.broadcasted_iota(jnp.int32, (tk, LANES), 1)
    rel = kstart - qstart + lax.broadcasted_iota(jnp.int32, (tk, LANES), 0)
    hi = (rel // BIAS_SPLIT) * BIAS_SPLIT
    lo = rel - hi
    pos_lane = lane_k % half
    kb_ref[...] = jnp.where(pos_lane == 0, hi, jnp.where(pos_lane == 1, lo, 0)).astype(F32).astype(BF16)

    needs_mask = (flags & FLAG_MASK) != 0

    @pl.when(needs_mask)
    def _():
        kpos = kstart + lax.broadcasted_iota(jnp.int32, (tk, tq), 0)
        qpos = qstart + lax.broadcasted_iota(jnp.int32, (tk, tq), 1)
        allowed = jnp.logical_and(kpos // CHUNK <= qpos // CHUNK, kpos < kv_len)
        after = jnp.where(kpos > qpos, 2.0 * (qpos - kpos).astype(F32), 0.0)
        corr_ref[0] = after
        corr_ref[1] = jnp.where(allowed, 0.0, NEG_BIG)

    lane_q = lax.broadcasted_iota(jnp.int32, (tq, LANES), 1)

    def head_group(g, masked):
        chains = []
        for hh in range(HEADS_PER_GROUP):
            h = g * HEADS_PER_GROUP + hh
            c0 = pl.multiple_of(h * LANES, LANES)
            q = q_ref[:, pl.ds(c0, LANES)] * jnp.asarray(scale, BF16)
            k = k_ref[:, pl.ds(c0, LANES)]
            v = v_ref[:, pl.ds(c0, LANES)]
            kb = kb_ref[...]
            slope = slope_ref[pl.ds(h, 1), :]
            slope_b = jnp.broadcast_to(slope.astype(BF16), (tq, LANES))
            zero_q = jnp.zeros((tq, LANES), BF16)
            for idx in range(2):
                own = (lane_q < half) if idx == 0 else (lane_q >= half)
                q_aug = jnp.where(own, q, jnp.where((lane_q % half) < 2, slope_b, zero_q))
                own_k = (lane_k < half) if idx == 0 else (lane_k >= half)
                k_aug = jnp.where(own_k, k, kb)
                chains.append((h, idx, q_aug, k_aug, v, slope, m_ref[idx, h], l_ref[idx, h]))
        results = []
        for h, idx, q_aug, k_aug, v, slope, m_prev, l_prev in chains:
            s = lax.dot_general(k_aug, q_aug, (((1,), (1,)), ((), ())), preferred_element_type=F32)
            if masked:
                s = s + slope[:, 0:1] * corr_ref[0] + corr_ref[1]
            m_new = jnp.maximum(m_prev, jnp.max(s, axis=0, keepdims=True))
            alpha = jnp.exp(m_prev - m_new)
            p = jnp.exp(s - m_new)
            l_new = alpha * l_prev + jnp.sum(p, axis=0, keepdims=True)
            pv = lax.dot_general(v, p.astype(BF16), (((0,), (0,)), ((), ())), preferred_element_type=F32)
            results.append((h, idx, m_new, l_new, alpha, pv))
        for h, idx, m_new, l_new, alpha, pv in results:
            m_ref[idx, h] = m_new
            l_ref[idx, h] = l_new
            acc_ref[idx, h] = alpha * acc_ref[idx, h] + pv

    n_groups = ATT_N_HEADS // HEADS_PER_GROUP

    @pl.when(needs_mask)
    def _():
        lax.fori_loop(0, n_groups, lambda g, c: (head_group(g, True), c)[1], 0)

    @pl.when(jnp.logical_not(needs_mask))
    def _():
        lax.fori_loop(0, n_groups, lambda g, c: (head_group(g, False), c)[1], 0)

    @pl.when((flags & FLAG_LAST) != 0)
    def _():
        lp = lam_ref[...]
        lam = (jnp.exp(jnp.sum(lp[0:1] * lp[1:2], axis=-1, keepdims=True))
               - jnp.exp(jnp.sum(lp[2:3] * lp[3:4], axis=-1, keepdims=True)) + lambda_init)

        def fin(h, c):
            c0 = pl.multiple_of(h * LANES, LANES)
            o = acc_ref[0, h] / l_ref[0, h] - lam * (acc_ref[1, h] / l_ref[1, h])
            on = o * lax.rsqrt(jnp.mean(o * o, axis=0, keepdims=True) + EPS)
            on = (on * sub_ref[...]) * (1.0 - lambda_init)
            o_ref[:, pl.ds(c0, LANES)] = on.T.astype(o_ref.dtype)
            return c

        lax.fori_loop(0, ATT_N_HEADS, fin, 0)


def _attn_pairs(nq, nk, tq, tk, q_off, kv_len):
    qis, kis, fls = [], [], []
    for qi in range(nq):
        first_q = q_off + qi * tq
        last_q = first_q + tq - 1
        kend = min((last_q // CHUNK + 1) * CHUNK, kv_len)
        nkv = -(-kend // tk)
        for ki in range(nkv):
            block_end = (ki + 1) * tk
            plain = block_end <= (first_q // CHUNK) * CHUNK and block_end <= kv_len
            fl = (FLAG_FIRST if ki == 0 else 0) | (FLAG_LAST if ki == nkv - 1 else 0) | (0 if plain else FLAG_MASK)
            qis.append(qi)
            kis.append(ki)
            fls.append(fl)
    return qis, kis, fls


def diff_attention_t(q_src, q_blk0, q_colblk, k_src, k_colblk, v_src, v_colblk, lam_rows, subln, *, b, lq, lk,
                     tq, tk, q_off, kv_len, lambda_init, out_rows):
    assert lq % tq == 0 and lk % tk == 0 and tq % LANES == 0
    assert q_off + lq <= 256 * BIAS_SPLIT + tq
    nq, nk = lq // tq, lk // tk
    width = ATT_N_HEADS * LANES
    qis, kis, fls = _attn_pairs(nq, nk, tq, tk, q_off, kv_len)
    kern = functools.partial(_attn_t_kernel, tq=tq, tk=tk, q_off=q_off, kv_len=kv_len, lambda_init=lambda_init)
    slopes = jnp.exp2(-ALIBI_MAX_BIAS * jnp.arange(1, ATT_N_HEADS + 1, dtype=F32) / ATT_N_HEADS)
    slopes = jnp.broadcast_to(slopes[:, None], (ATT_N_HEADS, LANES))
    const = lambda shape: pl.BlockSpec(shape, lambda bi, p, qt, kt, ft: (0,) * len(shape))
    return pl.pallas_call(
        kern,
        grid_spec=pltpu.PrefetchScalarGridSpec(
            num_scalar_prefetch=3,
            grid=(b, len(qis)),
            in_specs=[
                pl.BlockSpec((tq, width), lambda bi, p, qt, kt, ft: (q_blk0 + bi * nq + qt[p], q_colblk)),
                pl.BlockSpec((tk, width), lambda bi, p, qt, kt, ft: (bi * nk + kt[p], k_colblk)),
                pl.BlockSpec((tk, width), lambda bi, p, qt, kt, ft: (bi * nk + kt[p], v_colblk)),
                const((ATT_N_HEADS, LANES)),
                const((SUBLANES, ATT_HEAD_DIM)),
                const((ATT_V_HEAD, 1)),
            ],
            out_specs=pl.BlockSpec((tq, width), lambda bi, p, qt, kt, ft: (bi * nq + qt[p], 0)),
            scratch_shapes=[
                pltpu.VMEM((2, ATT_N_HEADS, 1, tq), F32),
                pltpu.VMEM((2, ATT_N_HEADS, 1, tq), F32),
                pltpu.VMEM((2, ATT_N_HEADS, ATT_V_HEAD, tq), F32),
                pltpu.VMEM((tk, LANES), BF16),
                pltpu.VMEM((2, tk, tq), F32),
            ],
        ),
        out_shape=jax.ShapeDtypeStruct((out_rows, width), BF16),
        compiler_params=_cparams(("arbitrary", "arbitrary")),
        name="diff_attention_t",
    )(jnp.asarray(qis, jnp.int32), jnp.asarray(kis, jnp.int32), jnp.asarray(fls, jnp.int32),
      q_src, k_src, v_src, slopes, lam_rows, subln.reshape(-1, 1))


ROUTE_W = 2 * TOP_K


def _mixer_out_kernel(y_ref, o_ref, gs_ref, ga_ref, x_ref, wos_ref, woa_ref, wout_ref, nf_ref, rw_ref, rb_ref,
                      run0_ref, x2_ref, h_ref, gate_ref, sel_ref, cnt_ref, run_ref, *, tm):
    i = pl.program_id(0)

    @pl.when(i == 0)
    def _():
        run_ref[...] = run0_ref[...]

    o_ssd = _dot(y_ref[...], wos_ref[...])
    o_att = _dot(o_ref[...], woa_ref[...])
    merged = (jax.nn.sigmoid(gs_ref[...].astype(F32)) * o_ssd
              + jax.nn.sigmoid(ga_ref[...].astype(F32)) * o_att)
    x2 = x_ref[...] + _dot(merged.astype(BF16), wout_ref[...])
    x2_ref[...] = x2
    hn = x2 * lax.rsqrt(jnp.mean(x2 * x2, axis=-1, keepdims=True) + EPS) * nf_ref[...]
    h_ref[...] = hn

    a1, a2, a3 = _split3(hn)
    w1, w2, w3 = _split3(rw_ref[...])
    logits = (_dot(a1, w1) + _dot(a1, w2) + _dot(a2, w1) + _dot(a2, w2) + _dot(a1, w3) + _dot(a3, w1)
              + rb_ref[...])
    lane = lax.broadcasted_iota(jnp.int32, (tm, LANES), 1)
    work = jnp.where(lane < N_EXPERTS, logits, -jnp.inf)
    tops, idxs = [], []
    for _ in range(TOP_K):
        mx = jnp.max(work, axis=-1, keepdims=True)
        ix = jnp.min(jnp.where(work == mx, lane, LANES), axis=-1, keepdims=True)
        tops.append(mx)
        idxs.append(ix)
        work = jnp.where(lane == ix, -jnp.inf, work)
    es = [jnp.exp(tv - tops[0]) for tv in tops]
    den = es[0] + es[1] + es[2] + es[3]
    gates = jnp.zeros((tm, LANES), F32)
    for k in range(TOP_K):
        gates = jnp.where(lane == k, es[k] / den, gates)
    gate_ref[...] = gates[:, :ROUTE_W]

    chosen = jnp.zeros((tm, LANES), jnp.bool_)
    for k in range(TOP_K):
        chosen = jnp.logical_or(chosen, lane == idxs[k])
    multihot = jnp.where(chosen, 1.0, 0.0).astype(BF16)
    ri = lax.broadcasted_iota(jnp.int32, (tm, tm), 0)
    ci = lax.broadcasted_iota(jnp.int32, (tm, tm), 1)
    strict = jnp.where(ci < ri, 1.0, 0.0).astype(BF16)
    prefix = _dot(strict, multihot) + run_ref[...]
    sel = jnp.zeros((tm, LANES), jnp.int32)
    for k in range(TOP_K):
        rank = jnp.sum(jnp.where(lane == idxs[k], prefix, 0.0), axis=-1, keepdims=True)
        sel = jnp.where(lane == k, idxs[k], sel)
        sel = jnp.where(lane == TOP_K + k, rank.astype(jnp.int32), sel)
    sel_ref[...] = sel[:, :ROUTE_W]
    run_ref[...] = run_ref[...] + jnp.sum(multihot.astype(F32), axis=0, keepdims=True)
    cnt_ref[...] = run_ref[...]


def mixer_out(y_ssd, o_att, gates_src, gs_col, ga_col, x, w_o_ssd, w_o_att, w_out, norm_ffn, rw, rb, counts0, tm):
    t, d = x.shape
    assert t % tm == 0
    kern = functools.partial(_mixer_out_kernel, tm=tm)
    const = lambda shape: pl.BlockSpec(shape, lambda i: (0, 0))
    return pl.pallas_call(
        kern,
        grid=(t // tm,),
        in_specs=[
            pl.BlockSpec((tm, SSD_D_INNER), lambda i: (i, 0)),
            pl.BlockSpec((tm, d), lambda i: (i, 0)),
            pl.BlockSpec((tm, d), lambda i: (i, gs_col)),
            pl.BlockSpec((tm, d), lambda i: (i, ga_col)),
            pl.BlockSpec((tm, d), lambda i: (i, 0)),
            const((SSD_D_INNER, d)),
            const((d, d)),
            const((d, d)),
            const((1, d)),
            const((d, LANES)),
            const((1, LANES)),
            const((1, LANES)),
        ],
        out_specs=[
            pl.BlockSpec((tm, d), lambda i: (i, 0)),
            pl.BlockSpec((tm, d), lambda i: (i, 0)),
            pl.BlockSpec((tm, ROUTE_W), lambda i: (i, 0)),
            pl.BlockSpec((tm, ROUTE_W), lambda i: (i, 0)),
            pl.BlockSpec((1, LANES), lambda i: (0, 0)),
        ],
        out_shape=[
            jax.ShapeDtypeStruct((t, d), F32),
            jax.ShapeDtypeStruct((t, d), F32),
            jax.ShapeDtypeStruct((t, ROUTE_W), F32),
            jax.ShapeDtypeStruct((t, ROUTE_W), jnp.int32),
            jax.ShapeDtypeStruct((1, LANES), F32),
        ],
        scratch_shapes=[pltpu.VMEM((1, LANES), F32)],
        compiler_params=_cparams(("arbitrary",)),
        name="mixer_out",
    )(y_ssd, o_att, gates_src, gates_src, x, w_o_ssd, w_o_att, w_out, norm_ffn.reshape(1, d), rw, rb, counts0)


MOE_ROWS = 256
ROUTE_TOK = 256


def _dispatch_kernel(ps_ref, pl_ref, nu_ref, dest_ref, hp_ref, hs_ref, xs_ref, zero_ref, sem,
                     *, prompt_tiles, n_tiles, n_blocks):
    i = pl.program_id(0)

    def zero_copies(action):
        def per_expert(e, c):
            def row(r, c2):
                action(pltpu.make_async_copy(zero_ref.at[pl.ds(0, 1)], xs_ref.at[pl.ds(ps_ref[e] + r, 1)],
                                             sem.at[1]))
                return c2
            return lax.fori_loop(0, pl_ref[e], row, c)

        lax.fori_loop(0, N_EXPERTS, per_expert, 0)

        def tail(b, c):
            row0 = pl.multiple_of(b * MOE_ROWS, MOE_ROWS)
            action(pltpu.make_async_copy(zero_ref, xs_ref.at[pl.ds(row0, MOE_ROWS)], sem.at[1]))
            return c

        lax.fori_loop(nu_ref[0], n_blocks, tail, 0)

    @pl.when(i == 0)
    def _():
        zero_ref[...] = jnp.zeros(zero_ref.shape, F32)
        zero_copies(lambda cp: cp.start())

    def scatter(src_ref):
        def body(t, c):
            for k in range(TOP_K):
                d = dest_ref[0, 0, t * TOP_K + k]
                pltpu.make_async_copy(src_ref.at[pl.ds(t, 1)], xs_ref.at[pl.ds(d, 1)], sem.at[0]).start(
                    priority=k % 2)
            return c
        lax.fori_loop(0, ROUTE_TOK, body, 0, unroll=2)
        for k in range(TOP_K):
            pltpu.make_async_copy(src_ref, xs_ref.at[pl.ds(0, ROUTE_TOK)], sem.at[0]).wait()

    @pl.when(i < prompt_tiles)
    def _():
        scatter(hp_ref)

    @pl.when(i >= prompt_tiles)
    def _():
        scatter(hs_ref)

    @pl.when(i == n_tiles - 1)
    def _():
        zero_copies(lambda cp: cp.wait())


def moe_dispatch(dest, hn_p, hn_s, pad_start, pad_len, n_used, n_blocks):
    tp, d = hn_p.shape
    tsn = hn_s.shape[0]
    assert tp % ROUTE_TOK == 0 and tsn % ROUTE_TOK == 0
    prompt_tiles = tp // ROUTE_TOK
    n_tiles = prompt_tiles + tsn // ROUTE_TOK
    kern = functools.partial(_dispatch_kernel, prompt_tiles=prompt_tiles, n_tiles=n_tiles, n_blocks=n_blocks)
    return pl.pallas_call(
        kern,
        grid_spec=pltpu.PrefetchScalarGridSpec(
            num_scalar_prefetch=3,
            grid=(n_tiles,),
            in_specs=[
                pl.BlockSpec((1, 1, ROUTE_TOK * TOP_K), lambda i, a, b, c: (i, 0, 0), memory_space=pltpu.SMEM),
                pl.BlockSpec((ROUTE_TOK, d), lambda i, a, b, c: (jnp.minimum(i, prompt_tiles - 1), 0)),
                pl.BlockSpec((ROUTE_TOK, d), lambda i, a, b, c: (jnp.maximum(i - prompt_tiles, 0), 0)),
            ],
            out_specs=pl.BlockSpec(memory_space=pl.ANY),
            scratch_shapes=[pltpu.VMEM((MOE_ROWS, d), F32), pltpu.SemaphoreType.DMA((2,))],
        ),
        out_shape=jax.ShapeDtypeStruct((n_blocks * MOE_ROWS, d), F32),
        compiler_params=_cparams(("arbitrary",)),
        name="moe_dispatch",
    )(pad_start, pad_len, n_used, dest.reshape(n_tiles, 1, ROUTE_TOK * TOP_K), hn_p, hn_s)


def _moe_kernel(be_ref, nu_ref, x_ref, wgu_ref, bgu_ref, wd_ref, bd_ref, o_ref, wgu16_ref, wd16_ref):
    b = pl.program_id(0)
    used = b < nu_ref[0]
    new_expert = jnp.logical_or(b == 0, be_ref[b] != be_ref[jnp.maximum(b - 1, 0)])

    @pl.when(jnp.logical_and(used, new_expert))
    def _():
        wgu16_ref[...] = wgu_ref[0].astype(BF16)
        wd16_ref[...] = wd_ref[0].astype(BF16)

    @pl.when(used)
    def _():
        x = x_ref[...].astype(BF16)
        gu = _dot(x, wgu16_ref[...]) + bgu_ref[0]
        d_ff = gu.shape[1] // 2
        gate = jnp.minimum(gu[:, :d_ff], SWIGLU_LIMIT)
        up = jnp.clip(gu[:, d_ff:], -SWIGLU_LIMIT, SWIGLU_LIMIT)
        act = (up + 1.0) * gate * jax.nn.sigmoid(SWIGLU_ALPHA * gate)
        o_ref[...] = _dot(act.astype(BF16), wd16_ref[...]) + bd_ref[0]

    @pl.when(jnp.logical_not(used))
    def _():
        o_ref[...] = jnp.zeros(o_ref.shape, F32)


def moe_experts(xs, block_e, n_used, w_gu, b_gu, w_down, b_down):
    rows, d = xs.shape
    nb = rows // MOE_ROWS
    e, _, gu_w = w_gu.shape
    last = lambda b, nu: jnp.minimum(b, nu[0] - 1)
    return pl.pallas_call(
        _moe_kernel,
        grid_spec=pltpu.PrefetchScalarGridSpec(
            num_scalar_prefetch=2,
            grid=(nb,),
            in_specs=[
                pl.BlockSpec((MOE_ROWS, d), lambda b, be, nu: (last(b, nu), 0)),
                pl.BlockSpec((1, d, gu_w), lambda b, be, nu: (be[last(b, nu)], 0, 0)),
                pl.BlockSpec((1, 1, gu_w), lambda b, be, nu: (be[last(b, nu)], 0, 0)),
                pl.BlockSpec((1, gu_w // 2, d), lambda b, be, nu: (be[last(b, nu)], 0, 0)),
                pl.BlockSpec((1, 1, d), lambda b, be, nu: (be[last(b, nu)], 0, 0)),
            ],
            out_specs=pl.BlockSpec((MOE_ROWS, d), lambda b, be, nu: (b, 0)),
            scratch_shapes=[pltpu.VMEM((d, gu_w), BF16), pltpu.VMEM((gu_w // 2, d), BF16)],
        ),
        out_shape=jax.ShapeDtypeStruct((rows, d), F32),
        compiler_params=_cparams(("arbitrary",)),
        name="moe_experts",
    )(block_e, n_used, xs, w_gu, b_gu.reshape(e, 1, gu_w), w_down, b_down.reshape(e, 1, d))


def _combine_kernel(dfirst_ref, dnext_ref, x2_ref, gate_ref, g_ref, ys_ref, o_ref, buf_ref, sem, *, n_tiles):
    i = pl.program_id(0)
    slot = i % 2

    def issue(dref, sl):
        def body(t, c):
            for k in range(TOP_K):
                d = dref[0, 0, t * TOP_K + k]
                pltpu.make_async_copy(ys_ref.at[pl.ds(d, 1)], buf_ref.at[sl, k, pl.ds(t, 1)], sem.at[sl]).start(
                    priority=k % 2)
            return c
        lax.fori_loop(0, ROUTE_TOK, body, 0, unroll=2)

    @pl.when(i == 0)
    def _():
        issue(dfirst_ref, 0)

    @pl.when(i + 1 < n_tiles)
    def _():
        issue(dnext_ref, 1 - slot)

    for k in range(TOP_K):
        pltpu.make_async_copy(ys_ref.at[pl.ds(0, ROUTE_TOK)], buf_ref.at[slot, k], sem.at[slot]).wait()

    gates = gate_ref[...]
    moe = buf_ref[slot, 0] * gates[:, 0:1]
    for k in range(1, TOP_K):
        moe = moe + buf_ref[slot, k] * gates[:, k:k + 1]
    tok = x2_ref[...] + moe
    y = tok * lax.rsqrt(jnp.mean(tok * tok, axis=-1, keepdims=True) + EPS)
    o_ref[...] = y * g_ref[...]


def combine(x2, ys, dest, gates, norm_final):
    t, d = x2.shape
    assert t % ROUTE_TOK == 0
    n_tiles = t // ROUTE_TOK
    dest3 = dest.reshape(n_tiles, 1, ROUTE_TOK * TOP_K)
    kern = functools.partial(_combine_kernel, n_tiles=n_tiles)
    smem = lambda imap: pl.BlockSpec((1, 1, ROUTE_TOK * TOP_K), imap, memory_space=pltpu.SMEM)
    return pl.pallas_call(
        kern,
        grid=(n_tiles,),
        in_specs=[
            smem(lambda i: (0, 0, 0)),
            smem(lambda i: (jnp.minimum(i + 1, n_tiles - 1), 0, 0)),
            pl.BlockSpec((ROUTE_TOK, d), lambda i: (i, 0)),
            pl.BlockSpec((ROUTE_TOK, ROUTE_W), lambda i: (i, 0)),
            pl.BlockSpec((1, d), lambda i: (0, 0)),
            pl.BlockSpec(memory_space=pl.ANY),
        ],
        out_specs=pl.BlockSpec((ROUTE_TOK, d), lambda i: (i, 0)),
        out_shape=jax.ShapeDtypeStruct((t, d), F32),
        scratch_shapes=[pltpu.VMEM((2, TOP_K, ROUTE_TOK, d), F32), pltpu.SemaphoreType.DMA((2,))],
        compiler_params=_cparams(("arbitrary",)),
        name="combine",
    )(dest3, dest3, x2, gates, norm_final.reshape(1, d), ys)


def kernel(x_prompt, x_sample, cache_k, cache_v, state_ssm, state_conv, norm_mix, w_in, conv_w, conv_b, dt_bias, a_log, d_skip, ssd_norm, w_o_ssd, lambda_q1, lambda_k1, lambda_q2, lambda_k2, subln, w_o_att, w_out, norm_ffn, router_w, router_b, w_gu, b_gu, w_down, b_down, norm_final):
    bp, s, d = x_prompt.shape
    bs, ts, _ = x_sample.shape
    past = cache_k.shape[2]
    depth = w_in.shape[0]
    assert depth == 1
    layer = 0
    lambda_init = 0.8 - 0.6 * math.exp(-0.3 * layer)
    tp, tsn = bp * s, bs * ts
    t_all = tp + tsn

    heads = ATT_N_HEADS
    xp2 = x_prompt.reshape(tp, d)
    xs2 = x_sample.reshape(tsn, d)

    sizes = (SSD_D_INNER, SSD_CONV_CH, SSD_N_HEADS, d, d, d, d, d)
    offs = [0]
    for sz in sizes:
        offs.append(offs[-1] + sz)
    w = w_in[layer]
    seg = lambda i: w[:, offs[i]:offs[i + 1]]
    dt_pad = jnp.zeros((d, PROJ_XD_W - SSD_CONV_CH - SSD_N_HEADS), F32)
    w_all = jnp.concatenate([seg(4), seg(5), seg(0), seg(3), seg(6), seg(7), seg(1), seg(2), dt_pad],
                            axis=1).astype(BF16)
    k_p, v_p, kv16_p, zq_p, xd_p = in_proj(xp2, norm_mix[layer], w_all, tm=1024)
    k_s, v_s, kv16_s, zq_s, xd_s = in_proj(xs2, norm_mix[layer], w_all, tm=tsn)

    prev_p = jnp.zeros((bp, SUBLANES, SSD_CONV_CH), F32)
    h0_p = jnp.zeros((bp, SSD_D_STATE, SSD_D_INNER), F32)
    ssd_w = (conv_w[layer], conv_b[layer], dt_bias[layer], a_log[layer], d_skip[layer], ssd_norm[layer])
    y_p, hT_p = ssd_mixer(xd_p, zq_p, 0, prev_p, h0_p, *ssd_w, b=bp, l=s, tl=256, valid_len=None, out_rows=tp)

    pad_rows = CHUNK - ts
    pad_seq = lambda a: jnp.pad(a.reshape(bs, ts, -1), ((0, 0), (0, pad_rows), (0, 0))).reshape(bs * CHUNK, -1)
    prev_s = jnp.pad(state_conv[layer], ((0, 0), (SUBLANES - (SSD_CONV_WIDTH - 1), 0), (0, 0)))
    h0_s = jnp.swapaxes(state_ssm[layer].reshape(bs, SSD_D_INNER, SSD_D_STATE), 1, 2)
    y_s, hT_s = ssd_mixer(pad_seq(xd_s), pad_seq(zq_s[:, :SSD_D_INNER]), 0, prev_s, h0_s, *ssd_w, b=bs, l=CHUNK,
                          tl=CHUNK, valid_len=ts, out_rows=bs * CHUNK)
    y_s = y_s.reshape(bs, CHUNK, -1)[:, :ts].reshape(tsn, -1)

    lam_rows = jnp.concatenate([lambda_q1[layer][None], lambda_k1[layer][None], lambda_q2[layer][None],
                                lambda_k2[layer][None], jnp.zeros((4, ATT_HEAD_DIM), F32)], axis=0)
    o_p = diff_attention_t(zq_p, 0, SSD_D_INNER // d, kv16_p, 0, kv16_p, 1, lam_rows, subln[layer], b=bp, lq=s,
                           lk=s, tq=512, tk=1024, q_off=0, kv_len=s, lambda_init=lambda_init, out_rows=tp)
    kv_new = kv16_s.reshape(bs, ts, -1)
    kv_len = past + ts
    kv_pad = (-kv_len) % LANES
    lk_s = kv_len + kv_pad
    k_all = jnp.concatenate([cache_k[layer].reshape(bs, past, -1).astype(BF16), kv_new[:, :, :d],
                             jnp.zeros((bs, kv_pad, d), BF16)], axis=1).reshape(bs * lk_s, d)
    v_all = jnp.concatenate([cache_v[layer].reshape(bs, past, -1).astype(BF16), kv_new[:, :, d:],
                             jnp.zeros((bs, kv_pad, d), BF16)], axis=1).reshape(bs * lk_s, d)
    q_s = jnp.pad(zq_s[:, SSD_D_INNER:SSD_D_INNER + d].reshape(bs, ts, d), ((0, 0), (0, LANES - ts), (0, 0)))
    o_s = diff_attention_t(q_s.reshape(bs * LANES, d), 0, 0, k_all, 0, v_all, 0, lam_rows, subln[layer], b=bs,
                           lq=LANES, lk=lk_s, tq=LANES, tk=lk_s, q_off=past, kv_len=kv_len,
                           lambda_init=lambda_init, out_rows=bs * LANES)
    o_s = o_s.reshape(bs, LANES, d)[:, :ts].reshape(tsn, d)

    rw = jnp.zeros((d, LANES), F32).at[:, :N_EXPERTS].set(router_w[layer])
    rb = jnp.zeros((1, LANES), F32).at[0, :N_EXPERTS].set(router_b[layer])
    mix_w = (w_o_ssd[layer].astype(BF16), w_o_att[layer].astype(BF16), w_out[layer].astype(BF16), norm_ffn[layer],
             rw, rb)
    x2_p, hn_p, gate_p, sel_p, cnt_p = mixer_out(y_p, o_p, zq_p, 3, 4, xp2, *mix_w,
                                                 jnp.zeros((1, LANES), F32), tm=512)
    x2_s, hn_s, gate_s, sel_s, cnt_all = mixer_out(y_s, o_s, zq_s, 3, 4, xs2, *mix_w, cnt_p, tm=tsn)

    sel = jnp.concatenate([sel_p, sel_s], axis=0)
    top_e = sel[:, :TOP_K]
    rank = sel[:, TOP_K:]
    cnt = cnt_all[0, :N_EXPERTS].astype(jnp.int32)
    padded = (cnt + MOE_ROWS - 1) // MOE_ROWS * MOE_ROWS
    ends = jnp.cumsum(padded)
    starts = ends - padded
    dest = (starts[top_e] + rank).reshape(-1)
    nb = (t_all * TOP_K + N_EXPERTS * (MOE_ROWS - 1) + MOE_ROWS - 1) // MOE_ROWS
    block_start = jnp.arange(nb, dtype=jnp.int32) * MOE_ROWS
    block_e = jnp.minimum(jnp.sum((ends[None, :] <= block_start[:, None]).astype(jnp.int32), axis=1),
                          N_EXPERTS - 1)
    n_used = (ends[-1] // MOE_ROWS).astype(jnp.int32).reshape(1)

    xs_sorted = moe_dispatch(dest, hn_p, hn_s, starts + cnt, padded - cnt, n_used, nb)
    ys = moe_experts(xs_sorted, block_e, n_used, w_gu[layer], b_gu[layer], w_down[layer], b_down[layer])
    y_prompt = combine(x2_p, ys, dest[:tp * TOP_K], gate_p, norm_final).reshape(bp, s, d)
    y_sample = combine(x2_s, ys, dest[tp * TOP_K:], gate_s, norm_final).reshape(bs, ts, d)

    new_k_p = k_p.reshape(1, bp, s, heads, 2 * ATT_HEAD_DIM)
    new_v_p = v_p.reshape(1, bp, s, heads, ATT_V_HEAD)
    new_k_s = k_s.reshape(1, bs, ts, heads, 2 * ATT_HEAD_DIM)
    new_v_s = v_s.reshape(1, bs, ts, heads, ATT_V_HEAD)
    ssm_p = jnp.swapaxes(hT_p, 1, 2).reshape(1, bp, SSD_N_HEADS, SSD_HEAD_DIM, SSD_D_STATE)
    ssm_s = jnp.swapaxes(hT_s, 1, 2).reshape(1, bs, SSD_N_HEADS, SSD_HEAD_DIM, SSD_D_STATE)
    keep = SSD_CONV_WIDTH - 1
    conv_p = xd_p.reshape(bp, s, -1)[:, s - keep:, :SSD_CONV_CH][None]
    raw_s = jnp.concatenate([state_conv[layer], xd_s.reshape(bs, ts, -1)[:, :, :SSD_CONV_CH]], axis=1)
    conv_s = raw_s[:, -keep:][None]
    return (y_prompt, y_sample, new_k_p, new_v_p, ssm_p, conv_p, new_k_s, new_v_s, ssm_s, conv_s)
```

```python
import functools
import math

import jax
import jax.numpy as jnp
from jax import lax
from jax.experimental import pallas as pl
from jax.experimental.pallas import tpu as pltpu

EPS = 1e-5
CHUNK = 64
D_MODEL = 1024
SSD_D_INNER = 2048
SSD_HEAD_DIM = 64
SSD_N_HEADS = 32
SSD_GROUPS = 4
SSD_D_STATE = 128
SSD_CONV_WIDTH = 4
SSD_CONV_CH = 3072
ATT_HEAD_DIM = 64
ATT_N_HEADS = 8
ATT_V_HEAD = 128
ALIBI_MAX_BIAS = 8.0
N_EXPERTS = 32
TOP_K = 4
SWIGLU_LIMIT = 7.0
SWIGLU_ALPHA = 1.702

LANES = 128
SUBLANES = 8
VMEM_LIMIT = 56 * 1024 * 1024
NEG_BIG = -1e30

BF16 = jnp.bfloat16
F32 = jnp.float32


def _cparams(sem, flags=None):
    return pltpu.CompilerParams(dimension_semantics=sem, vmem_limit_bytes=VMEM_LIMIT, flags=flags)


def _split3(x):
    h1 = x.astype(BF16)
    r1 = x - h1.astype(F32)
    h2 = r1.astype(BF16)
    h3 = (r1 - h2.astype(F32)).astype(BF16)
    return h1, h2, h3


def _dot(a, b):
    return jnp.dot(a, b, preferred_element_type=F32)


def _dot_exact_rhs(x, m):
    h1, h2, h3 = _split3(x)
    return _dot(h1, m) + _dot(h2, m) + _dot(h3, m)


def _dot_exact_lhs(m, x):
    h1, h2, h3 = _split3(x)
    return _dot(m, h1) + _dot(m, h2) + _dot(m, h3)


PROJ_TN = 512
PROJ_K = (0, D_MODEL // PROJ_TN)
PROJ_V = (PROJ_K[1], PROJ_K[1] + D_MODEL // PROJ_TN)
PROJ_ZQ_W = SSD_D_INNER + 3 * D_MODEL
PROJ_ZQ = (PROJ_V[1], PROJ_V[1] + PROJ_ZQ_W // PROJ_TN)
PROJ_XD_W = -(-(SSD_CONV_CH + SSD_N_HEADS) // PROJ_TN) * PROJ_TN
PROJ_XD = (PROJ_ZQ[1], PROJ_ZQ[1] + PROJ_XD_W // PROJ_TN)


def _in_proj_kernel(x_ref, g_ref, w_ref, k_ref, v_ref, kv16_ref, zq_ref, xd_ref, xn_ref):
    j = pl.program_id(1)

    @pl.when(j == 0)
    def _():
        x = x_ref[...]
        y = x * lax.rsqrt(jnp.mean(x * x, axis=-1, keepdims=True) + EPS)
        xn_ref[...] = (y * g_ref[...]).astype(BF16)

    @pl.when(j < PROJ_K[1])
    def _():
        acc = _dot(xn_ref[...], w_ref[...])
        k_ref[...] = acc
        kv16_ref[...] = acc.astype(BF16)

    @pl.when(jnp.logical_and(j >= PROJ_V[0], j < PROJ_V[1]))
    def _():
        acc = _dot(xn_ref[...], w_ref[...])
        v_ref[...] = acc
        kv16_ref[...] = acc.astype(BF16)

    @pl.when(jnp.logical_and(j >= PROJ_ZQ[0], j < PROJ_ZQ[1]))
    def _():
        zq_ref[...] = _dot(xn_ref[...], w_ref[...]).astype(BF16)

    @pl.when(j >= PROJ_XD[0])
    def _():
        xd_ref[...] = _dot(xn_ref[...], w_ref[...])


def in_proj(x, gain, w, tm):
    t, d = x.shape
    assert t % tm == 0 and w.shape[1] == PROJ_XD[1] * PROJ_TN
    tn = PROJ_TN

    def out_map(rng):
        return lambda i, j: (i, jnp.clip(j - rng[0], 0, rng[1] - rng[0] - 1))

    return pl.pallas_call(
        _in_proj_kernel,
        grid=(t // tm, PROJ_XD[1]),
        in_specs=[
            pl.BlockSpec((tm, d), lambda i, j: (i, 0)),
            pl.BlockSpec((1, d), lambda i, j: (0, 0)),
            pl.BlockSpec((d, tn), lambda i, j: (0, j)),
        ],
        out_specs=[
            pl.BlockSpec((tm, tn), out_map(PROJ_K)),
            pl.BlockSpec((tm, tn), out_map(PROJ_V)),
            pl.BlockSpec((tm, tn), out_map((PROJ_K[0], PROJ_V[1]))),
            pl.BlockSpec((tm, tn), out_map(PROJ_ZQ)),
            pl.BlockSpec((tm, tn), out_map(PROJ_XD)),
        ],
        out_shape=[
            jax.ShapeDtypeStruct((t, d), F32),
            jax.ShapeDtypeStruct((t, d), F32),
            jax.ShapeDtypeStruct((t, 2 * d), BF16),
            jax.ShapeDtypeStruct((t, PROJ_ZQ_W), BF16),
            jax.ShapeDtypeStruct((t, PROJ_XD_W), F32),
        ],
        scratch_shapes=[pltpu.VMEM((tm, d), BF16)],
        compiler_params=_cparams(("arbitrary", "arbitrary")),
        name="in_proj",
    )(x, gain.reshape(1, d), w)


CONV_ROWS, CONV_COLS = 64, 512


def _ssd_kernel(xbc_ref, z_ref, prev_ref, h0_ref, cw_ref, cb_ref, dtb_ref, alog_ref, dsk_ref, gn_ref,
                y_ref, hout_ref, xpad_ref, xc_ref, tail_ref, st_ref, *, tl, valid_len):
    t = pl.program_id(1)
    nt = pl.num_programs(1)
    nch = tl // CHUNK
    dinner = SSD_D_INNER
    gw = dinner // SSD_GROUPS
    ns = SSD_D_STATE

    @pl.when(t == 0)
    def _():
        tail_ref[...] = prev_ref[0]
        st_ref[...] = h0_ref[0]

    xpad_ref[0:SUBLANES, :] = tail_ref[...]
    xpad_ref[SUBLANES:SUBLANES + tl, :] = xbc_ref[:, 0:SSD_CONV_CH]
    tail_ref[...] = xbc_ref[tl - SUBLANES:tl, 0:SSD_CONV_CH]
    for r in range(0, tl, CONV_ROWS):
        for c in range(0, SSD_CONV_CH, CONV_COLS):
            acc = cb_ref[:, c:c + CONV_COLS]
            for k in range(SSD_CONV_WIDTH):
                off = r + SUBLANES - (SSD_CONV_WIDTH - 1) + k
                acc = acc + xpad_ref[off:off + CONV_ROWS, c:c + CONV_COLS] * cw_ref[k:k + 1, c:c + CONV_COLS]
            xc_ref[r:r + CONV_ROWS, c:c + CONV_COLS] = acc * jax.nn.sigmoid(acc)

    head_of_lane = lax.broadcasted_iota(jnp.int32, (SSD_N_HEADS, dinner), 1) // SSD_HEAD_DIM
    expand = (head_of_lane == lax.broadcasted_iota(jnp.int32, (SSD_N_HEADS, dinner), 0)).astype(BF16)
    ti = lax.broadcasted_iota(jnp.int32, (CHUNK, CHUNK), 0)
    si = lax.broadcasted_iota(jnp.int32, (CHUNK, CHUNK), 1)
    tril = (si <= ti).astype(BF16)
    row_c = lax.broadcasted_iota(jnp.int32, (CHUNK, dinner), 0)
    pos_in_head = lax.broadcasted_iota(jnp.int32, (CHUNK, dinner), 1) % CHUNK
    upper = row_c <= pos_in_head
    row_p = lax.broadcasted_iota(jnp.int32, (CHUNK, LANES), 0)
    lane_p = lax.broadcasted_iota(jnp.int32, (CHUNK, LANES), 1)
    causal_pair = (lane_p % CHUNK) <= row_p
    left_half = lane_p < SSD_HEAD_DIM
    a_neg_e = _dot_exact_rhs(-jnp.exp(alog_ref[...]), expand)
    dsk_e = _dot_exact_rhs(dsk_ref[...], expand)

    def chunk_body(c, carry):
        r0 = pl.multiple_of(c * CHUNK, CHUNK)
        xs = xc_ref[pl.ds(r0, CHUNK), 0:dinner]
        dt_raw = xbc_ref[pl.ds(r0, CHUNK), SSD_CONV_CH:SSD_CONV_CH + SSD_N_HEADS]
        dtv = dt_raw + dtb_ref[...]
        dt = jnp.maximum(dtv, 0.0) + jnp.log1p(jnp.exp(-jnp.abs(dtv)))
        if valid_len is not None:
            rows = t * tl + r0 + lax.broadcasted_iota(jnp.int32, (CHUNK, SSD_N_HEADS), 0)
            dt = jnp.where(rows < valid_len, dt, 0.0)
        dt_e = _dot_exact_rhs(dt, expand)
        a_e = dt_e * a_neg_e
        acs_e = _dot_exact_lhs(tril, a_e)
        rowterm = jnp.sum(jnp.where(upper, a_e, 0.0), axis=0, keepdims=True)
        acs_last = acs_e[CHUNK - 1:CHUNK, :]
        xdt = xs * dt_e
        x_dec = (xdt * jnp.exp(acs_last - acs_e)).astype(BF16)
        e_acs = jnp.exp(acs_e)
        e_last = jnp.exp(acs_last)

        y_parts = []
        for g in range(SSD_GROUPS):
            lo = g * gw
            bm = xc_ref[pl.ds(r0, CHUNK), dinner + g * ns:dinner + (g + 1) * ns].astype(BF16)
            cm = xc_ref[pl.ds(r0, CHUNK), dinner + SSD_GROUPS * ns + g * ns:
                        dinner + SSD_GROUPS * ns + (g + 1) * ns].astype(BF16)
            b2 = jnp.concatenate([bm, bm], axis=0)
            cb2 = lax.dot_general(cm, b2, (((1,), (1,)), ((), ())), preferred_element_type=F32)
            st_g = st_ref[:, lo:lo + gw]
            y_off = _dot(cm, st_g.astype(BF16)) * e_acs[:, lo:lo + gw]
            pieces = []
            for j in range(gw // LANES):
                l0 = lo + j * LANES
                diff = acs_e[:, l0:l0 + LANES] - rowterm[:, l0:l0 + LANES]
                dec = jnp.where(causal_pair, jnp.exp(diff), 0.0)
                scores = (cb2 * dec).astype(BF16)
                xp = xdt[:, l0:l0 + LANES]
                xblk = jnp.concatenate([jnp.where(left_half, xp, 0.0), jnp.where(left_half, 0.0, xp)],
                                       axis=0).astype(BF16)
                pieces.append(_dot(scores, xblk))
            y_diag = jnp.concatenate(pieces, axis=1)
            upd = lax.dot_general(bm, x_dec[:, lo:lo + gw], (((0,), (0,)), ((), ())),
                                  preferred_element_type=F32)
            st_ref[:, lo:lo + gw] = e_last[:, lo:lo + gw] * st_g + upd
            yg = y_diag + y_off + xs[:, lo:lo + gw] * dsk_e[:, lo:lo + gw]
            zg = z_ref[pl.ds(r0, CHUNK), lo:lo + gw].astype(F32)
            yg = yg * (zg * jax.nn.sigmoid(zg))
            yn = yg * lax.rsqrt(jnp.mean(yg * yg, axis=-1, keepdims=True) + EPS)
            y_parts.append((yn * gn_ref[:, lo:lo + gw]).astype(y_ref.dtype))
        y_ref[pl.ds(r0, CHUNK), :] = jnp.concatenate(y_parts, axis=1)
        return carry

    lax.fori_loop(0, nch, chunk_body, 0)

    @pl.when(t == nt - 1)
    def _():
        hout_ref[0] = st_ref[...]


def ssd_mixer(xbc, z_src, z_col, conv_prev8, h0_t, conv_w, conv_b, dt_bias, a_log, d_skip, ssd_norm,
              b, l, tl, valid_len, out_rows):
    wx = xbc.shape[1]
    assert l % tl == 0 and tl % CHUNK == 0 and CHUNK == SSD_HEAD_DIM
    nt = l // tl
    kern = functools.partial(_ssd_kernel, tl=tl, valid_len=valid_len)
    full = lambda shape: pl.BlockSpec(shape, lambda i, j: (0,) * len(shape))
    return pl.pallas_call(
        kern,
        grid=(b, nt),
        in_specs=[
            pl.BlockSpec((tl, wx), lambda i, j: (i * nt + j, 0)),
            pl.BlockSpec((tl, SSD_D_INNER), lambda i, j: (i * nt + j, z_col)),
            pl.BlockSpec((1, SUBLANES, SSD_CONV_CH), lambda i, j: (i, 0, 0)),
            pl.BlockSpec((1, SSD_D_STATE, SSD_D_INNER), lambda i, j: (i, 0, 0)),
            full((SSD_CONV_WIDTH, SSD_CONV_CH)),
            full((1, SSD_CONV_CH)),
            full((1, SSD_N_HEADS)),
            full((1, SSD_N_HEADS)),
            full((1, SSD_N_HEADS)),
            full((1, SSD_D_INNER)),
        ],
        out_specs=[
            pl.BlockSpec((tl, SSD_D_INNER), lambda i, j: (i * nt + j, 0)),
            pl.BlockSpec((1, SSD_D_STATE, SSD_D_INNER), lambda i, j: (i, 0, 0)),
        ],
        out_shape=[
            jax.ShapeDtypeStruct((out_rows, SSD_D_INNER), BF16),
            jax.ShapeDtypeStruct((b, SSD_D_STATE, SSD_D_INNER), F32),
        ],
        scratch_shapes=[
            pltpu.VMEM((tl + SUBLANES, SSD_CONV_CH), F32),
            pltpu.VMEM((tl, SSD_CONV_CH), F32),
            pltpu.VMEM((SUBLANES, SSD_CONV_CH), F32),
            pltpu.VMEM((SSD_D_STATE, SSD_D_INNER), F32),
        ],
        compiler_params=_cparams(("arbitrary", "arbitrary")),
        name="ssd_mixer",
    )(xbc, z_src, conv_prev8, h0_t, conv_w, conv_b.reshape(1, -1), dt_bias.reshape(1, -1),
      a_log.reshape(1, -1), d_skip.reshape(1, -1), ssd_norm.reshape(1, -1))


BIAS_SPLIT = 32
FLAG_FIRST, FLAG_LAST, FLAG_MASK = 1, 2, 4
ATT_SUB = 256


def _attn_t_kernel(qi_ref, ki_ref, fl_ref, q_ref, k_ref, v_ref, slope_ref, lam_ref, sub_ref, o_ref,
                   m_ref, l_ref, acc_ref, kb_ref, corr_ref, s0_ref, s1_ref, mt0_ref, mt1_ref,
                   *, tq, tk, q_off, kv_len, lambda_init):
    p_idx = pl.program_id(1)
    qi = qi_ref[p_idx]
    ki = ki_ref[p_idx]
    flags = fl_ref[p_idx]
    qstart = q_off + qi * tq
    kstart = ki * tk
    half = ATT_HEAD_DIM
    scale = ATT_HEAD_DIM ** -0.5

    @pl.when((flags & FLAG_FIRST) != 0)
    def _():
        m_ref[...] = jnp.full(m_ref.shape, NEG_BIG, F32)
        l_ref[...] = jnp.zeros(l_ref.shape, F32)
        acc_ref[...] = jnp.zeros(acc_ref.shape, F32)

    lane_k = lax.broadcasted_iota(jnp.int32, (tk, LANES), 1)
    rel = kstart - qstart + lax.broadcasted_iota(jnp.int32, (tk, LANES), 0)
    hi = (rel // BIAS_SPLIT) * BIAS_SPLIT
    lo = rel - hi
    pos_lane = lane_k % half
    kb_ref[...] = jnp.where(pos_lane == 0, hi, jnp.where(pos_lane == 1, lo, 0)).astype(F32).astype(BF16)

    needs_mask = (flags & FLAG_MASK) != 0

    @pl.when(needs_mask)
    def _():
        kpos = kstart + lax.broadcasted_iota(jnp.int32, (tk, tq), 0)
        qpos = qstart + lax.broadcasted_iota(jnp.int32, (tk, tq), 1)
        allowed = jnp.logical_and(kpos // CHUNK <= qpos // CHUNK, kpos < kv_len)
        after = jnp.where(kpos > qpos, 2.0 * (qpos - kpos).astype(F32), 0.0)
        corr_ref[0] = after
        corr_ref[1] = jnp.where(allowed, 0.0, NEG_BIG)

    lane_q = lax.broadcasted_iota(jnp.int32, (tq, LANES), 1)
    nsub = tk // ATT_SUB if tk % ATT_SUB == 0 else 1
    ts = tk // nsub
    lane_ks = lax.broadcasted_iota(jnp.int32, (ts, LANES), 1)

    def score_pass(h, s_ref, mt_ref, masked):
        c0 = pl.multiple_of(h * LANES, LANES)
        q = q_ref[:, pl.ds(c0, LANES)] * jnp.asarray(scale, BF16)
        slope = slope_ref[pl.ds(h, 1), :]
        slope_b = jnp.broadcast_to(slope.astype(BF16), (tq, LANES))
        zero_q = jnp.zeros((tq, LANES), BF16)
        for idx in range(2):
            own = (lane_q < half) if idx == 0 else (lane_q >= half)
            q_aug = jnp.where(own, q, jnp.where((lane_q % half) < 2, slope_b, zero_q))
            own_k = (lane_ks < half) if idx == 0 else (lane_ks >= half)
            mt = None
            for j in range(nsub):
                r0 = j * ts
                k_aug = jnp.where(own_k, k_ref[r0:r0 + ts, pl.ds(c0, LANES)], kb_ref[r0:r0 + ts, :])
                s = lax.dot_general(k_aug, q_aug, (((1,), (1,)), ((), ())), preferred_element_type=F32)
                if masked:
                    s = s + slope[:, 0:1] * corr_ref[0, r0:r0 + ts, :] + corr_ref[1, r0:r0 + ts, :]
                s_ref[idx, r0:r0 + ts, :] = s
                mj = jnp.max(s, axis=0, keepdims=True)
                mt = mj if mt is None else jnp.maximum(mt, mj)
                yield
            mt_ref[idx] = mt

    def value_pass(h, s_ref, mt_ref):
        c0 = pl.multiple_of(h * LANES, LANES)
        for idx in range(2):
            m_prev = m_ref[idx, h]
            m_new = jnp.maximum(m_prev, mt_ref[idx])
            alpha = jnp.exp(m_prev - m_new)
            lsum, pv = None, None
            for j in range(nsub):
                r0 = j * ts
                p = jnp.exp(s_ref[idx, r0:r0 + ts, :] - m_new)
                lj = jnp.sum(p, axis=0, keepdims=True)
                pj = lax.dot_general(v_ref[r0:r0 + ts, pl.ds(c0, LANES)], p.astype(BF16), (((0,), (0,)), ((), ())),
                                     preferred_element_type=F32)
                lsum = lj if lsum is None else lsum + lj
                pv = pj if pv is None else pv + pj
                yield
            l_ref[idx, h] = alpha * l_ref[idx, h] + lsum
            acc_ref[idx, h] = alpha * acc_ref[idx, h] + pv
            m_ref[idx, h] = m_new

    def run(*gens):
        live = list(gens)
        while live:
            for g in list(live):
                try:
                    next(g)
                except StopIteration:
                    live.remove(g)

    def all_heads(masked):
        bufs = ((s0_ref, mt0_ref), (s1_ref, mt1_ref))
        run(score_pass(0, *bufs[0], masked))

        def pair(g, c):
            h = 2 * g
            run(score_pass(h + 1, *bufs[1], masked), value_pass(h, *bufs[0]))
            run(score_pass(h + 2, *bufs[0], masked), value_pass(h + 1, *bufs[1]))
            return c

        lax.fori_loop(0, ATT_N_HEADS // 2 - 1, pair, 0)
        last = ATT_N_HEADS - 1
        run(score_pass(last, *bufs[1], masked), value_pass(last - 1, *bufs[0]))
        run(value_pass(last, *bufs[1]))

    @pl.when(needs_mask)
    def _():
        all_heads(True)

    @pl.when(jnp.logical_not(needs_mask))
    def _():
        all_heads(False)

    @pl.when((flags & FLAG_LAST) != 0)
    def _():
        lp = lam_ref[...]
        lam = (jnp.exp(jnp.sum(lp[0:1] * lp[1:2], axis=-1, keepdims=True))
               - jnp.exp(jnp.sum(lp[2:3] * lp[3:4], axis=-1, keepdims=True)) + lambda_init)

        def fin(h, c):
            c0 = pl.multiple_of(h * LANES, LANES)
            o = acc_ref[0, h] / l_ref[0, h] - lam * (acc_ref[1, h] / l_ref[1, h])
            on = o * lax.rsqrt(jnp.mean(o * o, axis=0, keepdims=True) + EPS)
            on = (on * sub_ref[...]) * (1.0 - lambda_init)
            o_ref[:, pl.ds(c0, LANES)] = on.T.astype(o_ref.dtype)
            return c

        lax.fori_loop(0, ATT_N_HEADS, fin, 0)


def _attn_pairs(nq, nk, tq, tk, q_off, kv_len):
    qis, kis, fls = [], [], []
    for qi in range(nq):
        first_q = q_off + qi * tq
        last_q = first_q + tq - 1
        kend = min((last_q // CHUNK + 1) * CHUNK, kv_len)
        nkv = -(-kend // tk)
        for ki in range(nkv):
            block_end = (ki + 1) * tk
            plain = block_end <= (first_q // CHUNK) * CHUNK and block_end <= kv_len
            fl = (FLAG_FIRST if ki == 0 else 0) | (FLAG_LAST if ki == nkv - 1 else 0) | (0 if plain else FLAG_MASK)
            qis.append(qi)
            kis.append(ki)
            fls.append(fl)
    return qis, kis, fls


def diff_attention_t(q_src, q_blk0, q_colblk, k_src, k_colblk, v_src, v_colblk, lam_rows, subln, *, b, lq, lk,
                     tq, tk, q_off, kv_len, lambda_init, out_rows):
    assert lq % tq == 0 and lk % tk == 0 and tq % LANES == 0
    assert q_off + lq <= 256 * BIAS_SPLIT + tq
    nq, nk = lq // tq, lk // tk
    width = ATT_N_HEADS * LANES
    qis, kis, fls = _attn_pairs(nq, nk, tq, tk, q_off, kv_len)
    kern = functools.partial(_attn_t_kernel, tq=tq, tk=tk, q_off=q_off, kv_len=kv_len, lambda_init=lambda_init)
    slopes = jnp.exp2(-ALIBI_MAX_BIAS * jnp.arange(1, ATT_N_HEADS + 1, dtype=F32) / ATT_N_HEADS)
    slopes = jnp.broadcast_to(slopes[:, None], (ATT_N_HEADS, LANES))
    const = lambda shape: pl.BlockSpec(shape, lambda bi, p, qt, kt, ft: (0,) * len(shape))
    return pl.pallas_call(
        kern,
        grid_spec=pltpu.PrefetchScalarGridSpec(
            num_scalar_prefetch=3,
            grid=(b, len(qis)),
            in_specs=[
                pl.BlockSpec((tq, width), lambda bi, p, qt, kt, ft: (q_blk0 + bi * nq + qt[p], q_colblk)),
                pl.BlockSpec((tk, width), lambda bi, p, qt, kt, ft: (bi * nk + kt[p], k_colblk)),
                pl.BlockSpec((tk, width), lambda bi, p, qt, kt, ft: (bi * nk + kt[p], v_colblk)),
                const((ATT_N_HEADS, LANES)),
                const((SUBLANES, ATT_HEAD_DIM)),
                const((ATT_V_HEAD, 1)),
            ],
            out_specs=pl.BlockSpec((tq, width), lambda bi, p, qt, kt, ft: (bi * nq + qt[p], 0)),
            scratch_shapes=[
                pltpu.VMEM((2, ATT_N_HEADS, 1, tq), F32),
                pltpu.VMEM((2, ATT_N_HEADS, 1, tq), F32),
                pltpu.VMEM((2, ATT_N_HEADS, ATT_V_HEAD, tq), F32),
                pltpu.VMEM((tk, LANES), BF16),
                pltpu.VMEM((2, tk, tq), F32),
                pltpu.VMEM((2, tk, tq), F32),
                pltpu.VMEM((2, tk, tq), F32),
                pltpu.VMEM((2, 1, tq), F32),
                pltpu.VMEM((2, 1, tq), F32),
            ],
        ),
        out_shape=jax.ShapeDtypeStruct((out_rows, width), BF16),
        compiler_params=_cparams(("arbitrary", "arbitrary")),
        name="diff_attention_t",
    )(jnp.asarray(qis, jnp.int32), jnp.asarray(kis, jnp.int32), jnp.asarray(fls, jnp.int32),
      q_src, k_src, v_src, slopes, lam_rows, subln.reshape(-1, 1))


ROUTE_W = 2 * TOP_K


def _mixer_out_kernel(y_ref, o_ref, gs_ref, ga_ref, x_ref, wos_ref, woa_ref, wout_ref, nf_ref, rw_ref, rb_ref,
                      run0_ref, x2_ref, h_ref, gate_ref, sel_ref, cnt_ref, run_ref, *, tm):
    i = pl.program_id(0)

    @pl.when(i == 0)
    def _():
        run_ref[...] = run0_ref[...]

    o_ssd = _dot(y_ref[...], wos_ref[...])
    o_att = _dot(o_ref[...], woa_ref[...])
    merged = (jax.nn.sigmoid(gs_ref[...].astype(F32)) * o_ssd
              + jax.nn.sigmoid(ga_ref[...].astype(F32)) * o_att)
    x2 = x_ref[...] + _dot(merged.astype(BF16), wout_ref[...])
    x2_ref[...] = x2
    hn = x2 * lax.rsqrt(jnp.mean(x2 * x2, axis=-1, keepdims=True) + EPS) * nf_ref[...]
    h_ref[...] = hn

    a1, a2, a3 = _split3(hn)
    w1, w2, w3 = _split3(rw_ref[...])
    logits = (_dot(a1, w1) + _dot(a1, w2) + _dot(a2, w1) + _dot(a2, w2) + _dot(a1, w3) + _dot(a3, w1)
              + rb_ref[...])
    lane = lax.broadcasted_iota(jnp.int32, (tm, LANES), 1)
    work = jnp.where(lane < N_EXPERTS, logits, -jnp.inf)
    tops, idxs = [], []
    for _ in range(TOP_K):
        mx = jnp.max(work, axis=-1, keepdims=True)
        ix = jnp.min(jnp.where(work == mx, lane, LANES), axis=-1, keepdims=True)
        tops.append(mx)
        idxs.append(ix)
        work = jnp.where(lane == ix, -jnp.inf, work)
    es = [jnp.exp(tv - tops[0]) for tv in tops]
    den = es[0] + es[1] + es[2] + es[3]
    gates = jnp.zeros((tm, LANES), F32)
    for k in range(TOP_K):
        gates = jnp.where(lane == k, es[k] / den, gates)
    gate_ref[...] = gates[:, :ROUTE_W]

    chosen = jnp.zeros((tm, LANES), jnp.bool_)
    for k in range(TOP_K):
        chosen = jnp.logical_or(chosen, lane == idxs[k])
    multihot = jnp.where(chosen, 1.0, 0.0).astype(BF16)
    ri = lax.broadcasted_iota(jnp.int32, (tm, tm), 0)
    ci = lax.broadcasted_iota(jnp.int32, (tm, tm), 1)
    strict = jnp.where(ci < ri, 1.0, 0.0).astype(BF16)
    prefix = _dot(strict, multihot) + run_ref[...]
    sel = jnp.zeros((tm, LANES), jnp.int32)
    for k in range(TOP_K):
        rank = jnp.sum(jnp.where(lane == idxs[k], prefix, 0.0), axis=-1, keepdims=True)
        sel = jnp.where(lane == k, idxs[k], sel)
        sel = jnp.where(lane == TOP_K + k, rank.astype(jnp.int32), sel)
    sel_ref[...] = sel[:, :ROUTE_W]
    run_ref[...] = run_ref[...] + jnp.sum(multihot.astype(F32), axis=0, keepdims=True)
    cnt_ref[...] = run_ref[...]


def mixer_out(y_ssd, o_att, gates_src, gs_col, ga_col, x, w_o_ssd, w_o_att, w_out, norm_ffn, rw, rb, counts0, tm):
    t, d = x.shape
    assert t % tm == 0
    kern = functools.partial(_mixer_out_kernel, tm=tm)
    const = lambda shape: pl.BlockSpec(shape, lambda i: (0, 0))
    return pl.pallas_call(
        kern,
        grid=(t // tm,),
        in_specs=[
            pl.BlockSpec((tm, SSD_D_INNER), lambda i: (i, 0)),
            pl.BlockSpec((tm, d), lambda i: (i, 0)),
            pl.BlockSpec((tm, d), lambda i: (i, gs_col)),
            pl.BlockSpec((tm, d), lambda i: (i, ga_col)),
            pl.BlockSpec((tm, d), lambda i: (i, 0)),
            const((SSD_D_INNER, d)),
            const((d, d)),
            const((d, d)),
            const((1, d)),
            const((d, LANES)),
            const((1, LANES)),
            const((1, LANES)),
        ],
        out_specs=[
            pl.BlockSpec((tm, d), lambda i: (i, 0)),
            pl.BlockSpec((tm, d), lambda i: (i, 0)),
            pl.BlockSpec((tm, ROUTE_W), lambda i: (i, 0)),
            pl.BlockSpec((tm, ROUTE_W), lambda i: (i, 0)),
            pl.BlockSpec((1, LANES), lambda i: (0, 0)),
        ],
        out_shape=[
            jax.ShapeDtypeStruct((t, d), F32),
            jax.ShapeDtypeStruct((t, d), F32),
            jax.ShapeDtypeStruct((t, ROUTE_W), F32),
            jax.ShapeDtypeStruct((t, ROUTE_W), jnp.int32),
            jax.ShapeDtypeStruct((1, LANES), F32),
        ],
        scratch_shapes=[pltpu.VMEM((1, LANES), F32)],
        compiler_params=_cparams(("arbitrary",)),
        name="mixer_out",
    )(y_ssd, o_att, gates_src, gates_src, x, w_o_ssd, w_o_att, w_out, norm_ffn.reshape(1, d), rw, rb, counts0)


MOE_ROWS = 256
ROUTE_TOK = 256


def _dispatch_kernel(ps_ref, pl_ref, nu_ref, dest_ref, hp_ref, hs_ref, xs_ref, zero_ref, sem,
                     *, prompt_tiles, n_tiles, n_blocks):
    i = pl.program_id(0)

    def zero_copies(action):
        def per_expert(e, c):
            def row(r, c2):
                action(pltpu.make_async_copy(zero_ref.at[pl.ds(0, 1)], xs_ref.at[pl.ds(ps_ref[e] + r, 1)],
                                             sem.at[1]))
                return c2
            return lax.fori_loop(0, pl_ref[e], row, c)

        lax.fori_loop(0, N_EXPERTS, per_expert, 0)

        def tail(b, c):
            row0 = pl.multiple_of(b * MOE_ROWS, MOE_ROWS)
            action(pltpu.make_async_copy(zero_ref, xs_ref.at[pl.ds(row0, MOE_ROWS)], sem.at[1]))
            return c

        lax.fori_loop(nu_ref[0], n_blocks, tail, 0)

    @pl.when(i == 0)
    def _():
        zero_ref[...] = jnp.zeros(zero_ref.shape, F32)
        zero_copies(lambda cp: cp.start())

    def scatter(src_ref):
        def body(t, c):
            for k in range(TOP_K):
                d = dest_ref[0, 0, t * TOP_K + k]
                pltpu.make_async_copy(src_ref.at[pl.ds(t, 1)], xs_ref.at[pl.ds(d, 1)], sem.at[0]).start(
                    priority=k % 2)
            return c
        lax.fori_loop(0, ROUTE_TOK, body, 0, unroll=2)
        for k in range(TOP_K):
            pltpu.make_async_copy(src_ref, xs_ref.at[pl.ds(0, ROUTE_TOK)], sem.at[0]).wait()

    @pl.when(i < prompt_tiles)
    def _():
        scatter(hp_ref)

    @pl.when(i >= prompt_tiles)
    def _():
        scatter(hs_ref)

    @pl.when(i == n_tiles - 1)
    def _():
        zero_copies(lambda cp: cp.wait())


def moe_dispatch(dest, hn_p, hn_s, pad_start, pad_len, n_used, n_blocks):
    tp, d = hn_p.shape
    tsn = hn_s.shape[0]
    assert tp % ROUTE_TOK == 0 and tsn % ROUTE_TOK == 0
    prompt_tiles = tp // ROUTE_TOK
    n_tiles = prompt_tiles + tsn // ROUTE_TOK
    kern = functools.partial(_dispatch_kernel, prompt_tiles=prompt_tiles, n_tiles=n_tiles, n_blocks=n_blocks)
    return pl.pallas_call(
        kern,
        grid_spec=pltpu.PrefetchScalarGridSpec(
            num_scalar_prefetch=3,
            grid=(n_tiles,),
            in_specs=[
                pl.BlockSpec((1, 1, ROUTE_TOK * TOP_K), lambda i, a, b, c: (i, 0, 0), memory_space=pltpu.SMEM),
                pl.BlockSpec((ROUTE_TOK, d), lambda i, a, b, c: (jnp.minimum(i, prompt_tiles - 1), 0)),
                pl.BlockSpec((ROUTE_TOK, d), lambda i, a, b, c: (jnp.maximum(i - prompt_tiles, 0), 0)),
            ],
            out_specs=pl.BlockSpec(memory_space=pl.ANY),
            scratch_shapes=[pltpu.VMEM((MOE_ROWS, d), F32), pltpu.SemaphoreType.DMA((2,))],
        ),
        out_shape=jax.ShapeDtypeStruct((n_blocks * MOE_ROWS, d), F32),
        compiler_params=_cparams(("arbitrary",)),
        name="moe_dispatch",
    )(pad_start, pad_len, n_used, dest.reshape(n_tiles, 1, ROUTE_TOK * TOP_K), hn_p, hn_s)


def _moe_kernel(be_ref, nu_ref, x_ref, wgu_ref, bgu_ref, wd_ref, bd_ref, o_ref, wgu16_ref, wd16_ref):
    b = pl.program_id(0)
    used = b < nu_ref[0]
    new_expert = jnp.logical_or(b == 0, be_ref[b] != be_ref[jnp.maximum(b - 1, 0)])

    @pl.when(jnp.logical_and(used, new_expert))
    def _():
        wgu16_ref[...] = wgu_ref[0].astype(BF16)
        wd16_ref[...] = wd_ref[0].astype(BF16)

    @pl.when(used)
    def _():
        x = x_ref[...].astype(BF16)
        gu = _dot(x, wgu16_ref[...]) + bgu_ref[0]
        d_ff = gu.shape[1] // 2
        gate = jnp.minimum(gu[:, :d_ff], SWIGLU_LIMIT)
        up = jnp.clip(gu[:, d_ff:], -SWIGLU_LIMIT, SWIGLU_LIMIT)
        act = (up + 1.0) * gate * jax.nn.sigmoid(SWIGLU_ALPHA * gate)
        o_ref[...] = _dot(act.astype(BF16), wd16_ref[...]) + bd_ref[0]

    @pl.when(jnp.logical_not(used))
    def _():
        o_ref[...] = jnp.zeros(o_ref.shape, F32)


def moe_experts(xs, block_e, n_used, w_gu, b_gu, w_down, b_down):
    rows, d = xs.shape
    nb = rows // MOE_ROWS
    e, _, gu_w = w_gu.shape
    last = lambda b, nu: jnp.minimum(b, nu[0] - 1)
    return pl.pallas_call(
        _moe_kernel,
        grid_spec=pltpu.PrefetchScalarGridSpec(
            num_scalar_prefetch=2,
            grid=(nb,),
            in_specs=[
                pl.BlockSpec((MOE_ROWS, d), lambda b, be, nu: (last(b, nu), 0)),
                pl.BlockSpec((1, d, gu_w), lambda b, be, nu: (be[last(b, nu)], 0, 0)),
                pl.BlockSpec((1, 1, gu_w), lambda b, be, nu: (be[last(b, nu)], 0, 0)),
                pl.BlockSpec((1, gu_w // 2, d), lambda b, be, nu: (be[last(b, nu)], 0, 0)),
                pl.BlockSpec((1, 1, d), lambda b, be, nu: (be[last(b, nu)], 0, 0)),
            ],
            out_specs=pl.BlockSpec((MOE_ROWS, d), lambda b, be, nu: (b, 0)),
            scratch_shapes=[pltpu.VMEM((d, gu_w), BF16), pltpu.VMEM((gu_w // 2, d), BF16)],
        ),
        out_shape=jax.ShapeDtypeStruct((rows, d), F32),
        compiler_params=_cparams(("arbitrary",)),
        name="moe_experts",
    )(block_e, n_used, xs, w_gu, b_gu.reshape(e, 1, gu_w), w_down, b_down.reshape(e, 1, d))


def _combine_kernel(dfirst_ref, dnext_ref, x2_ref, gate_ref, g_ref, ys_ref, o_ref, buf_ref, sem, *, n_tiles):
    i = pl.program_id(0)
    slot = i % 2

    def issue(dref, sl):
        def body(t, c):
            for k in range(TOP_K):
                d = dref[0, 0, t * TOP_K + k]
                pltpu.make_async_copy(ys_ref.at[pl.ds(d, 1)], buf_ref.at[sl, k, pl.ds(t, 1)], sem.at[sl]).start(
                    priority=k % 2)
            return c
        lax.fori_loop(0, ROUTE_TOK, body, 0, unroll=2)

    @pl.when(i == 0)
    def _():
        issue(dfirst_ref, 0)

    @pl.when(i + 1 < n_tiles)
    def _():
        issue(dnext_ref, 1 - slot)

    for k in range(TOP_K):
        pltpu.make_async_copy(ys_ref.at[pl.ds(0, ROUTE_TOK)], buf_ref.at[slot, k], sem.at[slot]).wait()

    gates = gate_ref[...]
    moe = buf_ref[slot, 0] * gates[:, 0:1]
    for k in range(1, TOP_K):
        moe = moe + buf_ref[slot, k] * gates[:, k:k + 1]
    tok = x2_ref[...] + moe
    y = tok * lax.rsqrt(jnp.mean(tok * tok, axis=-1, keepdims=True) + EPS)
    o_ref[...] = y * g_ref[...]


def combine(x2, ys, dest, gates, norm_final):
    t, d = x2.shape
    assert t % ROUTE_TOK == 0
    n_tiles = t // ROUTE_TOK
    dest3 = dest.reshape(n_tiles, 1, ROUTE_TOK * TOP_K)
    kern = functools.partial(_combine_kernel, n_tiles=n_tiles)
    smem = lambda imap: pl.BlockSpec((1, 1, ROUTE_TOK * TOP_K), imap, memory_space=pltpu.SMEM)
    return pl.pallas_call(
        kern,
        grid=(n_tiles,),
        in_specs=[
            smem(lambda i: (0, 0, 0)),
            smem(lambda i: (jnp.minimum(i + 1, n_tiles - 1), 0, 0)),
            pl.BlockSpec((ROUTE_TOK, d), lambda i: (i, 0)),
            pl.BlockSpec((ROUTE_TOK, ROUTE_W), lambda i: (i, 0)),
            pl.BlockSpec((1, d), lambda i: (0, 0)),
            pl.BlockSpec(memory_space=pl.ANY),
        ],
        out_specs=pl.BlockSpec((ROUTE_TOK, d), lambda i: (i, 0)),
        out_shape=jax.ShapeDtypeStruct((t, d), F32),
        scratch_shapes=[pltpu.VMEM((2, TOP_K, ROUTE_TOK, d), F32), pltpu.SemaphoreType.DMA((2,))],
        compiler_params=_cparams(("arbitrary",)),
        name="combine",
    )(dest3, dest3, x2, gates, norm_final.reshape(1, d), ys)


def kernel(x_prompt, x_sample, cache_k, cache_v, state_ssm, state_conv, norm_mix, w_in, conv_w, conv_b, dt_bias, a_log, d_skip, ssd_norm, w_o_ssd, lambda_q1, lambda_k1, lambda_q2, lambda_k2, subln, w_o_att, w_out, norm_ffn, router_w, router_b, w_gu, b_gu, w_down, b_down, norm_final):
    bp, s, d = x_prompt.shape
    bs, ts, _ = x_sample.shape
    past = cache_k.shape[2]
    depth = w_in.shape[0]
    assert depth == 1
    layer = 0
    lambda_init = 0.8 - 0.6 * math.exp(-0.3 * layer)
    tp, tsn = bp * s, bs * ts
    t_all = tp + tsn

    heads = ATT_N_HEADS
    xp2 = x_prompt.reshape(tp, d)
    xs2 = x_sample.reshape(tsn, d)

    sizes = (SSD_D_INNER, SSD_CONV_CH, SSD_N_HEADS, d, d, d, d, d)
    offs = [0]
    for sz in sizes:
        offs.append(offs[-1] + sz)
    w = w_in[layer]
    seg = lambda i: w[:, offs[i]:offs[i + 1]]
    dt_pad = jnp.zeros((d, PROJ_XD_W - SSD_CONV_CH - SSD_N_HEADS), F32)
    w_all = jnp.concatenate([seg(4), seg(5), seg(0), seg(3), seg(6), seg(7), seg(1), seg(2), dt_pad],
                            axis=1).astype(BF16)
    k_p, v_p, kv16_p, zq_p, xd_p = in_proj(xp2, norm_mix[layer], w_all, tm=1024)
    k_s, v_s, kv16_s, zq_s, xd_s = in_proj(xs2, norm_mix[layer], w_all, tm=tsn)

    prev_p = jnp.zeros((bp, SUBLANES, SSD_CONV_CH), F32)
    h0_p = jnp.zeros((bp, SSD_D_STATE, SSD_D_INNER), F32)
    ssd_w = (conv_w[layer], conv_b[layer], dt_bias[layer], a_log[layer], d_skip[layer], ssd_norm[layer])
    y_p, hT_p = ssd_mixer(xd_p, zq_p, 0, prev_p, h0_p, *ssd_w, b=bp, l=s, tl=256, valid_len=None, out_rows=tp)

    pad_rows = CHUNK - ts
    pad_seq = lambda a: jnp.pad(a.reshape(bs, ts, -1), ((0, 0), (0, pad_rows), (0, 0))).reshape(bs * CHUNK, -1)
    prev_s = jnp.pad(state_conv[layer], ((0, 0), (SUBLANES - (SSD_CONV_WIDTH - 1), 0), (0, 0)))
    h0_s = jnp.swapaxes(state_ssm[layer].reshape(bs, SSD_D_INNER, SSD_D_STATE), 1, 2)
    y_s, hT_s = ssd_mixer(pad_seq(xd_s), pad_seq(zq_s[:, :SSD_D_INNER]), 0, prev_s, h0_s, *ssd_w, b=bs, l=CHUNK,
                          tl=CHUNK, valid_len=ts, out_rows=bs * CHUNK)
    y_s = y_s.reshape(bs, CHUNK, -1)[:, :ts].reshape(tsn, -1)

    lam_rows = jnp.concatenate([lambda_q1[layer][None], lambda_k1[layer][None], lambda_q2[layer][None],
                                lambda_k2[layer][None], jnp.zeros((4, ATT_HEAD_DIM), F32)], axis=0)
    o_p = diff_attention_t(zq_p, 0, SSD_D_INNER // d, kv16_p, 0, kv16_p, 1, lam_rows, subln[layer], b=bp, lq=s,
                           lk=s, tq=512, tk=1024, q_off=0, kv_len=s, lambda_init=lambda_init, out_rows=tp)
    kv_new = kv16_s.reshape(bs, ts, -1)
    kv_len = past + ts
    kv_pad = (-kv_len) % LANES
    lk_s = kv_len + kv_pad
    k_all = jnp.concatenate([cache_k[layer].reshape(bs, past, -1).astype(BF16), kv_new[:, :, :d],
                             jnp.zeros((bs, kv_pad, d), BF16)], axis=1).reshape(bs * lk_s, d)
    v_all = jnp.concatenate([cache_v[layer].reshape(bs, past, -1).astype(BF16), kv_new[:, :, d:],
                             jnp.zeros((bs, kv_pad, d), BF16)], axis=1).reshape(bs * lk_s, d)
    q_s = jnp.pad(zq_s[:, SSD_D_INNER:SSD_D_INNER + d].reshape(bs, ts, d), ((0, 0), (0, LANES - ts), (0, 0)))
    o_s = diff_attention_t(q_s.reshape(bs * LANES, d), 0, 0, k_all, 0, v_all, 0, lam_rows, subln[layer], b=bs,
                           lq=LANES, lk=lk_s, tq=LANES, tk=lk_s, q_off=past, kv_len=kv_len,
                           lambda_init=lambda_init, out_rows=bs * LANES)
    o_s = o_s.reshape(bs, LANES, d)[:, :ts].reshape(tsn, d)

    rw = jnp.zeros((d, LANES), F32).at[:, :N_EXPERTS].set(router_w[layer])
    rb = jnp.zeros((1, LANES), F32).at[0, :N_EXPERTS].set(router_b[layer])
    mix_w = (w_o_ssd[layer].astype(BF16), w_o_att[layer].astype(BF16), w_out[layer].astype(BF16), norm_ffn[layer],
             rw, rb)
    x2_p, hn_p, gate_p, sel_p, cnt_p = mixer_out(y_p, o_p, zq_p, 3, 4, xp2, *mix_w,
                                                 jnp.zeros((1, LANES), F32), tm=512)
    x2_s, hn_s, gate_s, sel_s, cnt_all = mixer_out(y_s, o_s, zq_s, 3, 4, xs2, *mix_w, cnt_p, tm=tsn)

    sel = jnp.concatenate([sel_p, sel_s], axis=0)
    top_e = sel[:, :TOP_K]
    rank = sel[:, TOP_K:]
    cnt = cnt_all[0, :N_EXPERTS].astype(jnp.int32)
    padded = (cnt + MOE_ROWS - 1) // MOE_ROWS * MOE_ROWS
    ends = jnp.cumsum(padded)
    starts = ends - padded
    dest = (starts[top_e] + rank).reshape(-1)
    nb = (t_all * TOP_K + N_EXPERTS * (MOE_ROWS - 1) + MOE_ROWS - 1) // MOE_ROWS
    block_start = jnp.arange(nb, dtype=jnp.int32) * MOE_ROWS
    block_e = jnp.minimum(jnp.sum((ends[None, :] <= block_start[:, None]).astype(jnp.int32), axis=1),
                          N_EXPERTS - 1)
    n_used = (ends[-1] // MOE_ROWS).astype(jnp.int32).reshape(1)

    xs_sorted = moe_dispatch(dest, hn_p, hn_s, starts + cnt, padded - cnt, n_used, nb)
    ys = moe_experts(xs_sorted, block_e, n_used, w_gu[layer], b_gu[layer], w_down[layer], b_down[layer])
    y_prompt = combine(x2_p, ys, dest[:tp * TOP_K], gate_p, norm_final).reshape(bp, s, d)
    y_sample = combine(x2_s, ys, dest[tp * TOP_K:], gate_s, norm_final).reshape(bs, ts, d)

    new_k_p = k_p.reshape(1, bp, s, heads, 2 * ATT_HEAD_DIM)
    new_v_p = v_p.reshape(1, bp, s, heads, ATT_V_HEAD)
    new_k_s = k_s.reshape(1, bs, ts, heads, 2 * ATT_HEAD_DIM)
    new_v_s = v_s.reshape(1, bs, ts, heads, ATT_V_HEAD)
    ssm_p = jnp.swapaxes(hT_p, 1, 2).reshape(1, bp, SSD_N_HEADS, SSD_HEAD_DIM, SSD_D_STATE)
    ssm_s = jnp.swapaxes(hT_s, 1, 2).reshape(1, bs, SSD_N_HEADS, SSD_HEAD_DIM, SSD_D_STATE)
    keep = SSD_CONV_WIDTH - 1
    conv_p = xd_p.reshape(bp, s, -1)[:, s - keep:, :SSD_CONV_CH][None]
    raw_s = jnp.concatenate([state_conv[layer], xd_s.reshape(bs, ts, -1)[:, :, :SSD_CONV_CH]], axis=1)
    conv_s = raw_s[:, -keep:][None]
    return (y_prompt, y_sample, new_k_p, new_v_p, ssm_p, conv_p, new_k_s, new_v_s, ssm_s, conv_s)
```

```python
import functools
import math

import jax
import jax.numpy as jnp
from jax import lax
from jax.experimental import pallas as pl
from jax.experimental.pallas import tpu as pltpu

EPS = 1e-5
CHUNK = 64
D_MODEL = 1024
SSD_D_INNER = 2048
SSD_HEAD_DIM = 64
SSD_N_HEADS = 32
SSD_GROUPS = 4
SSD_D_STATE = 128
SSD_CONV_WIDTH = 4
SSD_CONV_CH = 3072
ATT_HEAD_DIM = 64
ATT_N_HEADS = 8
ATT_V_HEAD = 128
ALIBI_MAX_BIAS = 8.0
N_EXPERTS = 32
TOP_K = 4
SWIGLU_LIMIT = 7.0
SWIGLU_ALPHA = 1.702

LANES = 128
SUBLANES = 8
VMEM_LIMIT = 56 * 1024 * 1024
NEG_BIG = -1e30

BF16 = jnp.bfloat16
F32 = jnp.float32


def _cparams(sem, flags=None):
    return pltpu.CompilerParams(dimension_semantics=sem, vmem_limit_bytes=VMEM_LIMIT, flags=flags)


def _split3(x):
    h1 = x.astype(BF16)
    r1 = x - h1.astype(F32)
    h2 = r1.astype(BF16)
    h3 = (r1 - h2.astype(F32)).astype(BF16)
    return h1, h2, h3


def _dot(a, b):
    return jnp.dot(a, b, preferred_element_type=F32)


def _dot_exact_rhs(x, m):
    h1, h2, h3 = _split3(x)
    return _dot(h1, m) + _dot(h2, m) + _dot(h3, m)


def _dot_exact_lhs(m, x):
    h1, h2, h3 = _split3(x)
    return _dot(m, h1) + _dot(m, h2) + _dot(m, h3)


PROJ_TN = 512
PROJ_K = (0, D_MODEL // PROJ_TN)
PROJ_V = (PROJ_K[1], PROJ_K[1] + D_MODEL // PROJ_TN)
PROJ_ZQ_W = SSD_D_INNER + 3 * D_MODEL
PROJ_ZQ = (PROJ_V[1], PROJ_V[1] + PROJ_ZQ_W // PROJ_TN)
PROJ_XD_W = -(-(SSD_CONV_CH + SSD_N_HEADS) // PROJ_TN) * PROJ_TN
PROJ_XD = (PROJ_ZQ[1], PROJ_ZQ[1] + PROJ_XD_W // PROJ_TN)


def _in_proj_kernel(x_ref, g_ref, w_ref, k_ref, v_ref, kv16_ref, zq_ref, xd_ref, xn_ref):
    j = pl.program_id(1)

    @pl.when(j == 0)
    def _():
        x = x_ref[...]
        y = x * lax.rsqrt(jnp.mean(x * x, axis=-1, keepdims=True) + EPS)
        xn_ref[...] = (y * g_ref[...]).astype(BF16)

    @pl.when(j < PROJ_K[1])
    def _():
        acc = _dot(xn_ref[...], w_ref[...])
        k_ref[...] = acc
        kv16_ref[...] = acc.astype(BF16)

    @pl.when(jnp.logical_and(j >= PROJ_V[0], j < PROJ_V[1]))
    def _():
        acc = _dot(xn_ref[...], w_ref[...])
        v_ref[...] = acc
        kv16_ref[...] = acc.astype(BF16)

    @pl.when(jnp.logical_and(j >= PROJ_ZQ[0], j < PROJ_ZQ[1]))
    def _():
        zq_ref[...] = _dot(xn_ref[...], w_ref[...]).astype(BF16)

    @pl.when(j >= PROJ_XD[0])
    def _():
        xd_ref[...] = _dot(xn_ref[...], w_ref[...])


def in_proj(x, gain, w, tm):
    t, d = x.shape
    assert t % tm == 0 and w.shape[1] == PROJ_XD[1] * PROJ_TN
    tn = PROJ_TN

    def out_map(rng):
        return lambda i, j: (i, jnp.clip(j - rng[0], 0, rng[1] - rng[0] - 1))

    return pl.pallas_call(
        _in_proj_kernel,
        grid=(t // tm, PROJ_XD[1]),
        in_specs=[
            pl.BlockSpec((tm, d), lambda i, j: (i, 0)),
            pl.BlockSpec((1, d), lambda i, j: (0, 0)),
            pl.BlockSpec((d, tn), lambda i, j: (0, j)),
        ],
        out_specs=[
            pl.BlockSpec((tm, tn), out_map(PROJ_K)),
            pl.BlockSpec((tm, tn), out_map(PROJ_V)),
            pl.BlockSpec((tm, tn), out_map((PROJ_K[0], PROJ_V[1]))),
            pl.BlockSpec((tm, tn), out_map(PROJ_ZQ)),
            pl.BlockSpec((tm, tn), out_map(PROJ_XD)),
        ],
        out_shape=[
            jax.ShapeDtypeStruct((t, d), F32),
            jax.ShapeDtypeStruct((t, d), F32),
            jax.ShapeDtypeStruct((t, 2 * d), BF16),
            jax.ShapeDtypeStruct((t, PROJ_ZQ_W), BF16),
            jax.ShapeDtypeStruct((t, PROJ_XD_W), F32),
        ],
        scratch_shapes=[pltpu.VMEM((tm, d), BF16)],
        compiler_params=_cparams(("arbitrary", "arbitrary")),
        name="in_proj",
    )(x, gain.reshape(1, d), w)


CONV_ROWS, CONV_COLS = 64, 512


def _ssd_kernel(xbc_ref, z_ref, prev_ref, h0_ref, cw_ref, cb_ref, dtb_ref, alog_ref, dsk_ref, gn_ref,
                y_ref, hout_ref, xpad_ref, xc_ref, tail_ref, st_ref, *, tl, valid_len):
    t = pl.program_id(1)
    nt = pl.num_programs(1)
    nch = tl // CHUNK
    dinner = SSD_D_INNER
    gw = dinner // SSD_GROUPS
    ns = SSD_D_STATE

    @pl.when(t == 0)
    def _():
        tail_ref[...] = prev_ref[0]
        st_ref[...] = h0_ref[0]

    xpad_ref[0:SUBLANES, :] = tail_ref[...]
    xpad_ref[SUBLANES:SUBLANES + tl, :] = xbc_ref[:, 0:SSD_CONV_CH]
    tail_ref[...] = xbc_ref[tl - SUBLANES:tl, 0:SSD_CONV_CH]
    for r in range(0, tl, CONV_ROWS):
        for c in range(0, SSD_CONV_CH, CONV_COLS):
            acc = cb_ref[:, c:c + CONV_COLS]
            for k in range(SSD_CONV_WIDTH):
                off = r + SUBLANES - (SSD_CONV_WIDTH - 1) + k
                acc = acc + xpad_ref[off:off + CONV_ROWS, c:c + CONV_COLS] * cw_ref[k:k + 1, c:c + CONV_COLS]
            xc_ref[r:r + CONV_ROWS, c:c + CONV_COLS] = acc * jax.nn.sigmoid(acc)

    head_of_lane = lax.broadcasted_iota(jnp.int32, (SSD_N_HEADS, dinner), 1) // SSD_HEAD_DIM
    expand = (head_of_lane == lax.broadcasted_iota(jnp.int32, (SSD_N_HEADS, dinner), 0)).astype(BF16)
    ti = lax.broadcasted_iota(jnp.int32, (CHUNK, CHUNK), 0)
    si = lax.broadcasted_iota(jnp.int32, (CHUNK, CHUNK), 1)
    tril = (si <= ti).astype(BF16)
    row_c = lax.broadcasted_iota(jnp.int32, (CHUNK, dinner), 0)
    pos_in_head = lax.broadcasted_iota(jnp.int32, (CHUNK, dinner), 1) % CHUNK
    upper = row_c <= pos_in_head
    row_p = lax.broadcasted_iota(jnp.int32, (CHUNK, LANES), 0)
    lane_p = lax.broadcasted_iota(jnp.int32, (CHUNK, LANES), 1)
    causal_pair = (lane_p % CHUNK) <= row_p
    left_half = lane_p < SSD_HEAD_DIM
    a_neg_e = _dot_exact_rhs(-jnp.exp(alog_ref[...]), expand)
    dsk_e = _dot_exact_rhs(dsk_ref[...], expand)

    def chunk_body(c, carry):
        r0 = pl.multiple_of(c * CHUNK, CHUNK)
        xs = xc_ref[pl.ds(r0, CHUNK), 0:dinner]
        dt_raw = xbc_ref[pl.ds(r0, CHUNK), SSD_CONV_CH:SSD_CONV_CH + SSD_N_HEADS]
        dtv = dt_raw + dtb_ref[...]
        dt = jnp.maximum(dtv, 0.0) + jnp.log1p(jnp.exp(-jnp.abs(dtv)))
        if valid_len is not None:
            rows = t * tl + r0 + lax.broadcasted_iota(jnp.int32, (CHUNK, SSD_N_HEADS), 0)
            dt = jnp.where(rows < valid_len, dt, 0.0)
        dt_e = _dot_exact_rhs(dt, expand)
        a_e = dt_e * a_neg_e
        acs_e = _dot_exact_lhs(tril, a_e)
        rowterm = jnp.sum(jnp.where(upper, a_e, 0.0), axis=0, keepdims=True)
        acs_last = acs_e[CHUNK - 1:CHUNK, :]
        xdt = xs * dt_e
        x_dec = (xdt * jnp.exp(acs_last - acs_e)).astype(BF16)
        e_acs = jnp.exp(acs_e)
        e_last = jnp.exp(acs_last)

        y_parts = []
        for g in range(SSD_GROUPS):
            lo = g * gw
            bm = xc_ref[pl.ds(r0, CHUNK), dinner + g * ns:dinner + (g + 1) * ns].astype(BF16)
            cm = xc_ref[pl.ds(r0, CHUNK), dinner + SSD_GROUPS * ns + g * ns:
                        dinner + SSD_GROUPS * ns + (g + 1) * ns].astype(BF16)
            b2 = jnp.concatenate([bm, bm], axis=0)
            cb2 = lax.dot_general(cm, b2, (((1,), (1,)), ((), ())), preferred_element_type=F32)
            st_g = st_ref[:, lo:lo + gw]
            y_off = _dot(cm, st_g.astype(BF16)) * e_acs[:, lo:lo + gw]
            pieces = []
            for j in range(gw // LANES):
                l0 = lo + j * LANES
                diff = acs_e[:, l0:l0 + LANES] - rowterm[:, l0:l0 + LANES]
                dec = jnp.where(causal_pair, jnp.exp(diff), 0.0)
                scores = (cb2 * dec).astype(BF16)
                xp = xdt[:, l0:l0 + LANES]
                xblk = jnp.concatenate([jnp.where(left_half, xp, 0.0), jnp.where(left_half, 0.0, xp)],
                                       axis=0).astype(BF16)
                pieces.append(_dot(scores, xblk))
            y_diag = jnp.concatenate(pieces, axis=1)
            upd = lax.dot_general(bm, x_dec[:, lo:lo + gw], (((0,), (0,)), ((), ())),
                                  preferred_element_type=F32)
            st_ref[:, lo:lo + gw] = e_last[:, lo:lo + gw] * st_g + upd
            yg = y_diag + y_off + xs[:, lo:lo + gw] * dsk_e[:, lo:lo + gw]
            zg = z_ref[pl.ds(r0, CHUNK), lo:lo + gw].astype(F32)
            yg = yg * (zg * jax.nn.sigmoid(zg))
            yn = yg * lax.rsqrt(jnp.mean(yg * yg, axis=-1, keepdims=True) + EPS)
            y_parts.append((yn * gn_ref[:, lo:lo + gw]).astype(y_ref.dtype))
        y_ref[pl.ds(r0, CHUNK), :] = jnp.concatenate(y_parts, axis=1)
        return carry

    lax.fori_loop(0, nch, chunk_body, 0)

    @pl.when(t == nt - 1)
    def _():
        hout_ref[0] = st_ref[...]


def ssd_mixer(xbc, z_src, z_col, conv_prev8, h0_t, conv_w, conv_b, dt_bias, a_log, d_skip, ssd_norm,
              b, l, tl, valid_len, out_rows):
    wx = xbc.shape[1]
    assert l % tl == 0 and tl % CHUNK == 0 and CHUNK == SSD_HEAD_DIM
    nt = l // tl
    kern = functools.partial(_ssd_kernel, tl=tl, valid_len=valid_len)
    full = lambda shape: pl.BlockSpec(shape, lambda i, j: (0,) * len(shape))
    return pl.pallas_call(
        kern,
        grid=(b, nt),
        in_specs=[
            pl.BlockSpec((tl, wx), lambda i, j: (i * nt + j, 0)),
            pl.BlockSpec((tl, SSD_D_INNER), lambda i, j: (i * nt + j, z_col)),
            pl.BlockSpec((1, SUBLANES, SSD_CONV_CH), lambda i, j: (i, 0, 0)),
            pl.BlockSpec((1, SSD_D_STATE, SSD_D_INNER), lambda i, j: (i, 0, 0)),
            full((SSD_CONV_WIDTH, SSD_CONV_CH)),
            full((1, SSD_CONV_CH)),
            full((1, SSD_N_HEADS)),
            full((1, SSD_N_HEADS)),
            full((1, SSD_N_HEADS)),
            full((1, SSD_D_INNER)),
        ],
        out_specs=[
            pl.BlockSpec((tl, SSD_D_INNER), lambda i, j: (i * nt + j, 0)),
            pl.BlockSpec((1, SSD_D_STATE, SSD_D_INNER), lambda i, j: (i, 0, 0)),
        ],
        out_shape=[
            jax.ShapeDtypeStruct((out_rows, SSD_D_INNER), BF16),
            jax.ShapeDtypeStruct((b, SSD_D_STATE, SSD_D_INNER), F32),
        ],
        scratch_shapes=[
            pltpu.VMEM((tl + SUBLANES, SSD_CONV_CH), F32),
            pltpu.VMEM((tl, SSD_CONV_CH), F32),
            pltpu.VMEM((SUBLANES, SSD_CONV_CH), F32),
            pltpu.VMEM((SSD_D_STATE, SSD_D_INNER), F32),
        ],
        compiler_params=_cparams(("arbitrary", "arbitrary")),
        name="ssd_mixer",
    )(xbc, z_src, conv_prev8, h0_t, conv_w, conv_b.reshape(1, -1), dt_bias.reshape(1, -1),
      a_log.reshape(1, -1), d_skip.reshape(1, -1), ssd_norm.reshape(1, -1))


BIAS_SPLIT = 32
LOG2E = 1.4426950408889634
LOG2E_PARTS = (1.4453125, -0.00262451171875, 7.063150405883789e-06)
FLAG_FIRST, FLAG_LAST, VARIANT_SHIFT = 1, 2, 2
ATT_SUB = 256
MODE_PLAIN, MODE_MASKED, MODE_SKIP = "plain", "masked", "skip"


def _attn_sub(tk):
    nsub = tk // ATT_SUB if tk % ATT_SUB == 0 else 1
    return nsub, tk // nsub


def _attn_t_kernel(qi_ref, ki_ref, fl_ref, q_ref, k_ref, v_ref, slope_ref, lam_ref, sub_ref, o_ref,
                   m_ref, l_ref, acc_ref, kb_ref, corr_ref, s0_ref, s1_ref, mt0_ref, mt1_ref,
                   *, tq, tk, q_off, kv_len, lambda_init, variants):
    p_idx = pl.program_id(1)
    qi = qi_ref[p_idx]
    ki = ki_ref[p_idx]
    flags = fl_ref[p_idx]
    qstart = q_off + qi * tq
    kstart = ki * tk
    half = ATT_HEAD_DIM
    scale = ATT_HEAD_DIM ** -0.5

    @pl.when((flags & FLAG_FIRST) != 0)
    def _():
        m_ref[...] = jnp.full(m_ref.shape, NEG_BIG, F32)
        l_ref[...] = jnp.zeros(l_ref.shape, F32)
        acc_ref[...] = jnp.zeros(acc_ref.shape, F32)

    lane_k = lax.broadcasted_iota(jnp.int32, (tk, LANES), 1)
    rel = kstart - qstart + lax.broadcasted_iota(jnp.int32, (tk, LANES), 0)
    hi = (rel // BIAS_SPLIT) * BIAS_SPLIT
    lo = rel - hi
    pos_lane = lane_k % half
    n_parts = len(LOG2E_PARTS)
    kb_ref[...] = jnp.where(pos_lane < n_parts, hi, jnp.where(pos_lane < 2 * n_parts, lo, 0)).astype(F32).astype(BF16)

    variant = flags >> VARIANT_SHIFT
    nsub, ts = _attn_sub(tk)

    def mask_terms(modes):
        for j, mode in enumerate(modes):
            if mode != MODE_MASKED:
                continue
            r0 = j * ts
            kpos = kstart + r0 + lax.broadcasted_iota(jnp.int32, (ts, tq), 0)
            qpos = qstart + lax.broadcasted_iota(jnp.int32, (ts, tq), 1)
            allowed = jnp.logical_and(kpos // CHUNK <= qpos // CHUNK, kpos < kv_len)
            corr_ref[0, r0:r0 + ts, :] = jnp.where(kpos > qpos, (2.0 * LOG2E) * (qpos - kpos).astype(F32), 0.0)
            corr_ref[1, r0:r0 + ts, :] = jnp.where(allowed, 0.0, NEG_BIG)

    lane_q = lax.broadcasted_iota(jnp.int32, (tq, LANES), 1)
    lane_ks = lax.broadcasted_iota(jnp.int32, (ts, LANES), 1)
    part = (lax.broadcasted_iota(jnp.int32, (1, LANES), 1) % half) % n_parts
    log2e_lanes = jnp.where(part == 0, LOG2E_PARTS[0], jnp.where(part == 1, LOG2E_PARTS[1], LOG2E_PARTS[2]))

    def score_pass(h, s_ref, mt_ref, modes):
        c0 = pl.multiple_of(h * LANES, LANES)
        q = (q_ref[:, pl.ds(c0, LANES)].astype(F32) * (scale * LOG2E)).astype(BF16)
        slope = slope_ref[pl.ds(h, 1), :]
        slope_b = jnp.broadcast_to((slope * log2e_lanes).astype(BF16), (tq, LANES))
        zero_q = jnp.zeros((tq, LANES), BF16)
        for idx in range(2):
            own = (lane_q < half) if idx == 0 else (lane_q >= half)
            q_aug = jnp.where(own, q, jnp.where((lane_q % half) < 2 * n_parts, slope_b, zero_q))
            own_k = (lane_ks < half) if idx == 0 else (lane_ks >= half)
            mt = None
            for j, mode in enumerate(modes):
                if mode == MODE_SKIP:
                    continue
                r0 = j * ts
                k_aug = jnp.where(own_k, k_ref[r0:r0 + ts, pl.ds(c0, LANES)], kb_ref[r0:r0 + ts, :])
                s = lax.dot_general(k_aug, q_aug, (((1,), (1,)), ((), ())), preferred_element_type=F32)
                if mode == MODE_MASKED:
                    s = s + slope[:, 0:1] * corr_ref[0, r0:r0 + ts, :] + corr_ref[1, r0:r0 + ts, :]
                s_ref[idx, r0:r0 + ts, :] = s
                mj = jnp.max(s, axis=0, keepdims=True)
                mt = mj if mt is None else jnp.maximum(mt, mj)
                yield
            mt_ref[idx] = mt

    def value_pass(h, s_ref, mt_ref, modes):
        c0 = pl.multiple_of(h * LANES, LANES)
        for idx in range(2):
            m_prev = m_ref[idx, h]
            m_new = jnp.maximum(m_prev, mt_ref[idx])
            alpha = jnp.exp2(m_prev - m_new)
            lsum, pv = None, None
            for j, mode in enumerate(modes):
                if mode == MODE_SKIP:
                    continue
                r0 = j * ts
                p = jnp.exp2(s_ref[idx, r0:r0 + ts, :] - m_new)
                lj = jnp.sum(p, axis=0, keepdims=True)
                pj = lax.dot_general(v_ref[r0:r0 + ts, pl.ds(c0, LANES)], p.astype(BF16), (((0,), (0,)), ((), ())),
                                     preferred_element_type=F32)
                lsum = lj if lsum is None else lsum + lj
                pv = pj if pv is None else pv + pj
                yield
            l_ref[idx, h] = alpha * l_ref[idx, h] + lsum
            acc_ref[idx, h] = alpha * acc_ref[idx, h] + pv
            m_ref[idx, h] = m_new

    def run(*gens):
        live = list(gens)
        while live:
            for g in list(live):
                try:
                    next(g)
                except StopIteration:
                    live.remove(g)

    def all_heads(modes):
        bufs = ((s0_ref, mt0_ref), (s1_ref, mt1_ref))
        run(score_pass(0, *bufs[0], modes))

        def pair(g, c):
            h = 2 * g
            run(score_pass(h + 1, *bufs[1], modes), value_pass(h, *bufs[0], modes))
            run(score_pass(h + 2, *bufs[0], modes), value_pass(h + 1, *bufs[1], modes))
            return c

        lax.fori_loop(0, ATT_N_HEADS // 2 - 1, pair, 0)
        last = ATT_N_HEADS - 1
        run(score_pass(last, *bufs[1], modes), value_pass(last - 1, *bufs[0], modes))
        run(value_pass(last, *bufs[1], modes))

    for vi, modes in enumerate(variants):
        @pl.when(variant == vi)
        def _(modes=modes):
            mask_terms(modes)
            all_heads(modes)

    @pl.when((flags & FLAG_LAST) != 0)
    def _():
        lp = lam_ref[...]
        lam = (jnp.exp(jnp.sum(lp[0:1] * lp[1:2], axis=-1, keepdims=True))
               - jnp.exp(jnp.sum(lp[2:3] * lp[3:4], axis=-1, keepdims=True)) + lambda_init)

        def fin(h, c):
            c0 = pl.multiple_of(h * LANES, LANES)
            o = acc_ref[0, h] / l_ref[0, h] - lam * (acc_ref[1, h] / l_ref[1, h])
            on = o * lax.rsqrt(jnp.mean(o * o, axis=0, keepdims=True) + EPS)
            on = (on * sub_ref[...]) * (1.0 - lambda_init)
            o_ref[:, pl.ds(c0, LANES)] = on.T.astype(o_ref.dtype)
            return c

        lax.fori_loop(0, ATT_N_HEADS, fin, 0)


def _attn_pairs(nq, nk, tq, tk, q_off, kv_len):
    nsub, ts = _attn_sub(tk)
    qis, kis, fls, variants = [], [], [], []
    for qi in range(nq):
        first_q = q_off + qi * tq
        last_q = first_q + tq - 1
        kend = min((last_q // CHUNK + 1) * CHUNK, kv_len)
        nkv = -(-kend // tk)
        for ki in range(nkv):
            modes = []
            for j in range(nsub):
                ks = ki * tk + j * ts
                if ks >= kend:
                    modes.append(MODE_SKIP)
                elif ks + ts <= (first_q // CHUNK) * CHUNK and ks + ts <= kv_len:
                    modes.append(MODE_PLAIN)
                else:
                    modes.append(MODE_MASKED)
            modes = tuple(modes)
            if modes not in variants:
                variants.append(modes)
            fl = ((FLAG_FIRST if ki == 0 else 0) | (FLAG_LAST if ki == nkv - 1 else 0)
                  | (variants.index(modes) << VARIANT_SHIFT))
            qis.append(qi)
            kis.append(ki)
            fls.append(fl)
    return qis, kis, fls, tuple(variants)


def diff_attention_t(q_src, q_blk0, q_colblk, k_src, k_colblk, v_src, v_colblk, lam_rows, subln, *, b, lq, lk,
                     tq, tk, q_off, kv_len, lambda_init, out_rows):
    assert lq % tq == 0 and lk % tk == 0 and tq % LANES == 0
    assert q_off + lq <= 256 * BIAS_SPLIT + tq
    nq, nk = lq // tq, lk // tk
    width = ATT_N_HEADS * LANES
    qis, kis, fls, variants = _attn_pairs(nq, nk, tq, tk, q_off, kv_len)
    kern = functools.partial(_attn_t_kernel, tq=tq, tk=tk, q_off=q_off, kv_len=kv_len, lambda_init=lambda_init,
                             variants=variants)
    slopes = jnp.exp2(-ALIBI_MAX_BIAS * jnp.arange(1, ATT_N_HEADS + 1, dtype=F32) / ATT_N_HEADS)
    slopes = jnp.broadcast_to(slopes[:, None], (ATT_N_HEADS, LANES))
    const = lambda shape: pl.BlockSpec(shape, lambda bi, p, qt, kt, ft: (0,) * len(shape))
    return pl.pallas_call(
        kern,
        grid_spec=pltpu.PrefetchScalarGridSpec(
            num_scalar_prefetch=3,
            grid=(b, len(qis)),
            in_specs=[
                pl.BlockSpec((tq, width), lambda bi, p, qt, kt, ft: (q_blk0 + bi * nq + qt[p], q_colblk)),
                pl.BlockSpec((tk, width), lambda bi, p, qt, kt, ft: (bi * nk + kt[p], k_colblk)),
                pl.BlockSpec((tk, width), lambda bi, p, qt, kt, ft: (bi * nk + kt[p], v_colblk)),
                const((ATT_N_HEADS, LANES)),
                const((SUBLANES, ATT_HEAD_DIM)),
                const((ATT_V_HEAD, 1)),
            ],
            out_specs=pl.BlockSpec((tq, width), lambda bi, p, qt, kt, ft: (bi * nq + qt[p], 0)),
            scratch_shapes=[
                pltpu.VMEM((2, ATT_N_HEADS, 1, tq), F32),
                pltpu.VMEM((2, ATT_N_HEADS, 1, tq), F32),
                pltpu.VMEM((2, ATT_N_HEADS, ATT_V_HEAD, tq), F32),
                pltpu.VMEM((tk, LANES), BF16),
                pltpu.VMEM((2, tk, tq), F32),
                pltpu.VMEM((2, tk, tq), F32),
                pltpu.VMEM((2, tk, tq), F32),
                pltpu.VMEM((2, 1, tq), F32),
                pltpu.VMEM((2, 1, tq), F32),
            ],
        ),
        out_shape=jax.ShapeDtypeStruct((out_rows, width), BF16),
        compiler_params=_cparams(("arbitrary", "arbitrary")),
        name="diff_attention_t",
    )(jnp.asarray(qis, jnp.int32), jnp.asarray(kis, jnp.int32), jnp.asarray(fls, jnp.int32),
      q_src, k_src, v_src, slopes, lam_rows, subln.reshape(-1, 1))


ROUTE_W = 2 * TOP_K


def _mixer_out_kernel(y_ref, o_ref, gs_ref, ga_ref, x_ref, wos_ref, woa_ref, wout_ref, nf_ref, rw_ref, rb_ref,
                      run0_ref, x2_ref, h_ref, gate_ref, sel_ref, cnt_ref, run_ref, *, tm):
    i = pl.program_id(0)

    @pl.when(i == 0)
    def _():
        run_ref[...] = run0_ref[...]

    o_ssd = _dot(y_ref[...], wos_ref[...])
    o_att = _dot(o_ref[...], woa_ref[...])
    merged = (jax.nn.sigmoid(gs_ref[...].astype(F32)) * o_ssd
              + jax.nn.sigmoid(ga_ref[...].astype(F32)) * o_att)
    x2 = x_ref[...] + _dot(merged.astype(BF16), wout_ref[...])
    x2_ref[...] = x2
    hn = x2 * lax.rsqrt(jnp.mean(x2 * x2, axis=-1, keepdims=True) + EPS) * nf_ref[...]
    h_ref[...] = hn

    a1, a2, a3 = _split3(hn)
    w1, w2, w3 = _split3(rw_ref[...])
    logits = (_dot(a1, w1) + _dot(a1, w2) + _dot(a2, w1) + _dot(a2, w2) + _dot(a1, w3) + _dot(a3, w1)
              + rb_ref[...])
    lane = lax.broadcasted_iota(jnp.int32, (tm, LANES), 1)
    work = jnp.where(lane < N_EXPERTS, logits, -jnp.inf)
    tops, idxs = [], []
    for _ in range(TOP_K):
        mx = jnp.max(work, axis=-1, keepdims=True)
        ix = jnp.min(jnp.where(work == mx, lane, LANES), axis=-1, keepdims=True)
        tops.append(mx)
        idxs.append(ix)
        work = jnp.where(lane == ix, -jnp.inf, work)
    es = [jnp.exp(tv - tops[0]) for tv in tops]
    den = es[0] + es[1] + es[2] + es[3]
    gates = jnp.zeros((tm, LANES), F32)
    for k in range(TOP_K):
        gates = jnp.where(lane == k, es[k] / den, gates)
    gate_ref[...] = gates[:, :ROUTE_W]

    chosen = jnp.zeros((tm, LANES), jnp.bool_)
    for k in range(TOP_K):
        chosen = jnp.logical_or(chosen, lane == idxs[k])
    multihot = jnp.where(chosen, 1.0, 0.0).astype(BF16)
    ri = lax.broadcasted_iota(jnp.int32, (tm, tm), 0)
    ci = lax.broadcasted_iota(jnp.int32, (tm, tm), 1)
    strict = jnp.where(ci < ri, 1.0, 0.0).astype(BF16)
    prefix = _dot(strict, multihot) + run_ref[...]
    sel = jnp.zeros((tm, LANES), jnp.int32)
    for k in range(TOP_K):
        rank = jnp.sum(jnp.where(lane == idxs[k], prefix, 0.0), axis=-1, keepdims=True)
        sel = jnp.where(lane == k, idxs[k], sel)
        sel = jnp.where(lane == TOP_K + k, rank.astype(jnp.int32), sel)
    sel_ref[...] = sel[:, :ROUTE_W]
    run_ref[...] = run_ref[...] + jnp.sum(multihot.astype(F32), axis=0, keepdims=True)
    cnt_ref[...] = run_ref[...]


def mixer_out(y_ssd, o_att, gates_src, gs_col, ga_col, x, w_o_ssd, w_o_att, w_out, norm_ffn, rw, rb, counts0, tm):
    t, d = x.shape
    assert t % tm == 0
    kern = functools.partial(_mixer_out_kernel, tm=tm)
    const = lambda shape: pl.BlockSpec(shape, lambda i: (0, 0))
    return pl.pallas_call(
        kern,
        grid=(t // tm,),
        in_specs=[
            pl.BlockSpec((tm, SSD_D_INNER), lambda i: (i, 0)),
            pl.BlockSpec((tm, d), lambda i: (i, 0)),
            pl.BlockSpec((tm, d), lambda i: (i, gs_col)),
            pl.BlockSpec((tm, d), lambda i: (i, ga_col)),
            pl.BlockSpec((tm, d), lambda i: (i, 0)),
            const((SSD_D_INNER, d)),
            const((d, d)),
            const((d, d)),
            const((1, d)),
            const((d, LANES)),
            const((1, LANES)),
            const((1, LANES)),
        ],
        out_specs=[
            pl.BlockSpec((tm, d), lambda i: (i, 0)),
            pl.BlockSpec((tm, d), lambda i: (i, 0)),
            pl.BlockSpec((tm, ROUTE_W), lambda i: (i, 0)),
            pl.BlockSpec((tm, ROUTE_W), lambda i: (i, 0)),
            pl.BlockSpec((1, LANES), lambda i: (0, 0)),
        ],
        out_shape=[
            jax.ShapeDtypeStruct((t, d), F32),
            jax.ShapeDtypeStruct((t, d), F32),
            jax.ShapeDtypeStruct((t, ROUTE_W), F32),
            jax.ShapeDtypeStruct((t, ROUTE_W), jnp.int32),
            jax.ShapeDtypeStruct((1, LANES), F32),
        ],
        scratch_shapes=[pltpu.VMEM((1, LANES), F32)],
        compiler_params=_cparams(("arbitrary",)),
        name="mixer_out",
    )(y_ssd, o_att, gates_src, gates_src, x, w_o_ssd, w_o_att, w_out, norm_ffn.reshape(1, d), rw, rb, counts0)


MOE_ROWS = 256
ROUTE_TOK = 256


def _dispatch_kernel(ps_ref, pl_ref, nu_ref, dest_ref, hp_ref, hs_ref, xs_ref, zero_ref, sem,
                     *, prompt_tiles, n_tiles, n_blocks):
    i = pl.program_id(0)

    def zero_copies(action):
        def per_expert(e, c):
            def row(r, c2):
                action(pltpu.make_async_copy(zero_ref.at[pl.ds(0, 1)], xs_ref.at[pl.ds(ps_ref[e] + r, 1)],
                                             sem.at[1]))
                return c2
            return lax.fori_loop(0, pl_ref[e], row, c)

        lax.fori_loop(0, N_EXPERTS, per_expert, 0)

        def tail(b, c):
            row0 = pl.multiple_of(b * MOE_ROWS, MOE_ROWS)
            action(pltpu.make_async_copy(zero_ref, xs_ref.at[pl.ds(row0, MOE_ROWS)], sem.at[1]))
            return c

        lax.fori_loop(nu_ref[0], n_blocks, tail, 0)

    @pl.when(i == 0)
    def _():
        zero_ref[...] = jnp.zeros(zero_ref.shape, F32)
        zero_copies(lambda cp: cp.start())

    def scatter(src_ref):
        def body(t, c):
            for k in range(TOP_K):
                d = dest_ref[0, 0, t * TOP_K + k]
                pltpu.make_async_copy(src_ref.at[pl.ds(t, 1)], xs_ref.at[pl.ds(d, 1)], sem.at[0]).start(
                    priority=k % 2)
            return c
        lax.fori_loop(0, ROUTE_TOK, body, 0, unroll=2)
        for k in range(TOP_K):
            pltpu.make_async_copy(src_ref, xs_ref.at[pl.ds(0, ROUTE_TOK)], sem.at[0]).wait()

    @pl.when(i < prompt_tiles)
    def _():
        scatter(hp_ref)

    @pl.when(i >= prompt_tiles)
    def _():
        scatter(hs_ref)

    @pl.when(i == n_tiles - 1)
    def _():
        zero_copies(lambda cp: cp.wait())


def moe_dispatch(dest, hn_p, hn_s, pad_start, pad_len, n_used, n_blocks):
    tp, d = hn_p.shape
    tsn = hn_s.shape[0]
    assert tp % ROUTE_TOK == 0 and tsn % ROUTE_TOK == 0
    prompt_tiles = tp // ROUTE_TOK
    n_tiles = prompt_tiles + tsn // ROUTE_TOK
    kern = functools.partial(_dispatch_kernel, prompt_tiles=prompt_tiles, n_tiles=n_tiles, n_blocks=n_blocks)
    return pl.pallas_call(
        kern,
        grid_spec=pltpu.PrefetchScalarGridSpec(
            num_scalar_prefetch=3,
            grid=(n_tiles,),
            in_specs=[
                pl.BlockSpec((1, 1, ROUTE_TOK * TOP_K), lambda i, a, b, c: (i, 0, 0), memory_space=pltpu.SMEM),
                pl.BlockSpec((ROUTE_TOK, d), lambda i, a, b, c: (jnp.minimum(i, prompt_tiles - 1), 0)),
                pl.BlockSpec((ROUTE_TOK, d), lambda i, a, b, c: (jnp.maximum(i - prompt_tiles, 0), 0)),
            ],
            out_specs=pl.BlockSpec(memory_space=pl.ANY),
            scratch_shapes=[pltpu.VMEM((MOE_ROWS, d), F32), pltpu.SemaphoreType.DMA((2,))],
        ),
        out_shape=jax.ShapeDtypeStruct((n_blocks * MOE_ROWS, d), F32),
        compiler_params=_cparams(("arbitrary",)),
        name="moe_dispatch",
    )(pad_start, pad_len, n_used, dest.reshape(n_tiles, 1, ROUTE_TOK * TOP_K), hn_p, hn_s)


def _moe_kernel(be_ref, nu_ref, x_ref, wgu_ref, bgu_ref, wd_ref, bd_ref, o_ref, wgu16_ref, wd16_ref):
    b = pl.program_id(0)
    used = b < nu_ref[0]
    new_expert = jnp.logical_or(b == 0, be_ref[b] != be_ref[jnp.maximum(b - 1, 0)])

    @pl.when(jnp.logical_and(used, new_expert))
    def _():
        wgu16_ref[...] = wgu_ref[0].astype(BF16)
        wd16_ref[...] = wd_ref[0].astype(BF16)

    @pl.when(used)
    def _():
        x = x_ref[...].astype(BF16)
        gu = _dot(x, wgu16_ref[...]) + bgu_ref[0]
        d_ff = gu.shape[1] // 2
        gate = jnp.minimum(gu[:, :d_ff], SWIGLU_LIMIT)
        up = jnp.clip(gu[:, d_ff:], -SWIGLU_LIMIT, SWIGLU_LIMIT)
        act = (up + 1.0) * gate * jax.nn.sigmoid(SWIGLU_ALPHA * gate)
        o_ref[...] = _dot(act.astype(BF16), wd16_ref[...]) + bd_ref[0]

    @pl.when(jnp.logical_not(used))
    def _():
        o_ref[...] = jnp.zeros(o_ref.shape, F32)


def moe_experts(xs, block_e, n_used, w_gu, b_gu, w_down, b_down):
    rows, d = xs.shape
    nb = rows // MOE_ROWS
    e, _, gu_w = w_gu.shape
    last = lambda b, nu: jnp.minimum(b, nu[0] - 1)
    return pl.pallas_call(
        _moe_kernel,
        grid_spec=pltpu.PrefetchScalarGridSpec(
            num_scalar_prefetch=2,
            grid=(nb,),
            in_specs=[
                pl.BlockSpec((MOE_ROWS, d), lambda b, be, nu: (last(b, nu), 0)),
                pl.BlockSpec((1, d, gu_w), lambda b, be, nu: (be[last(b, nu)], 0, 0)),
                pl.BlockSpec((1, 1, gu_w), lambda b, be, nu: (be[last(b, nu)], 0, 0)),
                pl.BlockSpec((1, gu_w // 2, d), lambda b, be, nu: (be[last(b, nu)], 0, 0)),
                pl.BlockSpec((1, 1, d), lambda b, be, nu: (be[last(b, nu)], 0, 0)),
            ],
            out_specs=pl.BlockSpec((MOE_ROWS, d), lambda b, be, nu: (b, 0)),
            scratch_shapes=[pltpu.VMEM((d, gu_w), BF16), pltpu.VMEM((gu_w // 2, d), BF16)],
        ),
        out_shape=jax.ShapeDtypeStruct((rows, d), F32),
        compiler_params=_cparams(("arbitrary",)),
        name="moe_experts",
    )(block_e, n_used, xs, w_gu, b_gu.reshape(e, 1, gu_w), w_down, b_down.reshape(e, 1, d))


def _combine_kernel(dfirst_ref, dnext_ref, x2_ref, gate_ref, g_ref, ys_ref, o_ref, buf_ref, sem, *, n_tiles):
    i = pl.program_id(0)
    slot = i % 2

    def issue(dref, sl):
        def body(t, c):
            for k in range(TOP_K):
                d = dref[0, 0, t * TOP_K + k]
                pltpu.make_async_copy(ys_ref.at[pl.ds(d, 1)], buf_ref.at[sl, k, pl.ds(t, 1)], sem.at[sl]).start(
                    priority=k % 2)
            return c
        lax.fori_loop(0, ROUTE_TOK, body, 0, unroll=2)

    @pl.when(i == 0)
    def _():
        issue(dfirst_ref, 0)

    @pl.when(i + 1 < n_tiles)
    def _():
        issue(dnext_ref, 1 - slot)

    for k in range(TOP_K):
        pltpu.make_async_copy(ys_ref.at[pl.ds(0, ROUTE_TOK)], buf_ref.at[slot, k], sem.at[slot]).wait()

    gates = gate_ref[...]
    moe = buf_ref[slot, 0] * gates[:, 0:1]
    for k in range(1, TOP_K):
        moe = moe + buf_ref[slot, k] * gates[:, k:k + 1]
    tok = x2_ref[...] + moe
    y = tok * lax.rsqrt(jnp.mean(tok * tok, axis=-1, keepdims=True) + EPS)
    o_ref[...] = y * g_ref[...]


def combine(x2, ys, dest, gates, norm_final):
    t, d = x2.shape
    assert t % ROUTE_TOK == 0
    n_tiles = t // ROUTE_TOK
    dest3 = dest.reshape(n_tiles, 1, ROUTE_TOK * TOP_K)
    kern = functools.partial(_combine_kernel, n_tiles=n_tiles)
    smem = lambda imap: pl.BlockSpec((1, 1, ROUTE_TOK * TOP_K), imap, memory_space=pltpu.SMEM)
    return pl.pallas_call(
        kern,
        grid=(n_tiles,),
        in_specs=[
            smem(lambda i: (0, 0, 0)),
            smem(lambda i: (jnp.minimum(i + 1, n_tiles - 1), 0, 0)),
            pl.BlockSpec((ROUTE_TOK, d), lambda i: (i, 0)),
            pl.BlockSpec((ROUTE_TOK, ROUTE_W), lambda i: (i, 0)),
            pl.BlockSpec((1, d), lambda i: (0, 0)),
            pl.BlockSpec(memory_space=pl.ANY),
        ],
        out_specs=pl.BlockSpec((ROUTE_TOK, d), lambda i: (i, 0)),
        out_shape=jax.ShapeDtypeStruct((t, d), F32),
        scratch_shapes=[pltpu.VMEM((2, TOP_K, ROUTE_TOK, d), F32), pltpu.SemaphoreType.DMA((2,))],
        compiler_params=_cparams(("arbitrary",)),
        name="combine",
    )(dest3, dest3, x2, gates, norm_final.reshape(1, d), ys)


def kernel(x_prompt, x_sample, cache_k, cache_v, state_ssm, state_conv, norm_mix, w_in, conv_w, conv_b, dt_bias, a_log, d_skip, ssd_norm, w_o_ssd, lambda_q1, lambda_k1, lambda_q2, lambda_k2, subln, w_o_att, w_out, norm_ffn, router_w, router_b, w_gu, b_gu, w_down, b_down, norm_final):
    bp, s, d = x_prompt.shape
    bs, ts, _ = x_sample.shape
    past = cache_k.shape[2]
    depth = w_in.shape[0]
    assert depth == 1
    layer = 0
    lambda_init = 0.8 - 0.6 * math.exp(-0.3 * layer)
    tp, tsn = bp * s, bs * ts
    t_all = tp + tsn

    heads = ATT_N_HEADS
    xp2 = x_prompt.reshape(tp, d)
    xs2 = x_sample.reshape(tsn, d)

    sizes = (SSD_D_INNER, SSD_CONV_CH, SSD_N_HEADS, d, d, d, d, d)
    offs = [0]
    for sz in sizes:
        offs.append(offs[-1] + sz)
    w = w_in[layer]
    seg = lambda i: w[:, offs[i]:offs[i + 1]]
    dt_pad = jnp.zeros((d, PROJ_XD_W - SSD_CONV_CH - SSD_N_HEADS), F32)
    w_all = jnp.concatenate([seg(4), seg(5), seg(0), seg(3), seg(6), seg(7), seg(1), seg(2), dt_pad],
                            axis=1).astype(BF16)
    k_p, v_p, kv16_p, zq_p, xd_p = in_proj(xp2, norm_mix[layer], w_all, tm=1024)
    k_s, v_s, kv16_s, zq_s, xd_s = in_proj(xs2, norm_mix[layer], w_all, tm=tsn)

    prev_p = jnp.zeros((bp, SUBLANES, SSD_CONV_CH), F32)
    h0_p = jnp.zeros((bp, SSD_D_STATE, SSD_D_INNER), F32)
    ssd_w = (conv_w[layer], conv_b[layer], dt_bias[layer], a_log[layer], d_skip[layer], ssd_norm[layer])
    y_p, hT_p = ssd_mixer(xd_p, zq_p, 0, prev_p, h0_p, *ssd_w, b=bp, l=s, tl=256, valid_len=None, out_rows=tp)

    pad_rows = CHUNK - ts
    pad_seq = lambda a: jnp.pad(a.reshape(bs, ts, -1), ((0, 0), (0, pad_rows), (0, 0))).reshape(bs * CHUNK, -1)
    prev_s = jnp.pad(state_conv[layer], ((0, 0), (SUBLANES - (SSD_CONV_WIDTH - 1), 0), (0, 0)))
    h0_s = jnp.swapaxes(state_ssm[layer].reshape(bs, SSD_D_INNER, SSD_D_STATE), 1, 2)
    y_s, hT_s = ssd_mixer(pad_seq(xd_s), pad_seq(zq_s[:, :SSD_D_INNER]), 0, prev_s, h0_s, *ssd_w, b=bs, l=CHUNK,
                          tl=CHUNK, valid_len=ts, out_rows=bs * CHUNK)
    y_s = y_s.reshape(bs, CHUNK, -1)[:, :ts].reshape(tsn, -1)

    lam_rows = jnp.concatenate([lambda_q1[layer][None], lambda_k1[layer][None], lambda_q2[layer][None],
                                lambda_k2[layer][None], jnp.zeros((4, ATT_HEAD_DIM), F32)], axis=0)
    o_p = diff_attention_t(zq_p, 0, SSD_D_INNER // d, kv16_p, 0, kv16_p, 1, lam_rows, subln[layer], b=bp, lq=s,
                           lk=s, tq=512, tk=1024, q_off=0, kv_len=s, lambda_init=lambda_init, out_rows=tp)
    kv_new = kv16_s.reshape(bs, ts, -1)
    kv_len = past + ts
    kv_pad = (-kv_len) % LANES
    lk_s = kv_len + kv_pad
    k_all = jnp.concatenate([cache_k[layer].reshape(bs, past, -1).astype(BF16), kv_new[:, :, :d],
                             jnp.zeros((bs, kv_pad, d), BF16)], axis=1).reshape(bs * lk_s, d)
    v_all = jnp.concatenate([cache_v[layer].reshape(bs, past, -1).astype(BF16), kv_new[:, :, d:],
                             jnp.zeros((bs, kv_pad, d), BF16)], axis=1).reshape(bs * lk_s, d)
    q_s = jnp.pad(zq_s[:, SSD_D_INNER:SSD_D_INNER + d].reshape(bs, ts, d), ((0, 0), (0, LANES - ts), (0, 0)))
    o_s = diff_attention_t(q_s.reshape(bs * LANES, d), 0, 0, k_all, 0, v_all, 0, lam_rows, subln[layer], b=bs,
                           lq=LANES, lk=lk_s, tq=LANES, tk=lk_s, q_off=past, kv_len=kv_len,
                           lambda_init=lambda_init, out_rows=bs * LANES)
    o_s = o_s.reshape(bs, LANES, d)[:, :ts].reshape(tsn, d)

    rw = jnp.zeros((d, LANES), F32).at[:, :N_EXPERTS].set(router_w[layer])
    rb = jnp.zeros((1, LANES), F32).at[0, :N_EXPERTS].set(router_b[layer])
    mix_w = (w_o_ssd[layer].astype(BF16), w_o_att[layer].astype(BF16), w_out[layer].astype(BF16), norm_ffn[layer],
             rw, rb)
    x2_p, hn_p, gate_p, sel_p, cnt_p = mixer_out(y_p, o_p, zq_p, 3, 4, xp2, *mix_w,
                                                 jnp.zeros((1, LANES), F32), tm=512)
    x2_s, hn_s, gate_s, sel_s, cnt_all = mixer_out(y_s, o_s, zq_s, 3, 4, xs2, *mix_w, cnt_p, tm=tsn)

    sel = jnp.concatenate([sel_p, sel_s], axis=0)
    top_e = sel[:, :TOP_K]
    rank = sel[:, TOP_K:]
    cnt = cnt_all[0, :N_EXPERTS].astype(jnp.int32)
    padded = (cnt + MOE_ROWS - 1) // MOE_ROWS * MOE_ROWS
    ends = jnp.cumsum(padded)
    starts = ends - padded
    dest = (starts[top_e] + rank).reshape(-1)
    nb = (t_all * TOP_K + N_EXPERTS * (MOE_ROWS - 1) + MOE_ROWS - 1) // MOE_ROWS
    block_start = jnp.arange(nb, dtype=jnp.int32) * MOE_ROWS
    block_e = jnp.minimum(jnp.sum((ends[None, :] <= block_start[:, None]).astype(jnp.int32), axis=1),
                          N_EXPERTS - 1)
    n_used = (ends[-1] // MOE_ROWS).astype(jnp.int32).reshape(1)

    xs_sorted = moe_dispatch(dest, hn_p, hn_s, starts + cnt, padded - cnt, n_used, nb)
    ys = moe_experts(xs_sorted, block_e, n_used, w_gu[layer], b_gu[layer], w_down[layer], b_down[layer])
    y_prompt = combine(x2_p, ys, dest[:tp * TOP_K], gate_p, norm_final).reshape(bp, s, d)
    y_sample = combine(x2_s, ys, dest[tp * TOP_K:], gate_s, norm_final).reshape(bs, ts, d)

    new_k_p = k_p.reshape(1, bp, s, heads, 2 * ATT_HEAD_DIM)
    new_v_p = v_p.reshape(1, bp, s, heads, ATT_V_HEAD)
    new_k_s = k_s.reshape(1, bs, ts, heads, 2 * ATT_HEAD_DIM)
    new_v_s = v_s.reshape(1, bs, ts, heads, ATT_V_HEAD)
    ssm_p = jnp.swapaxes(hT_p, 1, 2).reshape(1, bp, SSD_N_HEADS, SSD_HEAD_DIM, SSD_D_STATE)
    ssm_s = jnp.swapaxes(hT_s, 1, 2).reshape(1, bs, SSD_N_HEADS, SSD_HEAD_DIM, SSD_D_STATE)
    keep = SSD_CONV_WIDTH - 1
    conv_p = xd_p.reshape(bp, s, -1)[:, s - keep:, :SSD_CONV_CH][None]
    raw_s = jnp.concatenate([state_conv[layer], xd_s.reshape(bs, ts, -1)[:, :, :SSD_CONV_CH]], axis=1)
    conv_s = raw_s[:, -keep:][None]
    return (y_prompt, y_sample, new_k_p, new_v_p, ssm_p, conv_p, new_k_s, new_v_s, ssm_s, conv_s)
```

```python
import functools
import math

import jax
import jax.numpy as jnp
from jax import lax
from jax.experimental import pallas as pl
from jax.experimental.pallas import tpu as pltpu

EPS = 1e-5
CHUNK = 64
D_MODEL = 1024
SSD_D_INNER = 2048
SSD_HEAD_DIM = 64
SSD_N_HEADS = 32
SSD_GROUPS = 4
SSD_D_STATE = 128
SSD_CONV_WIDTH = 4
SSD_CONV_CH = 3072
ATT_HEAD_DIM = 64
ATT_N_HEADS = 8
ATT_V_HEAD = 128
ALIBI_MAX_BIAS = 8.0
N_EXPERTS = 32
TOP_K = 4
SWIGLU_LIMIT = 7.0
SWIGLU_ALPHA = 1.702

LANES = 128
SUBLANES = 8
VMEM_LIMIT = 56 * 1024 * 1024
NEG_BIG = -1e30

BF16 = jnp.bfloat16
F32 = jnp.float32


def _cparams(sem, flags=None):
    return pltpu.CompilerParams(dimension_semantics=sem, vmem_limit_bytes=VMEM_LIMIT, flags=flags)


def _split3(x):
    h1 = x.astype(BF16)
    r1 = x - h1.astype(F32)
    h2 = r1.astype(BF16)
    h3 = (r1 - h2.astype(F32)).astype(BF16)
    return h1, h2, h3


def _dot(a, b):
    return jnp.dot(a, b, preferred_element_type=F32)


def _dot_exact_rhs(x, m):
    h1, h2, h3 = _split3(x)
    return _dot(h1, m) + _dot(h2, m) + _dot(h3, m)


def _dot_exact_lhs(m, x):
    h1, h2, h3 = _split3(x)
    return _dot(m, h1) + _dot(m, h2) + _dot(m, h3)


PROJ_TN = 512
PROJ_K = (0, D_MODEL // PROJ_TN)
PROJ_V = (PROJ_K[1], PROJ_K[1] + D_MODEL // PROJ_TN)
PROJ_ZQ_W = SSD_D_INNER + 3 * D_MODEL
PROJ_ZQ = (PROJ_V[1], PROJ_V[1] + PROJ_ZQ_W // PROJ_TN)
PROJ_XD_W = -(-(SSD_CONV_CH + SSD_N_HEADS) // PROJ_TN) * PROJ_TN
PROJ_XD = (PROJ_ZQ[1], PROJ_ZQ[1] + PROJ_XD_W // PROJ_TN)


def _in_proj_kernel(x_ref, g_ref, w_ref, k_ref, v_ref, kv16_ref, zq_ref, xd_ref, xn_ref):
    j = pl.program_id(1)

    @pl.when(j == 0)
    def _():
        x = x_ref[...]
        y = x * lax.rsqrt(jnp.mean(x * x, axis=-1, keepdims=True) + EPS)
        xn_ref[...] = (y * g_ref[...]).astype(BF16)

    @pl.when(j < PROJ_K[1])
    def _():
        acc = _dot(xn_ref[...], w_ref[...])
        k_ref[...] = acc
        kv16_ref[...] = acc.astype(BF16)

    @pl.when(jnp.logical_and(j >= PROJ_V[0], j < PROJ_V[1]))
    def _():
        acc = _dot(xn_ref[...], w_ref[...])
        v_ref[...] = acc
        kv16_ref[...] = acc.astype(BF16)

    @pl.when(jnp.logical_and(j >= PROJ_ZQ[0], j < PROJ_ZQ[1]))
    def _():
        zq_ref[...] = _dot(xn_ref[...], w_ref[...]).astype(BF16)

    @pl.when(j >= PROJ_XD[0])
    def _():
        xd_ref[...] = _dot(xn_ref[...], w_ref[...])


def in_proj(x, gain, w, tm):
    t, d = x.shape
    assert t % tm == 0 and w.shape[1] == PROJ_XD[1] * PROJ_TN
    tn = PROJ_TN

    def out_map(rng):
        return lambda i, j: (i, jnp.clip(j - rng[0], 0, rng[1] - rng[0] - 1))

    return pl.pallas_call(
        _in_proj_kernel,
        grid=(t // tm, PROJ_XD[1]),
        in_specs=[
            pl.BlockSpec((tm, d), lambda i, j: (i, 0)),
            pl.BlockSpec((1, d), lambda i, j: (0, 0)),
            pl.BlockSpec((d, tn), lambda i, j: (0, j)),
        ],
        out_specs=[
            pl.BlockSpec((tm, tn), out_map(PROJ_K)),
            pl.BlockSpec((tm, tn), out_map(PROJ_V)),
            pl.BlockSpec((tm, tn), out_map((PROJ_K[0], PROJ_V[1]))),
            pl.BlockSpec((tm, tn), out_map(PROJ_ZQ)),
            pl.BlockSpec((tm, tn), out_map(PROJ_XD)),
        ],
        out_shape=[
            jax.ShapeDtypeStruct((t, d), F32),
            jax.ShapeDtypeStruct((t, d), F32),
            jax.ShapeDtypeStruct((t, 2 * d), BF16),
            jax.ShapeDtypeStruct((t, PROJ_ZQ_W), BF16),
            jax.ShapeDtypeStruct((t, PROJ_XD_W), F32),
        ],
        scratch_shapes=[pltpu.VMEM((tm, d), BF16)],
        compiler_params=_cparams(("arbitrary", "arbitrary")),
        name="in_proj",
    )(x, gain.reshape(1, d), w)


CONV_ROWS, CONV_COLS = 64, 512


def _ssd_kernel(xbc_ref, z_ref, prev_ref, h0_ref, cw_ref, cb_ref, dtb_ref, alog_ref, dsk_ref, gn_ref,
                y_ref, hout_ref, xpad_ref, xc_ref, tail_ref, st_ref, *, tl, valid_len):
    t = pl.program_id(1)
    nt = pl.num_programs(1)
    nch = tl // CHUNK
    dinner = SSD_D_INNER
    gw = dinner // SSD_GROUPS
    ns = SSD_D_STATE

    @pl.when(t == 0)
    def _():
        tail_ref[...] = prev_ref[0]
        st_ref[...] = h0_ref[0]

    xpad_ref[0:SUBLANES, :] = tail_ref[...]
    xpad_ref[SUBLANES:SUBLANES + tl, :] = xbc_ref[:, 0:SSD_CONV_CH]
    tail_ref[...] = xbc_ref[tl - SUBLANES:tl, 0:SSD_CONV_CH]
    for r in range(0, tl, CONV_ROWS):
        for c in range(0, SSD_CONV_CH, CONV_COLS):
            acc = cb_ref[:, c:c + CONV_COLS]
            for k in range(SSD_CONV_WIDTH):
                off = r + SUBLANES - (SSD_CONV_WIDTH - 1) + k
                acc = acc + xpad_ref[off:off + CONV_ROWS, c:c + CONV_COLS] * cw_ref[k:k + 1, c:c + CONV_COLS]
            xc_ref[r:r + CONV_ROWS, c:c + CONV_COLS] = acc * jax.nn.sigmoid(acc)

    head_of_lane = lax.broadcasted_iota(jnp.int32, (SSD_N_HEADS, dinner), 1) // SSD_HEAD_DIM
    expand = (head_of_lane == lax.broadcasted_iota(jnp.int32, (SSD_N_HEADS, dinner), 0)).astype(BF16)
    ti = lax.broadcasted_iota(jnp.int32, (CHUNK, CHUNK), 0)
    si = lax.broadcasted_iota(jnp.int32, (CHUNK, CHUNK), 1)
    tril = (si <= ti).astype(BF16)
    row_c = lax.broadcasted_iota(jnp.int32, (CHUNK, dinner), 0)
    pos_in_head = lax.broadcasted_iota(jnp.int32, (CHUNK, dinner), 1) % CHUNK
    upper = row_c <= pos_in_head
    row_p = lax.broadcasted_iota(jnp.int32, (CHUNK, LANES), 0)
    lane_p = lax.broadcasted_iota(jnp.int32, (CHUNK, LANES), 1)
    causal_pair = (lane_p % CHUNK) <= row_p
    left_half = lane_p < SSD_HEAD_DIM
    a_neg_e = _dot_exact_rhs(-jnp.exp(alog_ref[...]), expand)
    dsk_e = _dot_exact_rhs(dsk_ref[...], expand)

    def chunk_body(c, carry):
        r0 = pl.multiple_of(c * CHUNK, CHUNK)
        xs = xc_ref[pl.ds(r0, CHUNK), 0:dinner]
        dt_raw = xbc_ref[pl.ds(r0, CHUNK), SSD_CONV_CH:SSD_CONV_CH + SSD_N_HEADS]
        dtv = dt_raw + dtb_ref[...]
        dt = jnp.maximum(dtv, 0.0) + jnp.log1p(jnp.exp(-jnp.abs(dtv)))
        if valid_len is not None:
            rows = t * tl + r0 + lax.broadcasted_iota(jnp.int32, (CHUNK, SSD_N_HEADS), 0)
            dt = jnp.where(rows < valid_len, dt, 0.0)
        dt_e = _dot_exact_rhs(dt, expand)
        a_e = dt_e * a_neg_e
        acs_e = _dot_exact_lhs(tril, a_e)
        rowterm = jnp.sum(jnp.where(upper, a_e, 0.0), axis=0, keepdims=True)
        acs_last = acs_e[CHUNK - 1:CHUNK, :]
        xdt = xs * dt_e
        x_dec = (xdt * jnp.exp(acs_last - acs_e)).astype(BF16)
        e_acs = jnp.exp(acs_e)
        e_last = jnp.exp(acs_last)

        y_parts = []
        for g in range(SSD_GROUPS):
            lo = g * gw
            bm = xc_ref[pl.ds(r0, CHUNK), dinner + g * ns:dinner + (g + 1) * ns].astype(BF16)
            cm = xc_ref[pl.ds(r0, CHUNK), dinner + SSD_GROUPS * ns + g * ns:
                        dinner + SSD_GROUPS * ns + (g + 1) * ns].astype(BF16)
            b2 = jnp.concatenate([bm, bm], axis=0)
            cb2 = lax.dot_general(cm, b2, (((1,), (1,)), ((), ())), preferred_element_type=F32)
            st_g = st_ref[:, lo:lo + gw]
            y_off = _dot(cm, st_g.astype(BF16)) * e_acs[:, lo:lo + gw]
            pieces = []
            for j in range(gw // LANES):
                l0 = lo + j * LANES
                diff = acs_e[:, l0:l0 + LANES] - rowterm[:, l0:l0 + LANES]
                dec = jnp.where(causal_pair, jnp.exp(diff), 0.0)
                scores = (cb2 * dec).astype(BF16)
                xp = xdt[:, l0:l0 + LANES]
                xblk = jnp.concatenate([jnp.where(left_half, xp, 0.0), jnp.where(left_half, 0.0, xp)],
                                       axis=0).astype(BF16)
                pieces.append(_dot(scores, xblk))
            y_diag = jnp.concatenate(pieces, axis=1)
            upd = lax.dot_general(bm, x_dec[:, lo:lo + gw], (((0,), (0,)), ((), ())),
                                  preferred_element_type=F32)
            st_ref[:, lo:lo + gw] = e_last[:, lo:lo + gw] * st_g + upd
            yg = y_diag + y_off + xs[:, lo:lo + gw] * dsk_e[:, lo:lo + gw]
            zg = z_ref[pl.ds(r0, CHUNK), lo:lo + gw].astype(F32)
            yg = yg * (zg * jax.nn.sigmoid(zg))
            yn = yg * lax.rsqrt(jnp.mean(yg * yg, axis=-1, keepdims=True) + EPS)
            y_parts.append((yn * gn_ref[:, lo:lo + gw]).astype(y_ref.dtype))
        y_ref[pl.ds(r0, CHUNK), :] = jnp.concatenate(y_parts, axis=1)
        return carry

    lax.fori_loop(0, nch, chunk_body, 0)

    @pl.when(t == nt - 1)
    def _():
        hout_ref[0] = st_ref[...]


def ssd_mixer(xbc, z_src, z_col, conv_prev8, h0_t, conv_w, conv_b, dt_bias, a_log, d_skip, ssd_norm,
              b, l, tl, valid_len, out_rows):
    wx = xbc.shape[1]
    assert l % tl == 0 and tl % CHUNK == 0 and CHUNK == SSD_HEAD_DIM
    nt = l // tl
    kern = functools.partial(_ssd_kernel, tl=tl, valid_len=valid_len)
    full = lambda shape: pl.BlockSpec(shape, lambda i, j: (0,) * len(shape))
    return pl.pallas_call(
        kern,
        grid=(b, nt),
        in_specs=[
            pl.BlockSpec((tl, wx), lambda i, j: (i * nt + j, 0)),
            pl.BlockSpec((tl, SSD_D_INNER), lambda i, j: (i * nt + j, z_col)),
            pl.BlockSpec((1, SUBLANES, SSD_CONV_CH), lambda i, j: (i, 0, 0)),
            pl.BlockSpec((1, SSD_D_STATE, SSD_D_INNER), lambda i, j: (i, 0, 0)),
            full((SSD_CONV_WIDTH, SSD_CONV_CH)),
            full((1, SSD_CONV_CH)),
            full((1, SSD_N_HEADS)),
            full((1, SSD_N_HEADS)),
            full((1, SSD_N_HEADS)),
            full((1, SSD_D_INNER)),
        ],
        out_specs=[
            pl.BlockSpec((tl, SSD_D_INNER), lambda i, j: (i * nt + j, 0)),
            pl.BlockSpec((1, SSD_D_STATE, SSD_D_INNER), lambda i, j: (i, 0, 0)),
        ],
        out_shape=[
            jax.ShapeDtypeStruct((out_rows, SSD_D_INNER), BF16),
            jax.ShapeDtypeStruct((b, SSD_D_STATE, SSD_D_INNER), F32),
        ],
        scratch_shapes=[
            pltpu.VMEM((tl + SUBLANES, SSD_CONV_CH), F32),
            pltpu.VMEM((tl, SSD_CONV_CH), F32),
            pltpu.VMEM((SUBLANES, SSD_CONV_CH), F32),
            pltpu.VMEM((SSD_D_STATE, SSD_D_INNER), F32),
        ],
        compiler_params=_cparams(("arbitrary", "arbitrary")),
        name="ssd_mixer",
    )(xbc, z_src, conv_prev8, h0_t, conv_w, conv_b.reshape(1, -1), dt_bias.reshape(1, -1),
      a_log.reshape(1, -1), d_skip.reshape(1, -1), ssd_norm.reshape(1, -1))


BIAS_SPLIT = 32
LOG2E = 1.4426950408889634
LOG2E_PARTS = (1.4453125, -0.00262451171875, 7.063150405883789e-06)
FLAG_FIRST, FLAG_LAST, VARIANT_SHIFT = 1, 2, 2
ATT_SUB = 256
MODE_PLAIN, MODE_MASKED, MODE_SKIP = "plain", "masked", "skip"


def _attn_sub(tk):
    nsub = tk // ATT_SUB if tk % ATT_SUB == 0 else 1
    return nsub, tk // nsub


def _attn_t_kernel(qi_ref, ki_ref, fl_ref, q_ref, k_ref, v_ref, slope_ref, lam_ref, sub_ref, o_ref,
                   m_ref, l_ref, acc_ref, kb_ref, corr_ref, s0_ref, s1_ref, mt0_ref, mt1_ref,
                   *, tq, tk, q_off, kv_len, lambda_init, variants):
    p_idx = pl.program_id(1)
    qi = qi_ref[p_idx]
    ki = ki_ref[p_idx]
    flags = fl_ref[p_idx]
    qstart = q_off + qi * tq
    kstart = ki * tk
    half = ATT_HEAD_DIM
    scale = ATT_HEAD_DIM ** -0.5

    @pl.when((flags & FLAG_FIRST) != 0)
    def _():
        m_ref[...] = jnp.full(m_ref.shape, NEG_BIG, F32)
        l_ref[...] = jnp.zeros(l_ref.shape, F32)
        acc_ref[...] = jnp.zeros(acc_ref.shape, F32)

    lane_k = lax.broadcasted_iota(jnp.int32, (tk, LANES), 1)
    rel = kstart - qstart + lax.broadcasted_iota(jnp.int32, (tk, LANES), 0)
    hi = (rel // BIAS_SPLIT) * BIAS_SPLIT
    lo = rel - hi
    pos_lane = lane_k % half
    n_parts = len(LOG2E_PARTS)
    kb_ref[...] = jnp.where(pos_lane < n_parts, hi, jnp.where(pos_lane < 2 * n_parts, lo, 0)).astype(F32).astype(BF16)

    variant = flags >> VARIANT_SHIFT
    nsub, ts = _attn_sub(tk)

    def mask_terms(modes):
        for j, mode in enumerate(modes):
            if mode != MODE_MASKED:
                continue
            r0 = j * ts
            kpos = kstart + r0 + lax.broadcasted_iota(jnp.int32, (ts, tq), 0)
            qpos = qstart + lax.broadcasted_iota(jnp.int32, (ts, tq), 1)
            allowed = jnp.logical_and(kpos // CHUNK <= qpos // CHUNK, kpos < kv_len)
            corr_ref[0, r0:r0 + ts, :] = jnp.where(kpos > qpos, (2.0 * LOG2E) * (qpos - kpos).astype(F32), 0.0)
            corr_ref[1, r0:r0 + ts, :] = jnp.where(allowed, 0.0, NEG_BIG)

    lane_q = lax.broadcasted_iota(jnp.int32, (tq, LANES), 1)
    lane_ks = lax.broadcasted_iota(jnp.int32, (ts, LANES), 1)
    part = (lax.broadcasted_iota(jnp.int32, (1, LANES), 1) % half) % n_parts
    log2e_lanes = jnp.where(part == 0, LOG2E_PARTS[0], jnp.where(part == 1, LOG2E_PARTS[1], LOG2E_PARTS[2]))

    def score_pass(h, s_ref, mt_ref, modes):
        c0 = pl.multiple_of(h * LANES, LANES)
        q = (q_ref[:, pl.ds(c0, LANES)].astype(F32) * (scale * LOG2E)).astype(BF16)
        slope = slope_ref[pl.ds(h, 1), :]
        slope_b = jnp.broadcast_to((slope * log2e_lanes).astype(BF16), (tq, LANES))
        zero_q = jnp.zeros((tq, LANES), BF16)
        for idx in range(2):
            own = (lane_q < half) if idx == 0 else (lane_q >= half)
            q_aug = jnp.where(own, q, jnp.where((lane_q % half) < 2 * n_parts, slope_b, zero_q))
            own_k = (lane_ks < half) if idx == 0 else (lane_ks >= half)
            mt = None
            for j, mode in enumerate(modes):
                if mode == MODE_SKIP:
                    continue
                r0 = j * ts
                k_aug = jnp.where(own_k, k_ref[r0:r0 + ts, pl.ds(c0, LANES)], kb_ref[r0:r0 + ts, :])
                s = lax.dot_general(k_aug, q_aug, (((1,), (1,)), ((), ())), preferred_element_type=F32)
                if mode == MODE_MASKED:
                    s = s + slope[:, 0:1] * corr_ref[0, r0:r0 + ts, :] + corr_ref[1, r0:r0 + ts, :]
                s_ref[idx, r0:r0 + ts, :] = s
                mj = jnp.max(s, axis=0, keepdims=True)
                mt = mj if mt is None else jnp.maximum(mt, mj)
                yield
            mt_ref[idx] = mt

    def value_pass(h, s_ref, mt_ref, modes):
        c0 = pl.multiple_of(h * LANES, LANES)
        for idx in range(2):
            m_prev = m_ref[idx, h]
            m_new = jnp.maximum(m_prev, mt_ref[idx])
            alpha = jnp.exp2(m_prev - m_new)
            lsum, pv = None, None
            for j, mode in enumerate(modes):
                if mode == MODE_SKIP:
                    continue
                r0 = j * ts
                p = jnp.exp2(s_ref[idx, r0:r0 + ts, :] - m_new)
                lj = jnp.sum(p, axis=0, keepdims=True)
                pj = lax.dot_general(v_ref[r0:r0 + ts, pl.ds(c0, LANES)], p.astype(BF16), (((0,), (0,)), ((), ())),
                                     preferred_element_type=F32)
                lsum = lj if lsum is None else lsum + lj
                pv = pj if pv is None else pv + pj
                yield
            l_ref[idx, h] = alpha * l_ref[idx, h] + lsum
            acc_ref[idx, h] = alpha * acc_ref[idx, h] + pv
            m_ref[idx, h] = m_new

    def run(*gens):
        live = list(gens)
        while live:
            for g in list(live):
                try:
                    next(g)
                except StopIteration:
                    live.remove(g)

    def all_heads(modes):
        bufs = ((s0_ref, mt0_ref), (s1_ref, mt1_ref))
        run(score_pass(0, *bufs[0], modes))

        def pair(g, c):
            h = 2 * g
            run(score_pass(h + 1, *bufs[1], modes), value_pass(h, *bufs[0], modes))
            run(score_pass(h + 2, *bufs[0], modes), value_pass(h + 1, *bufs[1], modes))
            return c

        lax.fori_loop(0, ATT_N_HEADS // 2 - 1, pair, 0)
        last = ATT_N_HEADS - 1
        run(score_pass(last, *bufs[1], modes), value_pass(last - 1, *bufs[0], modes))
        run(value_pass(last, *bufs[1], modes))

    for vi, modes in enumerate(variants):
        @pl.when(variant == vi)
        def _(modes=modes):
            mask_terms(modes)
            all_heads(modes)

    @pl.when((flags & FLAG_LAST) != 0)
    def _():
        lp = lam_ref[...]
        lam = (jnp.exp(jnp.sum(lp[0:1] * lp[1:2], axis=-1, keepdims=True))
               - jnp.exp(jnp.sum(lp[2:3] * lp[3:4], axis=-1, keepdims=True)) + lambda_init)

        def fin(h, c):
            c0 = pl.multiple_of(h * LANES, LANES)
            o = acc_ref[0, h] / l_ref[0, h] - lam * (acc_ref[1, h] / l_ref[1, h])
            on = o * lax.rsqrt(jnp.mean(o * o, axis=0, keepdims=True) + EPS)
            on = (on * sub_ref[...]) * (1.0 - lambda_init)
            o_ref[:, pl.ds(c0, LANES)] = on.T.astype(o_ref.dtype)
            return c

        lax.fori_loop(0, ATT_N_HEADS, fin, 0)


def _attn_pairs(nq, nk, tq, tk, q_off, kv_len):
    nsub, ts = _attn_sub(tk)
    qis, kis, fls, variants = [], [], [], []
    for qi in range(nq):
        first_q = q_off + qi * tq
        last_q = first_q + tq - 1
        kend = min((last_q // CHUNK + 1) * CHUNK, kv_len)
        nkv = -(-kend // tk)
        for ki in range(nkv):
            modes = []
            for j in range(nsub):
                ks = ki * tk + j * ts
                if ks >= kend:
                    modes.append(MODE_SKIP)
                elif ks + ts <= (first_q // CHUNK) * CHUNK and ks + ts <= kv_len:
                    modes.append(MODE_PLAIN)
                else:
                    modes.append(MODE_MASKED)
            modes = tuple(modes)
            if modes not in variants:
                variants.append(modes)
            fl = ((FLAG_FIRST if ki == 0 else 0) | (FLAG_LAST if ki == nkv - 1 else 0)
                  | (variants.index(modes) << VARIANT_SHIFT))
            qis.append(qi)
            kis.append(ki)
            fls.append(fl)
    return qis, kis, fls, tuple(variants)


def diff_attention_t(q_src, q_blk0, q_colblk, k_src, k_colblk, v_src, v_colblk, lam_rows, subln, *, b, lq, lk,
                     tq, tk, q_off, kv_len, lambda_init, out_rows):
    assert lq % tq == 0 and lk % tk == 0 and tq % LANES == 0
    assert q_off + lq <= 256 * BIAS_SPLIT + tq
    nq, nk = lq // tq, lk // tk
    width = ATT_N_HEADS * LANES
    qis, kis, fls, variants = _attn_pairs(nq, nk, tq, tk, q_off, kv_len)
    kern = functools.partial(_attn_t_kernel, tq=tq, tk=tk, q_off=q_off, kv_len=kv_len, lambda_init=lambda_init,
                             variants=variants)
    slopes = jnp.exp2(-ALIBI_MAX_BIAS * jnp.arange(1, ATT_N_HEADS + 1, dtype=F32) / ATT_N_HEADS)
    slopes = jnp.broadcast_to(slopes[:, None], (ATT_N_HEADS, LANES))
    const = lambda shape: pl.BlockSpec(shape, lambda bi, p, qt, kt, ft: (0,) * len(shape))
    return pl.pallas_call(
        kern,
        grid_spec=pltpu.PrefetchScalarGridSpec(
            num_scalar_prefetch=3,
            grid=(b, len(qis)),
            in_specs=[
                pl.BlockSpec((tq, width), lambda bi, p, qt, kt, ft: (q_blk0 + bi * nq + qt[p], q_colblk)),
                pl.BlockSpec((tk, width), lambda bi, p, qt, kt, ft: (bi * nk + kt[p], k_colblk)),
                pl.BlockSpec((tk, width), lambda bi, p, qt, kt, ft: (bi * nk + kt[p], v_colblk)),
                const((ATT_N_HEADS, LANES)),
                const((SUBLANES, ATT_HEAD_DIM)),
                const((ATT_V_HEAD, 1)),
            ],
            out_specs=pl.BlockSpec((tq, width), lambda bi, p, qt, kt, ft: (bi * nq + qt[p], 0)),
            scratch_shapes=[
                pltpu.VMEM((2, ATT_N_HEADS, 1, tq), F32),
                pltpu.VMEM((2, ATT_N_HEADS, 1, tq), F32),
                pltpu.VMEM((2, ATT_N_HEADS, ATT_V_HEAD, tq), F32),
                pltpu.VMEM((tk, LANES), BF16),
                pltpu.VMEM((2, tk, tq), F32),
                pltpu.VMEM((2, tk, tq), F32),
                pltpu.VMEM((2, tk, tq), F32),
                pltpu.VMEM((2, 1, tq), F32),
                pltpu.VMEM((2, 1, tq), F32),
            ],
        ),
        out_shape=jax.ShapeDtypeStruct((out_rows, width), BF16),
        compiler_params=_cparams(("arbitrary", "arbitrary")),
        name="diff_attention_t",
    )(jnp.asarray(qis, jnp.int32), jnp.asarray(kis, jnp.int32), jnp.asarray(fls, jnp.int32),
      q_src, k_src, v_src, slopes, lam_rows, subln.reshape(-1, 1))


def _attn_cache_kernel(q_ref, ck_ref, cv_ref, kn_ref, vn_ref, slope_ref, lam_ref, sub_ref, o_ref,
                       m_ref, l_ref, acc_ref, *, tq, tkc, ncache, past, n_new, lambda_init):
    j = pl.program_id(1)
    half = ATT_HEAD_DIM
    n_parts = len(LOG2E_PARTS)
    scale = ATT_HEAD_DIM ** -0.5

    @pl.when(j == 0)
    def _():
        m_ref[...] = jnp.full(m_ref.shape, NEG_BIG, F32)
        l_ref[...] = jnp.zeros(l_ref.shape, F32)
        acc_ref[...] = jnp.zeros(acc_ref.shape, F32)

    lane_q = lax.broadcasted_iota(jnp.int32, (tq, LANES), 1)
    part = (lax.broadcasted_iota(jnp.int32, (1, LANES), 1) % half) % n_parts
    log2e_lanes = jnp.where(part == 0, LOG2E_PARTS[0], jnp.where(part == 1, LOG2E_PARTS[1], LOG2E_PARTS[2]))

    def position_lanes(rows, kstart):
        lane = lax.broadcasted_iota(jnp.int32, (rows, LANES), 1)
        rel = kstart - past + lax.broadcasted_iota(jnp.int32, (rows, LANES), 0)
        hi = (rel // BIAS_SPLIT) * BIAS_SPLIT
        pos = lane % half
        kb = jnp.where(pos < n_parts, hi, jnp.where(pos < 2 * n_parts, rel - hi, 0)).astype(F32).astype(BF16)
        return kb, lane

    def head(h, k, v, kb, lane_k, corr):
        q = (q_ref[0, :, h * LANES:(h + 1) * LANES].astype(F32) * (scale * LOG2E)).astype(BF16)
        slope = slope_ref[h:h + 1, :]
        slope_b = jnp.broadcast_to((slope * log2e_lanes).astype(BF16), (tq, LANES))
        for idx in range(2):
            own = (lane_q < half) if idx == 0 else (lane_q >= half)
            q_aug = jnp.where(own, q, jnp.where((lane_q % half) < 2 * n_parts, slope_b, jnp.zeros_like(q)))
            own_k = (lane_k < half) if idx == 0 else (lane_k >= half)
            s = lax.dot_general(q_aug, jnp.where(own_k, k, kb), (((1,), (1,)), ((), ())),
                                preferred_element_type=F32)
            if corr is not None:
                s = s + slope[:, 0:1] * corr[0] + corr[1]
            m_prev = m_ref[idx, h]
            m_new = jnp.maximum(m_prev, jnp.max(s, axis=1, keepdims=True))
            alpha = jnp.exp2(m_prev - m_new)
            p = jnp.exp2(s - m_new)
            l_ref[idx, h] = alpha * l_ref[idx, h] + jnp.sum(p, axis=1, keepdims=True)
            acc_ref[idx, h] = alpha * acc_ref[idx, h] + _dot(p.astype(BF16), v)
            m_ref[idx, h] = m_new

    @pl.when(j < ncache)
    def _():
        kb, lane_k = position_lanes(tkc, j * tkc)
        k_heads = pltpu.einshape("khd->hkd", ck_ref[0])
        v_heads = pltpu.einshape("khd->hkd", cv_ref[0])
        for h in range(ATT_N_HEADS):
            head(h, k_heads[h].astype(BF16), v_heads[h].astype(BF16), kb, lane_k, None)

    @pl.when(j == ncache)
    def _():
        kb, lane_k = position_lanes(n_new, past)
        kpos = past + lax.broadcasted_iota(jnp.int32, (tq, n_new), 1)
        qpos = past + lax.broadcasted_iota(jnp.int32, (tq, n_new), 0)
        corr = (jnp.where(kpos > qpos, (2.0 * LOG2E) * (qpos - kpos).astype(F32), 0.0),
                jnp.where(kpos // CHUNK <= qpos // CHUNK, 0.0, NEG_BIG))
        lp = lam_ref[...]
        lam = (jnp.exp(jnp.sum(lp[0:1] * lp[1:2], axis=-1, keepdims=True))
               - jnp.exp(jnp.sum(lp[2:3] * lp[3:4], axis=-1, keepdims=True)) + lambda_init)
        for h in range(ATT_N_HEADS):
            cols = slice(h * LANES, (h + 1) * LANES)
            head(h, kn_ref[0, :, cols], vn_ref[0, :, cols], kb, lane_k, corr)
            o = acc_ref[0, h] / l_ref[0, h] - lam * (acc_ref[1, h] / l_ref[1, h])
            on = o * lax.rsqrt(jnp.mean(o * o, axis=1, keepdims=True) + EPS)
            on = (on * sub_ref[...]) * (1.0 - lambda_init)
            o_ref[0, :, cols] = on.astype(o_ref.dtype)


def diff_attention_cached(q_src, q_colblk, cache_k, cache_v, kv_new, lam_rows, subln, *, tkc, lambda_init):
    b, tq, _ = q_src.shape
    width = ATT_N_HEADS * LANES
    past = cache_k.shape[1]
    n_new = kv_new.shape[1]
    assert past % tkc == 0 and past % CHUNK == 0 and past <= 256 * BIAS_SPLIT
    ncache = past // tkc
    kern = functools.partial(_attn_cache_kernel, tq=tq, tkc=tkc, ncache=ncache, past=past, n_new=n_new,
                             lambda_init=lambda_init)
    slopes = jnp.exp2(-ALIBI_MAX_BIAS * jnp.arange(1, ATT_N_HEADS + 1, dtype=F32) / ATT_N_HEADS)
    slopes = jnp.broadcast_to(slopes[:, None], (ATT_N_HEADS, LANES))
    cache_spec = pl.BlockSpec((1, tkc, ATT_N_HEADS, LANES), lambda bi, j: (bi, jnp.minimum(j, ncache - 1), 0, 0))
    const = lambda shape: pl.BlockSpec(shape, lambda bi, j: (0,) * len(shape))
    return pl.pallas_call(
        kern,
        grid=(b, ncache + 1),
        in_specs=[
            pl.BlockSpec((1, tq, width), lambda bi, j: (bi, 0, q_colblk)),
            cache_spec,
            cache_spec,
            pl.BlockSpec((1, n_new, width), lambda bi, j: (bi, 0, 0)),
            pl.BlockSpec((1, n_new, width), lambda bi, j: (bi, 0, 1)),
            const((ATT_N_HEADS, LANES)),
            const((SUBLANES, ATT_HEAD_DIM)),
            const((1, ATT_V_HEAD)),
        ],
        out_specs=pl.BlockSpec((1, tq, width), lambda bi, j: (bi, 0, 0)),
        out_shape=jax.ShapeDtypeStruct((b, tq, width), BF16),
        scratch_shapes=[
            pltpu.VMEM((2, ATT_N_HEADS, tq, 1), F32),
            pltpu.VMEM((2, ATT_N_HEADS, tq, 1), F32),
            pltpu.VMEM((2, ATT_N_HEADS, tq, ATT_V_HEAD), F32),
        ],
        compiler_params=_cparams(("arbitrary", "arbitrary")),
        name="diff_attention_cached",
    )(q_src, cache_k, cache_v, kv_new, kv_new, slopes, lam_rows, subln.reshape(1, -1))


ROUTE_W = 2 * TOP_K


def _mixer_out_kernel(y_ref, o_ref, gs_ref, ga_ref, x_ref, wos_ref, woa_ref, wout_ref, nf_ref, rw_ref, rb_ref,
                      run0_ref, x2_ref, h_ref, gate_ref, sel_ref, cnt_ref, run_ref, *, tm):
    i = pl.program_id(0)

    @pl.when(i == 0)
    def _():
        run_ref[...] = run0_ref[...]

    o_ssd = _dot(y_ref[...], wos_ref[...])
    o_att = _dot(o_ref[...], woa_ref[...])
    merged = (jax.nn.sigmoid(gs_ref[...].astype(F32)) * o_ssd
              + jax.nn.sigmoid(ga_ref[...].astype(F32)) * o_att)
    x2 = x_ref[...] + _dot(merged.astype(BF16), wout_ref[...])
    x2_ref[...] = x2
    hn = x2 * lax.rsqrt(jnp.mean(x2 * x2, axis=-1, keepdims=True) + EPS) * nf_ref[...]
    h_ref[...] = hn

    a1, a2, a3 = _split3(hn)
    w1, w2, w3 = _split3(rw_ref[...])
    logits = (_dot(a1, w1) + _dot(a1, w2) + _dot(a2, w1) + _dot(a2, w2) + _dot(a1, w3) + _dot(a3, w1)
              + rb_ref[...])
    lane = lax.broadcasted_iota(jnp.int32, (tm, LANES), 1)
    work = jnp.where(lane < N_EXPERTS, logits, -jnp.inf)
    tops, idxs = [], []
    for _ in range(TOP_K):
        mx = jnp.max(work, axis=-1, keepdims=True)
        ix = jnp.min(jnp.where(work == mx, lane, LANES), axis=-1, keepdims=True)
        tops.append(mx)
        idxs.append(ix)
        work = jnp.where(lane == ix, -jnp.inf, work)
    es = [jnp.exp(tv - tops[0]) for tv in tops]
    den = es[0] + es[1] + es[2] + es[3]
    gates = jnp.zeros((tm, LANES), F32)
    for k in range(TOP_K):
        gates = jnp.where(lane == k, es[k] / den, gates)
    gate_ref[...] = gates[:, :ROUTE_W]

    chosen = jnp.zeros((tm, LANES), jnp.bool_)
    for k in range(TOP_K):
        chosen = jnp.logical_or(chosen, lane == idxs[k])
    multihot = jnp.where(chosen, 1.0, 0.0).astype(BF16)
    ri = lax.broadcasted_iota(jnp.int32, (tm, tm), 0)
    ci = lax.broadcasted_iota(jnp.int32, (tm, tm), 1)
    strict = jnp.where(ci < ri, 1.0, 0.0).astype(BF16)
    prefix = _dot(strict, multihot) + run_ref[...]
    sel = jnp.zeros((tm, LANES), jnp.int32)
    for k in range(TOP_K):
        rank = jnp.sum(jnp.where(lane == idxs[k], prefix, 0.0), axis=-1, keepdims=True)
        sel = jnp.where(lane == k, idxs[k], sel)
        sel = jnp.where(lane == TOP_K + k, rank.astype(jnp.int32), sel)
    sel_ref[...] = sel[:, :ROUTE_W]
    run_ref[...] = run_ref[...] + jnp.sum(multihot.astype(F32), axis=0, keepdims=True)
    cnt_ref[...] = run_ref[...]


def mixer_out(y_ssd, o_att, gates_src, gs_col, ga_col, x, w_o_ssd, w_o_att, w_out, norm_ffn, rw, rb, counts0, tm):
    t, d = x.shape
    assert t % tm == 0
    kern = functools.partial(_mixer_out_kernel, tm=tm)
    const = lambda shape: pl.BlockSpec(shape, lambda i: (0, 0))
    return pl.pallas_call(
        kern,
        grid=(t // tm,),
        in_specs=[
            pl.BlockSpec((tm, SSD_D_INNER), lambda i: (i, 0)),
            pl.BlockSpec((tm, d), lambda i: (i, 0)),
            pl.BlockSpec((tm, d), lambda i: (i, gs_col)),
            pl.BlockSpec((tm, d), lambda i: (i, ga_col)),
            pl.BlockSpec((tm, d), lambda i: (i, 0)),
            const((SSD_D_INNER, d)),
            const((d, d)),
            const((d, d)),
            const((1, d)),
            const((d, LANES)),
            const((1, LANES)),
            const((1, LANES)),
        ],
        out_specs=[
            pl.BlockSpec((tm, d), lambda i: (i, 0)),
            pl.BlockSpec((tm, d), lambda i: (i, 0)),
            pl.BlockSpec((tm, ROUTE_W), lambda i: (i, 0)),
            pl.BlockSpec((tm, ROUTE_W), lambda i: (i, 0)),
            pl.BlockSpec((1, LANES), lambda i: (0, 0)),
        ],
        out_shape=[
            jax.ShapeDtypeStruct((t, d), F32),
            jax.ShapeDtypeStruct((t, d), F32),
            jax.ShapeDtypeStruct((t, ROUTE_W), F32),
            jax.ShapeDtypeStruct((t, ROUTE_W), jnp.int32),
            jax.ShapeDtypeStruct((1, LANES), F32),
        ],
        scratch_shapes=[pltpu.VMEM((1, LANES), F32)],
        compiler_params=_cparams(("arbitrary",)),
        name="mixer_out",
    )(y_ssd, o_att, gates_src, gates_src, x, w_o_ssd, w_o_att, w_out, norm_ffn.reshape(1, d), rw, rb, counts0)


MOE_ROWS = 256
ROUTE_TOK = 256


def _dispatch_kernel(ps_ref, pl_ref, nu_ref, dest_ref, hp_ref, hs_ref, xs_ref, zero_ref, sem,
                     *, prompt_tiles, n_tiles, n_blocks):
    i = pl.program_id(0)

    def zero_copies(action):
        def per_expert(e, c):
            def row(r, c2):
                action(pltpu.make_async_copy(zero_ref.at[pl.ds(0, 1)], xs_ref.at[pl.ds(ps_ref[e] + r, 1)],
                                             sem.at[1]))
                return c2
            return lax.fori_loop(0, pl_ref[e], row, c)

        lax.fori_loop(0, N_EXPERTS, per_expert, 0)

        def tail(b, c):
            row0 = pl.multiple_of(b * MOE_ROWS, MOE_ROWS)
            action(pltpu.make_async_copy(zero_ref, xs_ref.at[pl.ds(row0, MOE_ROWS)], sem.at[1]))
            return c

        lax.fori_loop(nu_ref[0], n_blocks, tail, 0)

    @pl.when(i == 0)
    def _():
        zero_ref[...] = jnp.zeros(zero_ref.shape, F32)
        zero_copies(lambda cp: cp.start())

    def scatter(src_ref):
        def body(t, c):
            for k in range(TOP_K):
                d = dest_ref[0, 0, t * TOP_K + k]
                pltpu.make_async_copy(src_ref.at[pl.ds(t, 1)], xs_ref.at[pl.ds(d, 1)], sem.at[0]).start(
                    priority=k % 2)
            return c
        lax.fori_loop(0, ROUTE_TOK, body, 0, unroll=2)
        for k in range(TOP_K):
            pltpu.make_async_copy(src_ref, xs_ref.at[pl.ds(0, ROUTE_TOK)], sem.at[0]).wait()

    @pl.when(i < prompt_tiles)
    def _():
        scatter(hp_ref)

    @pl.when(i >= prompt_tiles)
    def _():
        scatter(hs_ref)

    @pl.when(i == n_tiles - 1)
    def _():
        zero_copies(lambda cp: cp.wait())


def moe_dispatch(dest, hn_p, hn_s, pad_start, pad_len, n_used, n_blocks):
    tp, d = hn_p.shape
    tsn = hn_s.shape[0]
    assert tp % ROUTE_TOK == 0 and tsn % ROUTE_TOK == 0
    prompt_tiles = tp // ROUTE_TOK
    n_tiles = prompt_tiles + tsn // ROUTE_TOK
    kern = functools.partial(_dispatch_kernel, prompt_tiles=prompt_tiles, n_tiles=n_tiles, n_blocks=n_blocks)
    return pl.pallas_call(
        kern,
        grid_spec=pltpu.PrefetchScalarGridSpec(
            num_scalar_prefetch=3,
            grid=(n_tiles,),
            in_specs=[
                pl.BlockSpec((1, 1, ROUTE_TOK * TOP_K), lambda i, a, b, c: (i, 0, 0), memory_space=pltpu.SMEM),
                pl.BlockSpec((ROUTE_TOK, d), lambda i, a, b, c: (jnp.minimum(i, prompt_tiles - 1), 0)),
                pl.BlockSpec((ROUTE_TOK, d), lambda i, a, b, c: (jnp.maximum(i - prompt_tiles, 0), 0)),
            ],
            out_specs=pl.BlockSpec(memory_space=pl.ANY),
            scratch_shapes=[pltpu.VMEM((MOE_ROWS, d), F32), pltpu.SemaphoreType.DMA((2,))],
        ),
        out_shape=jax.ShapeDtypeStruct((n_blocks * MOE_ROWS, d), F32),
        compiler_params=_cparams(("arbitrary",)),
        name="moe_dispatch",
    )(pad_start, pad_len, n_used, dest.reshape(n_tiles, 1, ROUTE_TOK * TOP_K), hn_p, hn_s)


def _moe_kernel(be_ref, nu_ref, x_ref, wgu_ref, bgu_ref, wd_ref, bd_ref, o_ref, wgu16_ref, wd16_ref):
    b = pl.program_id(0)
    used = b < nu_ref[0]
    new_expert = jnp.logical_or(b == 0, be_ref[b] != be_ref[jnp.maximum(b - 1, 0)])

    @pl.when(jnp.logical_and(used, new_expert))
    def _():
        wgu16_ref[...] = wgu_ref[0].astype(BF16)
        wd16_ref[...] = wd_ref[0].astype(BF16)

    @pl.when(used)
    def _():
        x = x_ref[...].astype(BF16)
        gu = _dot(x, wgu16_ref[...]) + bgu_ref[0]
        d_ff = gu.shape[1] // 2
        gate = jnp.minimum(gu[:, :d_ff], SWIGLU_LIMIT)
        up = jnp.clip(gu[:, d_ff:], -SWIGLU_LIMIT, SWIGLU_LIMIT)
        act = (up + 1.0) * gate * jax.nn.sigmoid(SWIGLU_ALPHA * gate)
        o_ref[...] = _dot(act.astype(BF16), wd16_ref[...]) + bd_ref[0]

    @pl.when(jnp.logical_not(used))
    def _():
        o_ref[...] = jnp.zeros(o_ref.shape, F32)


def moe_experts(xs, block_e, n_used, w_gu, b_gu, w_down, b_down):
    rows, d = xs.shape
    nb = rows // MOE_ROWS
    e, _, gu_w = w_gu.shape
    last = lambda b, nu: jnp.minimum(b, nu[0] - 1)
    return pl.pallas_call(
        _moe_kernel,
        grid_spec=pltpu.PrefetchScalarGridSpec(
            num_scalar_prefetch=2,
            grid=(nb,),
            in_specs=[
                pl.BlockSpec((MOE_ROWS, d), lambda b, be, nu: (last(b, nu), 0)),
                pl.BlockSpec((1, d, gu_w), lambda b, be, nu: (be[last(b, nu)], 0, 0)),
                pl.BlockSpec((1, 1, gu_w), lambda b, be, nu: (be[last(b, nu)], 0, 0)),
                pl.BlockSpec((1, gu_w // 2, d), lambda b, be, nu: (be[last(b, nu)], 0, 0)),
                pl.BlockSpec((1, 1, d), lambda b, be, nu: (be[last(b, nu)], 0, 0)),
            ],
            out_specs=pl.BlockSpec((MOE_ROWS, d), lambda b, be, nu: (b, 0)),
            scratch_shapes=[pltpu.VMEM((d, gu_w), BF16), pltpu.VMEM((gu_w // 2, d), BF16)],
        ),
        out_shape=jax.ShapeDtypeStruct((rows, d), F32),
        compiler_params=_cparams(("arbitrary",)),
        name="moe_experts",
    )(block_e, n_used, xs, w_gu, b_gu.reshape(e, 1, gu_w), w_down, b_down.reshape(e, 1, d))


def _combine_kernel(dfirst_ref, dnext_ref, x2_ref, gate_ref, g_ref, ys_ref, o_ref, buf_ref, sem, *, n_tiles):
    i = pl.program_id(0)
    slot = i % 2

    def issue(dref, sl):
        def body(t, c):
            for k in range(TOP_K):
                d = dref[0, 0, t * TOP_K + k]
                pltpu.make_async_copy(ys_ref.at[pl.ds(d, 1)], buf_ref.at[sl, k, pl.ds(t, 1)], sem.at[sl]).start(
                    priority=k % 2)
            return c
        lax.fori_loop(0, ROUTE_TOK, body, 0, unroll=2)

    @pl.when(i == 0)
    def _():
        issue(dfirst_ref, 0)

    @pl.when(i + 1 < n_tiles)
    def _():
        issue(dnext_ref, 1 - slot)

    for k in range(TOP_K):
        pltpu.make_async_copy(ys_ref.at[pl.ds(0, ROUTE_TOK)], buf_ref.at[slot, k], sem.at[slot]).wait()

    gates = gate_ref[...]
    moe = buf_ref[slot, 0] * gates[:, 0:1]
    for k in range(1, TOP_K):
        moe = moe + buf_ref[slot, k] * gates[:, k:k + 1]
    tok = x2_ref[...] + moe
    y = tok * lax.rsqrt(jnp.mean(tok * tok, axis=-1, keepdims=True) + EPS)
    o_ref[...] = y * g_ref[...]


def combine(x2, ys, dest, gates, norm_final):
    t, d = x2.shape
    assert t % ROUTE_TOK == 0
    n_tiles = t // ROUTE_TOK
    dest3 = dest.reshape(n_tiles, 1, ROUTE_TOK * TOP_K)
    kern = functools.partial(_combine_kernel, n_tiles=n_tiles)
    smem = lambda imap: pl.BlockSpec((1, 1, ROUTE_TOK * TOP_K), imap, memory_space=pltpu.SMEM)
    return pl.pallas_call(
        kern,
        grid=(n_tiles,),
        in_specs=[
            smem(lambda i: (0, 0, 0)),
            smem(lambda i: (jnp.minimum(i + 1, n_tiles - 1), 0, 0)),
            pl.BlockSpec((ROUTE_TOK, d), lambda i: (i, 0)),
            pl.BlockSpec((ROUTE_TOK, ROUTE_W), lambda i: (i, 0)),
            pl.BlockSpec((1, d), lambda i: (0, 0)),
            pl.BlockSpec(memory_space=pl.ANY),
        ],
        out_specs=pl.BlockSpec((ROUTE_TOK, d), lambda i: (i, 0)),
        out_shape=jax.ShapeDtypeStruct((t, d), F32),
        scratch_shapes=[pltpu.VMEM((2, TOP_K, ROUTE_TOK, d), F32), pltpu.SemaphoreType.DMA((2,))],
        compiler_params=_cparams(("arbitrary",)),
        name="combine",
    )(dest3, dest3, x2, gates, norm_final.reshape(1, d), ys)


def kernel(x_prompt, x_sample, cache_k, cache_v, state_ssm, state_conv, norm_mix, w_in, conv_w, conv_b, dt_bias, a_log, d_skip, ssd_norm, w_o_ssd, lambda_q1, lambda_k1, lambda_q2, lambda_k2, subln, w_o_att, w_out, norm_ffn, router_w, router_b, w_gu, b_gu, w_down, b_down, norm_final):
    bp, s, d = x_prompt.shape
    bs, ts, _ = x_sample.shape
    past = cache_k.shape[2]
    depth = w_in.shape[0]
    assert depth == 1
    layer = 0
    lambda_init = 0.8 - 0.6 * math.exp(-0.3 * layer)
    tp, tsn = bp * s, bs * ts
    t_all = tp + tsn

    heads = ATT_N_HEADS
    xp2 = x_prompt.reshape(tp, d)
    xs2 = x_sample.reshape(tsn, d)

    sizes = (SSD_D_INNER, SSD_CONV_CH, SSD_N_HEADS, d, d, d, d, d)
    offs = [0]
    for sz in sizes:
        offs.append(offs[-1] + sz)
    w = w_in[layer]
    seg = lambda i: w[:, offs[i]:offs[i + 1]]
    dt_pad = jnp.zeros((d, PROJ_XD_W - SSD_CONV_CH - SSD_N_HEADS), F32)
    w_all = jnp.concatenate([seg(4), seg(5), seg(0), seg(3), seg(6), seg(7), seg(1), seg(2), dt_pad],
                            axis=1).astype(BF16)
    k_p, v_p, kv16_p, zq_p, xd_p = in_proj(xp2, norm_mix[layer], w_all, tm=1024)
    k_s, v_s, kv16_s, zq_s, xd_s = in_proj(xs2, norm_mix[layer], w_all, tm=tsn)

    prev_p = jnp.zeros((bp, SUBLANES, SSD_CONV_CH), F32)
    h0_p = jnp.zeros((bp, SSD_D_STATE, SSD_D_INNER), F32)
    ssd_w = (conv_w[layer], conv_b[layer], dt_bias[layer], a_log[layer], d_skip[layer], ssd_norm[layer])
    y_p, hT_p = ssd_mixer(xd_p, zq_p, 0, prev_p, h0_p, *ssd_w, b=bp, l=s, tl=256, valid_len=None, out_rows=tp)

    pad_rows = CHUNK - ts
    pad_seq = lambda a: jnp.pad(a.reshape(bs, ts, -1), ((0, 0), (0, pad_rows), (0, 0))).reshape(bs * CHUNK, -1)
    prev_s = jnp.pad(state_conv[layer], ((0, 0), (SUBLANES - (SSD_CONV_WIDTH - 1), 0), (0, 0)))
    h0_s = jnp.swapaxes(state_ssm[layer].reshape(bs, SSD_D_INNER, SSD_D_STATE), 1, 2)
    y_s, hT_s = ssd_mixer(pad_seq(xd_s), pad_seq(zq_s[:, :SSD_D_INNER]), 0, prev_s, h0_s, *ssd_w, b=bs, l=CHUNK,
                          tl=CHUNK, valid_len=ts, out_rows=bs * CHUNK)
    y_s = y_s.reshape(bs, CHUNK, -1)[:, :ts].reshape(tsn, -1)

    lam_rows = jnp.concatenate([lambda_q1[layer][None], lambda_k1[layer][None], lambda_q2[layer][None],
                                lambda_k2[layer][None], jnp.zeros((4, ATT_HEAD_DIM), F32)], axis=0)
    o_p = diff_attention_t(zq_p, 0, SSD_D_INNER // d, kv16_p, 0, kv16_p, 1, lam_rows, subln[layer], b=bp, lq=s,
                           lk=s, tq=512, tk=1024, q_off=0, kv_len=s, lambda_init=lambda_init, out_rows=tp)
    o_s = diff_attention_cached(zq_s.reshape(bs, ts, -1), SSD_D_INNER // d, cache_k[layer], cache_v[layer],
                                kv16_s.reshape(bs, ts, 2 * d), lam_rows, subln[layer], tkc=1024,
                                lambda_init=lambda_init).reshape(tsn, d)

    rw = jnp.zeros((d, LANES), F32).at[:, :N_EXPERTS].set(router_w[layer])
    rb = jnp.zeros((1, LANES), F32).at[0, :N_EXPERTS].set(router_b[layer])
    mix_w = (w_o_ssd[layer].astype(BF16), w_o_att[layer].astype(BF16), w_out[layer].astype(BF16), norm_ffn[layer],
             rw, rb)
    x2_p, hn_p, gate_p, sel_p, cnt_p = mixer_out(y_p, o_p, zq_p, 3, 4, xp2, *mix_w,
                                                 jnp.zeros((1, LANES), F32), tm=512)
    x2_s, hn_s, gate_s, sel_s, cnt_all = mixer_out(y_s, o_s, zq_s, 3, 4, xs2, *mix_w, cnt_p, tm=tsn)

    sel = jnp.concatenate([sel_p, sel_s], axis=0)
    top_e = sel[:, :TOP_K]
    rank = sel[:, TOP_K:]
    cnt = cnt_all[0, :N_EXPERTS].astype(jnp.int32)
    padded = (cnt + MOE_ROWS - 1) // MOE_ROWS * MOE_ROWS
    ends = jnp.cumsum(padded)
    starts = ends - padded
    dest = (starts[top_e] + rank).reshape(-1)
    nb = (t_all * TOP_K + N_EXPERTS * (MOE_ROWS - 1) + MOE_ROWS - 1) // MOE_ROWS
    block_start = jnp.arange(nb, dtype=jnp.int32) * MOE_ROWS
    block_e = jnp.minimum(jnp.sum((ends[None, :] <= block_start[:, None]).astype(jnp.int32), axis=1),
                          N_EXPERTS - 1)
    n_used = (ends[-1] // MOE_ROWS).astype(jnp.int32).reshape(1)

    xs_sorted = moe_dispatch(dest, hn_p, hn_s, starts + cnt, padded - cnt, n_used, nb)
    ys = moe_experts(xs_sorted, block_e, n_used, w_gu[layer], b_gu[layer], w_down[layer], b_down[layer])
    y_prompt = combine(x2_p, ys, dest[:tp * TOP_K], gate_p, norm_final).reshape(bp, s, d)
    y_sample = combine(x2_s, ys, dest[tp * TOP_K:], gate_s, norm_final).reshape(bs, ts, d)

    new_k_p = k_p.reshape(1, bp, s, heads, 2 * ATT_HEAD_DIM)
    new_v_p = v_p.reshape(1, bp, s, heads, ATT_V_HEAD)
    new_k_s = k_s.reshape(1, bs, ts, heads, 2 * ATT_HEAD_DIM)
    new_v_s = v_s.reshape(1, bs, ts, heads, ATT_V_HEAD)
    ssm_p = jnp.swapaxes(hT_p, 1, 2).reshape(1, bp, SSD_N_HEADS, SSD_HEAD_DIM, SSD_D_STATE)
    ssm_s = jnp.swapaxes(hT_s, 1, 2).reshape(1, bs, SSD_N_HEADS, SSD_HEAD_DIM, SSD_D_STATE)
    keep = SSD_CONV_WIDTH - 1
    conv_p = xd_p.reshape(bp, s, -1)[:, s - keep:, :SSD_CONV_CH][None]
    raw_s = jnp.concatenate([state_conv[layer], xd_s.reshape(bs, ts, -1)[:, :, :SSD_CONV_CH]], axis=1)
    conv_s = raw_s[:, -keep:][None]
    return (y_prompt, y_sample, new_k_p, new_v_p, ssm_p, conv_p, new_k_s, new_v_s, ssm_s, conv_s)
```

```python
import functools
import math

import jax
import jax.numpy as jnp
from jax import lax
from jax.experimental import pallas as pl
from jax.experimental.pallas import tpu as pltpu

EPS = 1e-5
CHUNK = 64
D_MODEL = 1024
SSD_D_INNER = 2048
SSD_HEAD_DIM = 64
SSD_N_HEADS = 32
SSD_GROUPS = 4
SSD_D_STATE = 128
SSD_CONV_WIDTH = 4
SSD_CONV_CH = 3072
ATT_HEAD_DIM = 64
ATT_N_HEADS = 8
ATT_V_HEAD = 128
ALIBI_MAX_BIAS = 8.0
N_EXPERTS = 32
TOP_K = 4
SWIGLU_LIMIT = 7.0
SWIGLU_ALPHA = 1.702

LANES = 128
SUBLANES = 8
VMEM_LIMIT = 56 * 1024 * 1024
NEG_BIG = -1e30

BF16 = jnp.bfloat16
F32 = jnp.float32


def _cparams(sem, flags=None):
    return pltpu.CompilerParams(dimension_semantics=sem, vmem_limit_bytes=VMEM_LIMIT, flags=flags)


def _split3(x):
    h1 = x.astype(BF16)
    r1 = x - h1.astype(F32)
    h2 = r1.astype(BF16)
    h3 = (r1 - h2.astype(F32)).astype(BF16)
    return h1, h2, h3


def _dot(a, b):
    return jnp.dot(a, b, preferred_element_type=F32)


def _dot_exact_rhs(x, m):
    h1, h2, h3 = _split3(x)
    return _dot(h1, m) + _dot(h2, m) + _dot(h3, m)


def _dot_exact_lhs(m, x):
    h1, h2, h3 = _split3(x)
    return _dot(m, h1) + _dot(m, h2) + _dot(m, h3)


PROJ_TN = 256
PROJ_K = (0, D_MODEL // PROJ_TN)
PROJ_V = (PROJ_K[1], PROJ_K[1] + D_MODEL // PROJ_TN)
PROJ_ZQ_W = SSD_D_INNER + 3 * D_MODEL
PROJ_ZQ = (PROJ_V[1], PROJ_V[1] + PROJ_ZQ_W // PROJ_TN)
PROJ_XD_W = -(-(SSD_CONV_CH + SSD_N_HEADS) // PROJ_TN) * PROJ_TN
PROJ_XD = (PROJ_ZQ[1], PROJ_ZQ[1] + PROJ_XD_W // PROJ_TN)


def _in_proj_kernel(x_ref, g_ref, w_ref, k_ref, v_ref, kv16_ref, zq_ref, xd_ref, xn_ref):
    j = pl.program_id(1)

    @pl.when(j == 0)
    def _():
        x = x_ref[...]
        y = x * lax.rsqrt(jnp.mean(x * x, axis=-1, keepdims=True) + EPS)
        xn_ref[...] = (y * g_ref[...]).astype(BF16)

    @pl.when(j < PROJ_K[1])
    def _():
        acc = _dot(xn_ref[...], w_ref[...])
        k_ref[...] = acc
        kv16_ref[...] = acc.astype(BF16)

    @pl.when(jnp.logical_and(j >= PROJ_V[0], j < PROJ_V[1]))
    def _():
        acc = _dot(xn_ref[...], w_ref[...])
        v_ref[...] = acc
        kv16_ref[...] = acc.astype(BF16)

    @pl.when(jnp.logical_and(j >= PROJ_ZQ[0], j < PROJ_ZQ[1]))
    def _():
        zq_ref[...] = _dot(xn_ref[...], w_ref[...]).astype(BF16)

    @pl.when(j >= PROJ_XD[0])
    def _():
        xd_ref[...] = _dot(xn_ref[...], w_ref[...])


def in_proj(x, gain, w, tm):
    t, d = x.shape
    assert t % tm == 0 and w.shape[1] == PROJ_XD[1] * PROJ_TN
    tn = PROJ_TN

    def out_map(rng):
        return lambda i, j: (i, jnp.clip(j - rng[0], 0, rng[1] - rng[0] - 1))

    return pl.pallas_call(
        _in_proj_kernel,
        grid=(t // tm, PROJ_XD[1]),
        in_specs=[
            pl.BlockSpec((tm, d), lambda i, j: (i, 0)),
            pl.BlockSpec((1, d), lambda i, j: (0, 0)),
            pl.BlockSpec((d, tn), lambda i, j: (0, j)),
        ],
        out_specs=[
            pl.BlockSpec((tm, tn), out_map(PROJ_K)),
            pl.BlockSpec((tm, tn), out_map(PROJ_V)),
            pl.BlockSpec((tm, tn), out_map((PROJ_K[0], PROJ_V[1]))),
            pl.BlockSpec((tm, tn), out_map(PROJ_ZQ)),
            pl.BlockSpec((tm, tn), out_map(PROJ_XD)),
        ],
        out_shape=[
            jax.ShapeDtypeStruct((t, d), F32),
            jax.ShapeDtypeStruct((t, d), F32),
            jax.ShapeDtypeStruct((t, 2 * d), BF16),
            jax.ShapeDtypeStruct((t, PROJ_ZQ_W), BF16),
            jax.ShapeDtypeStruct((t, PROJ_XD_W), F32),
        ],
        scratch_shapes=[pltpu.VMEM((tm, d), BF16)],
        compiler_params=_cparams(("arbitrary", "arbitrary")),
        name="in_proj",
    )(x, gain.reshape(1, d), w)


CONV_ROWS, CONV_COLS = 64, 512


def _ssd_kernel(xbc_ref, z_ref, prev_ref, h0_ref, cw_ref, cb_ref, dtb_ref, alog_ref, dsk_ref, gn_ref,
                y_ref, hout_ref, xpad_ref, xc_ref, tail_ref, st_ref, *, tl, valid_len):
    t = pl.program_id(1)
    nt = pl.num_programs(1)
    nch = tl // CHUNK
    dinner = SSD_D_INNER
    gw = dinner // SSD_GROUPS
    ns = SSD_D_STATE

    @pl.when(t == 0)
    def _():
        tail_ref[...] = prev_ref[0]
        st_ref[...] = h0_ref[0]

    xpad_ref[0:SUBLANES, :] = tail_ref[...]
    xpad_ref[SUBLANES:SUBLANES + tl, :] = xbc_ref[:, 0:SSD_CONV_CH]
    tail_ref[...] = xbc_ref[tl - SUBLANES:tl, 0:SSD_CONV_CH]
    for r in range(0, tl, CONV_ROWS):
        for c in range(0, SSD_CONV_CH, CONV_COLS):
            acc = cb_ref[:, c:c + CONV_COLS]
            for k in range(SSD_CONV_WIDTH):
                off = r + SUBLANES - (SSD_CONV_WIDTH - 1) + k
                acc = acc + xpad_ref[off:off + CONV_ROWS, c:c + CONV_COLS] * cw_ref[k:k + 1, c:c + CONV_COLS]
            xc_ref[r:r + CONV_ROWS, c:c + CONV_COLS] = acc * jax.nn.sigmoid(acc)

    head_of_lane = lax.broadcasted_iota(jnp.int32, (SSD_N_HEADS, dinner), 1) // SSD_HEAD_DIM
    expand = (head_of_lane == lax.broadcasted_iota(jnp.int32, (SSD_N_HEADS, dinner), 0)).astype(BF16)
    ti = lax.broadcasted_iota(jnp.int32, (CHUNK, CHUNK), 0)
    si = lax.broadcasted_iota(jnp.int32, (CHUNK, CHUNK), 1)
    tril = (si <= ti).astype(BF16)
    row_c = lax.broadcasted_iota(jnp.int32, (CHUNK, dinner), 0)
    pos_in_head = lax.broadcasted_iota(jnp.int32, (CHUNK, dinner), 1) % CHUNK
    upper = row_c <= pos_in_head
    row_p = lax.broadcasted_iota(jnp.int32, (CHUNK, LANES), 0)
    lane_p = lax.broadcasted_iota(jnp.int32, (CHUNK, LANES), 1)
    causal_pair = (lane_p % CHUNK) <= row_p
    left_half = lane_p < SSD_HEAD_DIM
    a_neg_e = _dot_exact_rhs(-jnp.exp(alog_ref[...]), expand)
    dsk_e = _dot_exact_rhs(dsk_ref[...], expand)

    def chunk_body(c, carry):
        r0 = pl.multiple_of(c * CHUNK, CHUNK)
        xs = xc_ref[pl.ds(r0, CHUNK), 0:dinner]
        dt_raw = xbc_ref[pl.ds(r0, CHUNK), SSD_CONV_CH:SSD_CONV_CH + SSD_N_HEADS]
        dtv = dt_raw + dtb_ref[...]
        dt = jnp.maximum(dtv, 0.0) + jnp.log1p(jnp.exp(-jnp.abs(dtv)))
        if valid_len is not None:
            rows = t * tl + r0 + lax.broadcasted_iota(jnp.int32, (CHUNK, SSD_N_HEADS), 0)
            dt = jnp.where(rows < valid_len, dt, 0.0)
        dt_e = _dot_exact_rhs(dt, expand)
        a_e = dt_e * a_neg_e
        acs_e = _dot_exact_lhs(tril, a_e)
        rowterm = jnp.sum(jnp.where(upper, a_e, 0.0), axis=0, keepdims=True)
        acs_last = acs_e[CHUNK - 1:CHUNK, :]
        xdt = xs * dt_e
        x_dec = (xdt * jnp.exp(acs_last - acs_e)).astype(BF16)
        e_acs = jnp.exp(acs_e)
        e_last = jnp.exp(acs_last)

        y_parts = []
        for g in range(SSD_GROUPS):
            lo = g * gw
            bm = xc_ref[pl.ds(r0, CHUNK), dinner + g * ns:dinner + (g + 1) * ns].astype(BF16)
            cm = xc_ref[pl.ds(r0, CHUNK), dinner + SSD_GROUPS * ns + g * ns:
                        dinner + SSD_GROUPS * ns + (g + 1) * ns].astype(BF16)
            b2 = jnp.concatenate([bm, bm], axis=0)
            cb2 = lax.dot_general(cm, b2, (((1,), (1,)), ((), ())), preferred_element_type=F32)
            st_g = st_ref[:, lo:lo + gw]
            y_off = _dot(cm, st_g.astype(BF16)) * e_acs[:, lo:lo + gw]
            pieces = []
            for j in range(gw // LANES):
                l0 = lo + j * LANES
                diff = acs_e[:, l0:l0 + LANES] - rowterm[:, l0:l0 + LANES]
                dec = jnp.where(causal_pair, jnp.exp(diff), 0.0)
                scores = (cb2 * dec).astype(BF16)
                xp = xdt[:, l0:l0 + LANES]
                xblk = jnp.concatenate([jnp.where(left_half, xp, 0.0), jnp.where(left_half, 0.0, xp)],
                                       axis=0).astype(BF16)
                pieces.append(_dot(scores, xblk))
            y_diag = jnp.concatenate(pieces, axis=1)
            upd = lax.dot_general(bm, x_dec[:, lo:lo + gw], (((0,), (0,)), ((), ())),
                                  preferred_element_type=F32)
            st_ref[:, lo:lo + gw] = e_last[:, lo:lo + gw] * st_g + upd
            yg = y_diag + y_off + xs[:, lo:lo + gw] * dsk_e[:, lo:lo + gw]
            zg = z_ref[pl.ds(r0, CHUNK), lo:lo + gw].astype(F32)
            yg = yg * (zg * jax.nn.sigmoid(zg))
            yn = yg * lax.rsqrt(jnp.mean(yg * yg, axis=-1, keepdims=True) + EPS)
            y_parts.append((yn * gn_ref[:, lo:lo + gw]).astype(y_ref.dtype))
        y_ref[pl.ds(r0, CHUNK), :] = jnp.concatenate(y_parts, axis=1)
        return carry

    lax.fori_loop(0, nch, chunk_body, 0)

    @pl.when(t == nt - 1)
    def _():
        hout_ref[0] = st_ref[...]


def ssd_mixer(xbc, z_src, z_col, conv_prev8, h0_t, conv_w, conv_b, dt_bias, a_log, d_skip, ssd_norm,
              b, l, tl, valid_len, out_rows):
    wx = xbc.shape[1]
    assert l % tl == 0 and tl % CHUNK == 0 and CHUNK == SSD_HEAD_DIM
    nt = l // tl
    kern = functools.partial(_ssd_kernel, tl=tl, valid_len=valid_len)
    full = lambda shape: pl.BlockSpec(shape, lambda i, j: (0,) * len(shape))
    return pl.pallas_call(
        kern,
        grid=(b, nt),
        in_specs=[
            pl.BlockSpec((tl, wx), lambda i, j: (i * nt + j, 0)),
            pl.BlockSpec((tl, SSD_D_INNER), lambda i, j: (i * nt + j, z_col)),
            pl.BlockSpec((1, SUBLANES, SSD_CONV_CH), lambda i, j: (i, 0, 0)),
            pl.BlockSpec((1, SSD_D_STATE, SSD_D_INNER), lambda i, j: (i, 0, 0)),
            full((SSD_CONV_WIDTH, SSD_CONV_CH)),
            full((1, SSD_CONV_CH)),
            full((1, SSD_N_HEADS)),
            full((1, SSD_N_HEADS)),
            full((1, SSD_N_HEADS)),
            full((1, SSD_D_INNER)),
        ],
        out_specs=[
            pl.BlockSpec((tl, SSD_D_INNER), lambda i, j: (i * nt + j, 0)),
            pl.BlockSpec((1, SSD_D_STATE, SSD_D_INNER), lambda i, j: (i, 0, 0)),
        ],
        out_shape=[
            jax.ShapeDtypeStruct((out_rows, SSD_D_INNER), BF16),
            jax.ShapeDtypeStruct((b, SSD_D_STATE, SSD_D_INNER), F32),
        ],
        scratch_shapes=[
            pltpu.VMEM((tl + SUBLANES, SSD_CONV_CH), F32),
            pltpu.VMEM((tl, SSD_CONV_CH), F32),
            pltpu.VMEM((SUBLANES, SSD_CONV_CH), F32),
            pltpu.VMEM((SSD_D_STATE, SSD_D_INNER), F32),
        ],
        compiler_params=_cparams(("arbitrary", "arbitrary")),
        name="ssd_mixer",
    )(xbc, z_src, conv_prev8, h0_t, conv_w, conv_b.reshape(1, -1), dt_bias.reshape(1, -1),
      a_log.reshape(1, -1), d_skip.reshape(1, -1), ssd_norm.reshape(1, -1))


BIAS_SPLIT = 32
LOG2E = 1.4426950408889634
LOG2E_PARTS = (1.4453125, -0.00262451171875, 7.063150405883789e-06)
FLAG_FIRST, FLAG_LAST, VARIANT_SHIFT = 1, 2, 2
ATT_SUB = 256
MODE_PLAIN, MODE_MASKED, MODE_SKIP = "plain", "masked", "skip"


def _attn_sub(tk):
    nsub = tk // ATT_SUB if tk % ATT_SUB == 0 else 1
    return nsub, tk // nsub


def _attn_t_kernel(qi_ref, ki_ref, fl_ref, q_ref, k_ref, v_ref, slope_ref, lam_ref, sub_ref, o_ref,
                   m_ref, l_ref, acc_ref, kb_ref, corr_ref, s0_ref, s1_ref, mt0_ref, mt1_ref,
                   *, tq, tk, q_off, kv_len, lambda_init, variants):
    p_idx = pl.program_id(1)
    qi = qi_ref[p_idx]
    ki = ki_ref[p_idx]
    flags = fl_ref[p_idx]
    qstart = q_off + qi * tq
    kstart = ki * tk
    half = ATT_HEAD_DIM
    scale = ATT_HEAD_DIM ** -0.5

    @pl.when((flags & FLAG_FIRST) != 0)
    def _():
        m_ref[...] = jnp.full(m_ref.shape, NEG_BIG, F32)
        l_ref[...] = jnp.zeros(l_ref.shape, F32)
        acc_ref[...] = jnp.zeros(acc_ref.shape, F32)

    lane_k = lax.broadcasted_iota(jnp.int32, (tk, LANES), 1)
    rel = kstart - qstart + lax.broadcasted_iota(jnp.int32, (tk, LANES), 0)
    hi = (rel // BIAS_SPLIT) * BIAS_SPLIT
    lo = rel - hi
    pos_lane = lane_k % half
    n_parts = len(LOG2E_PARTS)
    kb_ref[...] = jnp.where(pos_lane < n_parts, hi, jnp.where(pos_lane < 2 * n_parts, lo, 0)).astype(F32).astype(BF16)

    variant = flags >> VARIANT_SHIFT
    nsub, ts = _attn_sub(tk)

    def mask_terms(modes):
        for j, mode in enumerate(modes):
            if mode != MODE_MASKED:
                continue
            r0 = j * ts
            kpos = kstart + r0 + lax.broadcasted_iota(jnp.int32, (ts, tq), 0)
            qpos = qstart + lax.broadcasted_iota(jnp.int32, (ts, tq), 1)
            allowed = jnp.logical_and(kpos // CHUNK <= qpos // CHUNK, kpos < kv_len)
            corr_ref[0, r0:r0 + ts, :] = jnp.where(kpos > qpos, (2.0 * LOG2E) * (qpos - kpos).astype(F32), 0.0)
            corr_ref[1, r0:r0 + ts, :] = jnp.where(allowed, 0.0, NEG_BIG)

    lane_q = lax.broadcasted_iota(jnp.int32, (tq, LANES), 1)
    lane_ks = lax.broadcasted_iota(jnp.int32, (ts, LANES), 1)
    part = (lax.broadcasted_iota(jnp.int32, (1, LANES), 1) % half) % n_parts
    log2e_lanes = jnp.where(part == 0, LOG2E_PARTS[0], jnp.where(part == 1, LOG2E_PARTS[1], LOG2E_PARTS[2]))

    def score_pass(h, s_ref, mt_ref, modes):
        c0 = pl.multiple_of(h * LANES, LANES)
        q = (q_ref[:, pl.ds(c0, LANES)].astype(F32) * (scale * LOG2E)).astype(BF16)
        slope = slope_ref[pl.ds(h, 1), :]
        slope_b = jnp.broadcast_to((slope * log2e_lanes).astype(BF16), (tq, LANES))
        zero_q = jnp.zeros((tq, LANES), BF16)
        for idx in range(2):
            own = (lane_q < half) if idx == 0 else (lane_q >= half)
            q_aug = jnp.where(own, q, jnp.where((lane_q % half) < 2 * n_parts, slope_b, zero_q))
            own_k = (lane_ks < half) if idx == 0 else (lane_ks >= half)
            mt = None
            for j, mode in enumerate(modes):
                if mode == MODE_SKIP:
                    continue
                r0 = j * ts
                k_aug = jnp.where(own_k, k_ref[r0:r0 + ts, pl.ds(c0, LANES)], kb_ref[r0:r0 + ts, :])
                s = lax.dot_general(k_aug, q_aug, (((1,), (1,)), ((), ())), preferred_element_type=F32)
                if mode == MODE_MASKED:
                    s = s + slope[:, 0:1] * corr_ref[0, r0:r0 + ts, :] + corr_ref[1, r0:r0 + ts, :]
                s_ref[idx, r0:r0 + ts, :] = s
                mj = jnp.max(s, axis=0, keepdims=True)
                mt = mj if mt is None else jnp.maximum(mt, mj)
                yield
            mt_ref[idx] = mt

    def value_pass(h, s_ref, mt_ref, modes):
        c0 = pl.multiple_of(h * LANES, LANES)
        for idx in range(2):
            m_prev = m_ref[idx, h]
            m_new = jnp.maximum(m_prev, mt_ref[idx])
            alpha = jnp.exp2(m_prev - m_new)
            lsum, pv = None, None
            for j, mode in enumerate(modes):
                if mode == MODE_SKIP:
                    continue
                r0 = j * ts
                p = jnp.exp2(s_ref[idx, r0:r0 + ts, :] - m_new)
                lj = jnp.sum(p, axis=0, keepdims=True)
                pj = lax.dot_general(v_ref[r0:r0 + ts, pl.ds(c0, LANES)], p.astype(BF16), (((0,), (0,)), ((), ())),
                                     preferred_element_type=F32)
                lsum = lj if lsum is None else lsum + lj
                pv = pj if pv is None else pv + pj
                yield
            l_ref[idx, h] = alpha * l_ref[idx, h] + lsum
            acc_ref[idx, h] = alpha * acc_ref[idx, h] + pv
            m_ref[idx, h] = m_new

    def run(*gens):
        live = list(gens)
        while live:
            for g in list(live):
                try:
                    next(g)
                except StopIteration:
                    live.remove(g)

    def all_heads(modes):
        bufs = ((s0_ref, mt0_ref), (s1_ref, mt1_ref))
        run(score_pass(0, *bufs[0], modes))

        def pair(g, c):
            h = 2 * g
            run(score_pass(h + 1, *bufs[1], modes), value_pass(h, *bufs[0], modes))
            run(score_pass(h + 2, *bufs[0], modes), value_pass(h + 1, *bufs[1], modes))
            return c

        lax.fori_loop(0, ATT_N_HEADS // 2 - 1, pair, 0)
        last = ATT_N_HEADS - 1
        run(score_pass(last, *bufs[1], modes), value_pass(last - 1, *bufs[0], modes))
        run(value_pass(last, *bufs[1], modes))

    for vi, modes in enumerate(variants):
        @pl.when(variant == vi)
        def _(modes=modes):
            mask_terms(modes)
            all_heads(modes)

    @pl.when((flags & FLAG_LAST) != 0)
    def _():
        lp = lam_ref[...]
        lam = (jnp.exp(jnp.sum(lp[0:1] * lp[1:2], axis=-1, keepdims=True))
               - jnp.exp(jnp.sum(lp[2:3] * lp[3:4], axis=-1, keepdims=True)) + lambda_init)

        def fin(h, c):
            c0 = pl.multiple_of(h * LANES, LANES)
            o = acc_ref[0, h] / l_ref[0, h] - lam * (acc_ref[1, h] / l_ref[1, h])
            on = o * lax.rsqrt(jnp.mean(o * o, axis=0, keepdims=True) + EPS)
            on = (on * sub_ref[...]) * (1.0 - lambda_init)
            o_ref[:, pl.ds(c0, LANES)] = on.T.astype(o_ref.dtype)
            return c

        lax.fori_loop(0, ATT_N_HEADS, fin, 0)


def _attn_pairs(nq, nk, tq, tk, q_off, kv_len):
    nsub, ts = _attn_sub(tk)
    qis, kis, fls, variants = [], [], [], []
    for qi in range(nq):
        first_q = q_off + qi * tq
        last_q = first_q + tq - 1
        kend = min((last_q // CHUNK + 1) * CHUNK, kv_len)
        nkv = -(-kend // tk)
        for ki in range(nkv):
            modes = []
            for j in range(nsub):
                ks = ki * tk + j * ts
                if ks >= kend:
                    modes.append(MODE_SKIP)
                elif ks + ts <= (first_q // CHUNK) * CHUNK and ks + ts <= kv_len:
                    modes.append(MODE_PLAIN)
                else:
                    modes.append(MODE_MASKED)
            modes = tuple(modes)
            if modes not in variants:
                variants.append(modes)
            fl = ((FLAG_FIRST if ki == 0 else 0) | (FLAG_LAST if ki == nkv - 1 else 0)
                  | (variants.index(modes) << VARIANT_SHIFT))
            qis.append(qi)
            kis.append(ki)
            fls.append(fl)
    return qis, kis, fls, tuple(variants)


def diff_attention_t(q_src, q_blk0, q_colblk, k_src, k_colblk, v_src, v_colblk, lam_rows, subln, *, b, lq, lk,
                     tq, tk, q_off, kv_len, lambda_init, out_rows):
    assert lq % tq == 0 and lk % tk == 0 and tq % LANES == 0
    assert q_off + lq <= 256 * BIAS_SPLIT + tq
    nq, nk = lq // tq, lk // tk
    width = ATT_N_HEADS * LANES
    qis, kis, fls, variants = _attn_pairs(nq, nk, tq, tk, q_off, kv_len)
    kern = functools.partial(_attn_t_kernel, tq=tq, tk=tk, q_off=q_off, kv_len=kv_len, lambda_init=lambda_init,
                             variants=variants)
    slopes = jnp.exp2(-ALIBI_MAX_BIAS * jnp.arange(1, ATT_N_HEADS + 1, dtype=F32) / ATT_N_HEADS)
    slopes = jnp.broadcast_to(slopes[:, None], (ATT_N_HEADS, LANES))
    const = lambda shape: pl.BlockSpec(shape, lambda bi, p, qt, kt, ft: (0,) * len(shape))
    return pl.pallas_call(
        kern,
        grid_spec=pltpu.PrefetchScalarGridSpec(
            num_scalar_prefetch=3,
            grid=(b, len(qis)),
            in_specs=[
                pl.BlockSpec((tq, width), lambda bi, p, qt, kt, ft: (q_blk0 + bi * nq + qt[p], q_colblk)),
                pl.BlockSpec((tk, width), lambda bi, p, qt, kt, ft: (bi * nk + kt[p], k_colblk)),
                pl.BlockSpec((tk, width), lambda bi, p, qt, kt, ft: (bi * nk + kt[p], v_colblk)),
                const((ATT_N_HEADS, LANES)),
                const((SUBLANES, ATT_HEAD_DIM)),
                const((ATT_V_HEAD, 1)),
            ],
            out_specs=pl.BlockSpec((tq, width), lambda bi, p, qt, kt, ft: (bi * nq + qt[p], 0)),
            scratch_shapes=[
                pltpu.VMEM((2, ATT_N_HEADS, 1, tq), F32),
                pltpu.VMEM((2, ATT_N_HEADS, 1, tq), F32),
                pltpu.VMEM((2, ATT_N_HEADS, ATT_V_HEAD, tq), F32),
                pltpu.VMEM((tk, LANES), BF16),
                pltpu.VMEM((2, tk, tq), F32),
                pltpu.VMEM((2, tk, tq), F32),
                pltpu.VMEM((2, tk, tq), F32),
                pltpu.VMEM((2, 1, tq), F32),
                pltpu.VMEM((2, 1, tq), F32),
            ],
        ),
        out_shape=jax.ShapeDtypeStruct((out_rows, width), BF16),
        compiler_params=_cparams(("arbitrary", "arbitrary")),
        name="diff_attention_t",
    )(jnp.asarray(qis, jnp.int32), jnp.asarray(kis, jnp.int32), jnp.asarray(fls, jnp.int32),
      q_src, k_src, v_src, slopes, lam_rows, subln.reshape(-1, 1))


def _attn_cache_kernel(q_ref, ck_ref, cv_ref, kn_ref, vn_ref, slope_ref, lam_ref, sub_ref, o_ref,
                       m_ref, l_ref, acc_ref, *, tq, tkc, ncache, past, n_new, lambda_init):
    j = pl.program_id(1)
    half = ATT_HEAD_DIM
    n_parts = len(LOG2E_PARTS)
    scale = ATT_HEAD_DIM ** -0.5

    @pl.when(j == 0)
    def _():
        m_ref[...] = jnp.full(m_ref.shape, NEG_BIG, F32)
        l_ref[...] = jnp.zeros(l_ref.shape, F32)
        acc_ref[...] = jnp.zeros(acc_ref.shape, F32)

    lane_q = lax.broadcasted_iota(jnp.int32, (tq, LANES), 1)
    part = (lax.broadcasted_iota(jnp.int32, (1, LANES), 1) % half) % n_parts
    log2e_lanes = jnp.where(part == 0, LOG2E_PARTS[0], jnp.where(part == 1, LOG2E_PARTS[1], LOG2E_PARTS[2]))

    def position_lanes(rows, kstart):
        lane = lax.broadcasted_iota(jnp.int32, (rows, LANES), 1)
        rel = kstart - past + lax.broadcasted_iota(jnp.int32, (rows, LANES), 0)
        hi = (rel // BIAS_SPLIT) * BIAS_SPLIT
        pos = lane % half
        kb = jnp.where(pos < n_parts, hi, jnp.where(pos < 2 * n_parts, rel - hi, 0)).astype(F32).astype(BF16)
        return kb, lane

    def head(h, k, v, kb, lane_k, corr):
        q = (q_ref[0, :, h * LANES:(h + 1) * LANES].astype(F32) * (scale * LOG2E)).astype(BF16)
        slope = slope_ref[h:h + 1, :]
        slope_b = jnp.broadcast_to((slope * log2e_lanes).astype(BF16), (tq, LANES))
        for idx in range(2):
            own = (lane_q < half) if idx == 0 else (lane_q >= half)
            q_aug = jnp.where(own, q, jnp.where((lane_q % half) < 2 * n_parts, slope_b, jnp.zeros_like(q)))
            own_k = (lane_k < half) if idx == 0 else (lane_k >= half)
            s = lax.dot_general(q_aug, jnp.where(own_k, k, kb), (((1,), (1,)), ((), ())),
                                preferred_element_type=F32)
            if corr is not None:
                s = s + slope[:, 0:1] * corr[0] + corr[1]
            m_prev = m_ref[idx, h]
            m_new = jnp.maximum(m_prev, jnp.max(s, axis=1, keepdims=True))
            alpha = jnp.exp2(m_prev - m_new)
            p = jnp.exp2(s - m_new)
            l_ref[idx, h] = alpha * l_ref[idx, h] + jnp.sum(p, axis=1, keepdims=True)
            acc_ref[idx, h] = alpha * acc_ref[idx, h] + _dot(p.astype(BF16), v)
            m_ref[idx, h] = m_new

    @pl.when(j < ncache)
    def _():
        kb, lane_k = position_lanes(tkc, j * tkc)
        k_heads = pltpu.einshape("khd->hkd", ck_ref[0])
        v_heads = pltpu.einshape("khd->hkd", cv_ref[0])
        for h in range(ATT_N_HEADS):
            head(h, k_heads[h].astype(BF16), v_heads[h].astype(BF16), kb, lane_k, None)

    @pl.when(j == ncache)
    def _():
        kb, lane_k = position_lanes(n_new, past)
        kpos = past + lax.broadcasted_iota(jnp.int32, (tq, n_new), 1)
        qpos = past + lax.broadcasted_iota(jnp.int32, (tq, n_new), 0)
        corr = (jnp.where(kpos > qpos, (2.0 * LOG2E) * (qpos - kpos).astype(F32), 0.0),
                jnp.where(kpos // CHUNK <= qpos // CHUNK, 0.0, NEG_BIG))
        lp = lam_ref[...]
        lam = (jnp.exp(jnp.sum(lp[0:1] * lp[1:2], axis=-1, keepdims=True))
               - jnp.exp(jnp.sum(lp[2:3] * lp[3:4], axis=-1, keepdims=True)) + lambda_init)
        for h in range(ATT_N_HEADS):
            cols = slice(h * LANES, (h + 1) * LANES)
            head(h, kn_ref[0, :, cols], vn_ref[0, :, cols], kb, lane_k, corr)
            o = acc_ref[0, h] / l_ref[0, h] - lam * (acc_ref[1, h] / l_ref[1, h])
            on = o * lax.rsqrt(jnp.mean(o * o, axis=1, keepdims=True) + EPS)
            on = (on * sub_ref[...]) * (1.0 - lambda_init)
            o_ref[0, :, cols] = on.astype(o_ref.dtype)


def diff_attention_cached(q_src, q_colblk, cache_k, cache_v, kv_new, lam_rows, subln, *, tkc, lambda_init):
    b, tq, _ = q_src.shape
    width = ATT_N_HEADS * LANES
    past = cache_k.shape[1]
    n_new = kv_new.shape[1]
    assert past % tkc == 0 and past % CHUNK == 0 and past <= 256 * BIAS_SPLIT
    ncache = past // tkc
    kern = functools.partial(_attn_cache_kernel, tq=tq, tkc=tkc, ncache=ncache, past=past, n_new=n_new,
                             lambda_init=lambda_init)
    slopes = jnp.exp2(-ALIBI_MAX_BIAS * jnp.arange(1, ATT_N_HEADS + 1, dtype=F32) / ATT_N_HEADS)
    slopes = jnp.broadcast_to(slopes[:, None], (ATT_N_HEADS, LANES))
    cache_spec = pl.BlockSpec((1, tkc, ATT_N_HEADS, LANES), lambda bi, j: (bi, jnp.minimum(j, ncache - 1), 0, 0))
    const = lambda shape: pl.BlockSpec(shape, lambda bi, j: (0,) * len(shape))
    return pl.pallas_call(
        kern,
        grid=(b, ncache + 1),
        in_specs=[
            pl.BlockSpec((1, tq, width), lambda bi, j: (bi, 0, q_colblk)),
            cache_spec,
            cache_spec,
            pl.BlockSpec((1, n_new, width), lambda bi, j: (bi, 0, 0)),
            pl.BlockSpec((1, n_new, width), lambda bi, j: (bi, 0, 1)),
            const((ATT_N_HEADS, LANES)),
            const((SUBLANES, ATT_HEAD_DIM)),
            const((1, ATT_V_HEAD)),
        ],
        out_specs=pl.BlockSpec((1, tq, width), lambda bi, j: (bi, 0, 0)),
        out_shape=jax.ShapeDtypeStruct((b, tq, width), BF16),
        scratch_shapes=[
            pltpu.VMEM((2, ATT_N_HEADS, tq, 1), F32),
            pltpu.VMEM((2, ATT_N_HEADS, tq, 1), F32),
            pltpu.VMEM((2, ATT_N_HEADS, tq, ATT_V_HEAD), F32),
        ],
        compiler_params=_cparams(("arbitrary", "arbitrary")),
        name="diff_attention_cached",
    )(q_src, cache_k, cache_v, kv_new, kv_new, slopes, lam_rows, subln.reshape(1, -1))


ROUTE_W = 2 * TOP_K


def _mixer_out_kernel(y_ref, o_ref, gs_ref, ga_ref, x_ref, wos_ref, woa_ref, wout_ref, nf_ref, rw_ref, rb_ref,
                      run0_ref, x2_ref, h_ref, gate_ref, sel_ref, cnt_ref, run_ref, *, tm):
    i = pl.program_id(0)

    @pl.when(i == 0)
    def _():
        run_ref[...] = run0_ref[...]

    o_ssd = _dot(y_ref[...], wos_ref[...])
    o_att = _dot(o_ref[...], woa_ref[...])
    merged = (jax.nn.sigmoid(gs_ref[...].astype(F32)) * o_ssd
              + jax.nn.sigmoid(ga_ref[...].astype(F32)) * o_att)
    x2 = x_ref[...] + _dot(merged.astype(BF16), wout_ref[...])
    x2_ref[...] = x2
    hn = x2 * lax.rsqrt(jnp.mean(x2 * x2, axis=-1, keepdims=True) + EPS) * nf_ref[...]
    h_ref[...] = hn

    a1, a2, a3 = _split3(hn)
    w1, w2, w3 = _split3(rw_ref[...])
    logits = (_dot(a1, w1) + _dot(a1, w2) + _dot(a2, w1) + _dot(a2, w2) + _dot(a1, w3) + _dot(a3, w1)
              + rb_ref[...])
    lane = lax.broadcasted_iota(jnp.int32, (tm, LANES), 1)
    work = jnp.where(lane < N_EXPERTS, logits, -jnp.inf)
    tops, idxs = [], []
    for _ in range(TOP_K):
        mx = jnp.max(work, axis=-1, keepdims=True)
        ix = jnp.min(jnp.where(work == mx, lane, LANES), axis=-1, keepdims=True)
        tops.append(mx)
        idxs.append(ix)
        work = jnp.where(lane == ix, -jnp.inf, work)
    es = [jnp.exp(tv - tops[0]) for tv in tops]
    den = es[0] + es[1] + es[2] + es[3]
    gates = jnp.zeros((tm, LANES), F32)
    for k in range(TOP_K):
        gates = jnp.where(lane == k, es[k] / den, gates)
    gate_ref[...] = gates[:, :ROUTE_W]

    chosen = jnp.zeros((tm, LANES), jnp.bool_)
    for k in range(TOP_K):
        chosen = jnp.logical_or(chosen, lane == idxs[k])
    multihot = jnp.where(chosen, 1.0, 0.0).astype(BF16)
    ri = lax.broadcasted_iota(jnp.int32, (tm, tm), 0)
    ci = lax.broadcasted_iota(jnp.int32, (tm, tm), 1)
    strict = jnp.where(ci < ri, 1.0, 0.0).astype(BF16)
    prefix = _dot(strict, multihot) + run_ref[...]
    sel = jnp.zeros((tm, LANES), jnp.int32)
    for k in range(TOP_K):
        rank = jnp.sum(jnp.where(lane == idxs[k], prefix, 0.0), axis=-1, keepdims=True)
        sel = jnp.where(lane == k, idxs[k], sel)
        sel = jnp.where(lane == TOP_K + k, rank.astype(jnp.int32), sel)
    sel_ref[...] = sel[:, :ROUTE_W]
    run_ref[...] = run_ref[...] + jnp.sum(multihot.astype(F32), axis=0, keepdims=True)
    cnt_ref[...] = run_ref[...]


def mixer_out(y_ssd, o_att, gates_src, gs_col, ga_col, x, w_o_ssd, w_o_att, w_out, norm_ffn, rw, rb, counts0, tm):
    t, d = x.shape
    assert t % tm == 0
    kern = functools.partial(_mixer_out_kernel, tm=tm)
    const = lambda shape: pl.BlockSpec(shape, lambda i: (0, 0))
    return pl.pallas_call(
        kern,
        grid=(t // tm,),
        in_specs=[
            pl.BlockSpec((tm, SSD_D_INNER), lambda i: (i, 0)),
            pl.BlockSpec((tm, d), lambda i: (i, 0)),
            pl.BlockSpec((tm, d), lambda i: (i, gs_col)),
            pl.BlockSpec((tm, d), lambda i: (i, ga_col)),
            pl.BlockSpec((tm, d), lambda i: (i, 0)),
            const((SSD_D_INNER, d)),
            const((d, d)),
            const((d, d)),
            const((1, d)),
            const((d, LANES)),
            const((1, LANES)),
            const((1, LANES)),
        ],
        out_specs=[
            pl.BlockSpec((tm, d), lambda i: (i, 0)),
            pl.BlockSpec((tm, d), lambda i: (i, 0)),
            pl.BlockSpec((tm, ROUTE_W), lambda i: (i, 0)),
            pl.BlockSpec((tm, ROUTE_W), lambda i: (i, 0)),
            pl.BlockSpec((1, LANES), lambda i: (0, 0)),
        ],
        out_shape=[
            jax.ShapeDtypeStruct((t, d), F32),
            jax.ShapeDtypeStruct((t, d), F32),
            jax.ShapeDtypeStruct((t, ROUTE_W), F32),
            jax.ShapeDtypeStruct((t, ROUTE_W), jnp.int32),
            jax.ShapeDtypeStruct((1, LANES), F32),
        ],
        scratch_shapes=[pltpu.VMEM((1, LANES), F32)],
        compiler_params=_cparams(("arbitrary",)),
        name="mixer_out",
    )(y_ssd, o_att, gates_src, gates_src, x, w_o_ssd, w_o_att, w_out, norm_ffn.reshape(1, d), rw, rb, counts0)


MOE_ROWS = 256
ROUTE_TOK = 256


def _dispatch_kernel(ps_ref, pl_ref, nu_ref, dest_ref, hp_ref, hs_ref, xs_ref, zero_ref, sem,
                     *, prompt_tiles, n_tiles, n_blocks):
    i = pl.program_id(0)

    def zero_copies(action):
        def per_expert(e, c):
            def row(r, c2):
                action(pltpu.make_async_copy(zero_ref.at[pl.ds(0, 1)], xs_ref.at[pl.ds(ps_ref[e] + r, 1)],
                                             sem.at[1]))
                return c2
            return lax.fori_loop(0, pl_ref[e], row, c)

        lax.fori_loop(0, N_EXPERTS, per_expert, 0)

        def tail(b, c):
            row0 = pl.multiple_of(b * MOE_ROWS, MOE_ROWS)
            action(pltpu.make_async_copy(zero_ref, xs_ref.at[pl.ds(row0, MOE_ROWS)], sem.at[1]))
            return c

        lax.fori_loop(nu_ref[0], n_blocks, tail, 0)

    @pl.when(i == 0)
    def _():
        zero_ref[...] = jnp.zeros(zero_ref.shape, F32)
        zero_copies(lambda cp: cp.start())

    def scatter(src_ref):
        def body(g, c):
            t0 = pl.multiple_of(g * SUBLANES, SUBLANES)
            for r in range(SUBLANES):
                for k in range(TOP_K):
                    d = dest_ref[0, 0, (t0 + r) * TOP_K + k]
                    pltpu.make_async_copy(src_ref.at[pl.ds(t0 + r, 1)], xs_ref.at[pl.ds(d, 1)], sem.at[0]).start(
                        priority=k % 2)
            return c
        lax.fori_loop(0, ROUTE_TOK // SUBLANES, body, 0)
        for k in range(TOP_K):
            pltpu.make_async_copy(src_ref, xs_ref.at[pl.ds(0, ROUTE_TOK)], sem.at[0]).wait()

    @pl.when(i < prompt_tiles)
    def _():
        scatter(hp_ref)

    @pl.when(i >= prompt_tiles)
    def _():
        scatter(hs_ref)

    @pl.when(i == n_tiles - 1)
    def _():
        zero_copies(lambda cp: cp.wait())


def moe_dispatch(dest, hn_p, hn_s, pad_start, pad_len, n_used, n_blocks):
    tp, d = hn_p.shape
    tsn = hn_s.shape[0]
    assert tp % ROUTE_TOK == 0 and tsn % ROUTE_TOK == 0
    prompt_tiles = tp // ROUTE_TOK
    n_tiles = prompt_tiles + tsn // ROUTE_TOK
    kern = functools.partial(_dispatch_kernel, prompt_tiles=prompt_tiles, n_tiles=n_tiles, n_blocks=n_blocks)
    return pl.pallas_call(
        kern,
        grid_spec=pltpu.PrefetchScalarGridSpec(
            num_scalar_prefetch=3,
            grid=(n_tiles,),
            in_specs=[
                pl.BlockSpec((1, 1, ROUTE_TOK * TOP_K), lambda i, a, b, c: (i, 0, 0), memory_space=pltpu.SMEM),
                pl.BlockSpec((ROUTE_TOK, d), lambda i, a, b, c: (jnp.minimum(i, prompt_tiles - 1), 0)),
                pl.BlockSpec((ROUTE_TOK, d), lambda i, a, b, c: (jnp.maximum(i - prompt_tiles, 0), 0)),
            ],
            out_specs=pl.BlockSpec(memory_space=pl.ANY),
            scratch_shapes=[pltpu.VMEM((MOE_ROWS, d), F32), pltpu.SemaphoreType.DMA((2,))],
        ),
        out_shape=jax.ShapeDtypeStruct((n_blocks * MOE_ROWS, d), F32),
        compiler_params=_cparams(("arbitrary",)),
        name="moe_dispatch",
    )(pad_start, pad_len, n_used, dest.reshape(n_tiles, 1, ROUTE_TOK * TOP_K), hn_p, hn_s)


def _moe_kernel(be_ref, nu_ref, nxt_ref, x_ref, wgu_hbm, bgu_ref, wd_hbm, bd_ref, o_ref,
                wgu32_ref, wd32_ref, wgu16_ref, wd16_ref, seg_ref, sem):
    b = pl.program_id(0)
    used = b < nu_ref[0]
    new_expert = jnp.logical_or(b == 0, be_ref[b] != be_ref[jnp.maximum(b - 1, 0)])

    def weight_copies(e, slot):
        return (pltpu.make_async_copy(wgu_hbm.at[e], wgu32_ref.at[slot], sem.at[slot, 0]),
                pltpu.make_async_copy(wd_hbm.at[e], wd32_ref.at[slot], sem.at[slot, 1]))

    @pl.when(b == 0)
    def _():
        seg_ref[0] = 0
        for cp in weight_copies(be_ref[0], 0):
            cp.start()

    @pl.when(jnp.logical_and(used, new_expert))
    def _():
        slot = seg_ref[0] % 2
        nxt = nxt_ref[b]

        @pl.when(nxt < nu_ref[0])
        def _():
            for cp in weight_copies(be_ref[nxt], 1 - slot):
                cp.start()

        for cp in weight_copies(be_ref[b], slot):
            cp.wait()
        wgu16_ref[...] = wgu32_ref[slot].astype(BF16)
        wd16_ref[...] = wd32_ref[slot].astype(BF16)
        seg_ref[0] = seg_ref[0] + 1

    @pl.when(used)
    def _():
        x = x_ref[...].astype(BF16)
        gu = _dot(x, wgu16_ref[...]) + bgu_ref[0]
        d_ff = gu.shape[1] // 2
        gate = jnp.minimum(gu[:, :d_ff], SWIGLU_LIMIT)
        up = jnp.clip(gu[:, d_ff:], -SWIGLU_LIMIT, SWIGLU_LIMIT)
        act = (up + 1.0) * gate * jax.nn.sigmoid(SWIGLU_ALPHA * gate)
        o_ref[...] = _dot(act.astype(BF16), wd16_ref[...]) + bd_ref[0]

    @pl.when(jnp.logical_not(used))
    def _():
        o_ref[...] = jnp.zeros(o_ref.shape, F32)


def moe_experts(xs, block_e, n_used, next_first, w_gu, b_gu, w_down, b_down):
    rows, d = xs.shape
    nb = rows // MOE_ROWS
    e, _, gu_w = w_gu.shape
    last = lambda b, nu: jnp.minimum(b, nu[0] - 1)
    return pl.pallas_call(
        _moe_kernel,
        grid_spec=pltpu.PrefetchScalarGridSpec(
            num_scalar_prefetch=3,
            grid=(nb,),
            in_specs=[
                pl.BlockSpec((MOE_ROWS, d), lambda b, be, nu, nx: (last(b, nu), 0)),
                pl.BlockSpec(memory_space=pl.ANY),
                pl.BlockSpec((1, 1, gu_w), lambda b, be, nu, nx: (be[last(b, nu)], 0, 0)),
                pl.BlockSpec(memory_space=pl.ANY),
                pl.BlockSpec((1, 1, d), lambda b, be, nu, nx: (be[last(b, nu)], 0, 0)),
            ],
            out_specs=pl.BlockSpec((MOE_ROWS, d), lambda b, be, nu, nx: (b, 0)),
            scratch_shapes=[
                pltpu.VMEM((2, d, gu_w), F32),
                pltpu.VMEM((2, gu_w // 2, d), F32),
                pltpu.VMEM((d, gu_w), BF16),
                pltpu.VMEM((gu_w // 2, d), BF16),
                pltpu.SMEM((1,), jnp.int32),
                pltpu.SemaphoreType.DMA((2, 2)),
            ],
        ),
        out_shape=jax.ShapeDtypeStruct((rows, d), F32),
        compiler_params=_cparams(("arbitrary",)),
        name="moe_experts",
    )(block_e, n_used, next_first, xs, w_gu, b_gu.reshape(e, 1, gu_w), w_down, b_down.reshape(e, 1, d))


def _combine_kernel(dfirst_ref, dnext_ref, x2_ref, gate_ref, g_ref, ys_ref, o_ref, buf_ref, sem, *, n_tiles):
    i = pl.program_id(0)
    slot = i % 2

    def issue(dref, sl):
        def body(g, c):
            t0 = pl.multiple_of(g * SUBLANES, SUBLANES)
            for r in range(SUBLANES):
                for k in range(TOP_K):
                    d = dref[0, 0, (t0 + r) * TOP_K + k]
                    pltpu.make_async_copy(ys_ref.at[pl.ds(d, 1)], buf_ref.at[sl, k, pl.ds(t0 + r, 1)],
                                          sem.at[sl]).start(priority=k % 2)
            return c
        lax.fori_loop(0, ROUTE_TOK // SUBLANES, body, 0)

    @pl.when(i == 0)
    def _():
        issue(dfirst_ref, 0)

    @pl.when(i + 1 < n_tiles)
    def _():
        issue(dnext_ref, 1 - slot)

    for k in range(TOP_K):
        pltpu.make_async_copy(ys_ref.at[pl.ds(0, ROUTE_TOK)], buf_ref.at[slot, k], sem.at[slot]).wait()

    gates = gate_ref[...]
    moe = buf_ref[slot, 0] * gates[:, 0:1]
    for k in range(1, TOP_K):
        moe = moe + buf_ref[slot, k] * gates[:, k:k + 1]
    tok = x2_ref[...] + moe
    y = tok * lax.rsqrt(jnp.mean(tok * tok, axis=-1, keepdims=True) + EPS)
    o_ref[...] = y * g_ref[...]


def combine(x2, ys, dest, gates, norm_final):
    t, d = x2.shape
    assert t % ROUTE_TOK == 0
    n_tiles = t // ROUTE_TOK
    dest3 = dest.reshape(n_tiles, 1, ROUTE_TOK * TOP_K)
    kern = functools.partial(_combine_kernel, n_tiles=n_tiles)
    smem = lambda imap: pl.BlockSpec((1, 1, ROUTE_TOK * TOP_K), imap, memory_space=pltpu.SMEM)
    return pl.pallas_call(
        kern,
        grid=(n_tiles,),
        in_specs=[
            smem(lambda i: (0, 0, 0)),
            smem(lambda i: (jnp.minimum(i + 1, n_tiles - 1), 0, 0)),
            pl.BlockSpec((ROUTE_TOK, d), lambda i: (i, 0)),
            pl.BlockSpec((ROUTE_TOK, ROUTE_W), lambda i: (i, 0)),
            pl.BlockSpec((1, d), lambda i: (0, 0)),
            pl.BlockSpec(memory_space=pl.ANY),
        ],
        out_specs=pl.BlockSpec((ROUTE_TOK, d), lambda i: (i, 0)),
        out_shape=jax.ShapeDtypeStruct((t, d), F32),
        scratch_shapes=[pltpu.VMEM((2, TOP_K, ROUTE_TOK, d), F32), pltpu.SemaphoreType.DMA((2,))],
        compiler_params=_cparams(("arbitrary",)),
        name="combine",
    )(dest3, dest3, x2, gates, norm_final.reshape(1, d), ys)


def kernel(x_prompt, x_sample, cache_k, cache_v, state_ssm, state_conv, norm_mix, w_in, conv_w, conv_b, dt_bias, a_log, d_skip, ssd_norm, w_o_ssd, lambda_q1, lambda_k1, lambda_q2, lambda_k2, subln, w_o_att, w_out, norm_ffn, router_w, router_b, w_gu, b_gu, w_down, b_down, norm_final):
    bp, s, d = x_prompt.shape
    bs, ts, _ = x_sample.shape
    past = cache_k.shape[2]
    depth = w_in.shape[0]
    assert depth == 1
    layer = 0
    lambda_init = 0.8 - 0.6 * math.exp(-0.3 * layer)
    tp, tsn = bp * s, bs * ts
    t_all = tp + tsn

    heads = ATT_N_HEADS
    xp2 = x_prompt.reshape(tp, d)
    xs2 = x_sample.reshape(tsn, d)

    sizes = (SSD_D_INNER, SSD_CONV_CH, SSD_N_HEADS, d, d, d, d, d)
    offs = [0]
    for sz in sizes:
        offs.append(offs[-1] + sz)
    w = w_in[layer]
    seg = lambda i: w[:, offs[i]:offs[i + 1]]
    dt_pad = jnp.zeros((d, PROJ_XD_W - SSD_CONV_CH - SSD_N_HEADS), F32)
    w_all = jnp.concatenate([seg(4), seg(5), seg(0), seg(3), seg(6), seg(7), seg(1), seg(2), dt_pad],
                            axis=1).astype(BF16)
    k_p, v_p, kv16_p, zq_p, xd_p = in_proj(xp2, norm_mix[layer], w_all, tm=2048)
    k_s, v_s, kv16_s, zq_s, xd_s = in_proj(xs2, norm_mix[layer], w_all, tm=tsn)

    prev_p = jnp.zeros((bp, SUBLANES, SSD_CONV_CH), F32)
    h0_p = jnp.zeros((bp, SSD_D_STATE, SSD_D_INNER), F32)
    ssd_w = (conv_w[layer], conv_b[layer], dt_bias[layer], a_log[layer], d_skip[layer], ssd_norm[layer])
    y_p, hT_p = ssd_mixer(xd_p, zq_p, 0, prev_p, h0_p, *ssd_w, b=bp, l=s, tl=256, valid_len=None, out_rows=tp)

    pad_rows = CHUNK - ts
    pad_seq = lambda a: jnp.pad(a.reshape(bs, ts, -1), ((0, 0), (0, pad_rows), (0, 0))).reshape(bs * CHUNK, -1)
    prev_s = jnp.pad(state_conv[layer], ((0, 0), (SUBLANES - (SSD_CONV_WIDTH - 1), 0), (0, 0)))
    h0_s = jnp.swapaxes(state_ssm[layer].reshape(bs, SSD_D_INNER, SSD_D_STATE), 1, 2)
    y_s, hT_s = ssd_mixer(pad_seq(xd_s), pad_seq(zq_s[:, :SSD_D_INNER]), 0, prev_s, h0_s, *ssd_w, b=bs, l=CHUNK,
                          tl=CHUNK, valid_len=ts, out_rows=bs * CHUNK)
    y_s = y_s.reshape(bs, CHUNK, -1)[:, :ts].reshape(tsn, -1)

    lam_rows = jnp.concatenate([lambda_q1[layer][None], lambda_k1[layer][None], lambda_q2[layer][None],
                                lambda_k2[layer][None], jnp.zeros((4, ATT_HEAD_DIM), F32)], axis=0)
    o_p = diff_attention_t(zq_p, 0, SSD_D_INNER // d, kv16_p, 0, kv16_p, 1, lam_rows, subln[layer], b=bp, lq=s,
                           lk=s, tq=512, tk=1024, q_off=0, kv_len=s, lambda_init=lambda_init, out_rows=tp)
    o_s = diff_attention_cached(zq_s.reshape(bs, ts, -1), SSD_D_INNER // d, cache_k[layer], cache_v[layer],
                                kv16_s.reshape(bs, ts, 2 * d), lam_rows, subln[layer], tkc=1024,
                                lambda_init=lambda_init).reshape(tsn, d)

    rw = jnp.zeros((d, LANES), F32).at[:, :N_EXPERTS].set(router_w[layer])
    rb = jnp.zeros((1, LANES), F32).at[0, :N_EXPERTS].set(router_b[layer])
    mix_w = (w_o_ssd[layer].astype(BF16), w_o_att[layer].astype(BF16), w_out[layer].astype(BF16), norm_ffn[layer],
             rw, rb)
    x2_p, hn_p, gate_p, sel_p, cnt_p = mixer_out(y_p, o_p, zq_p, 3, 4, xp2, *mix_w,
                                                 jnp.zeros((1, LANES), F32), tm=512)
    x2_s, hn_s, gate_s, sel_s, cnt_all = mixer_out(y_s, o_s, zq_s, 3, 4, xs2, *mix_w, cnt_p, tm=tsn)

    sel = jnp.concatenate([sel_p, sel_s], axis=0)
    top_e = sel[:, :TOP_K]
    rank = sel[:, TOP_K:]
    cnt = cnt_all[0, :N_EXPERTS].astype(jnp.int32)
    padded = (cnt + MOE_ROWS - 1) // MOE_ROWS * MOE_ROWS
    ends = jnp.cumsum(padded)
    starts = ends - padded
    dest = (starts[top_e] + rank).reshape(-1)
    nb = (t_all * TOP_K + N_EXPERTS * (MOE_ROWS - 1) + MOE_ROWS - 1) // MOE_ROWS
    block_start = jnp.arange(nb, dtype=jnp.int32) * MOE_ROWS
    block_e = jnp.minimum(jnp.sum((ends[None, :] <= block_start[:, None]).astype(jnp.int32), axis=1),
                          N_EXPERTS - 1)
    n_used = (ends[-1] // MOE_ROWS).astype(jnp.int32).reshape(1)

    xs_sorted = moe_dispatch(dest, hn_p, hn_s, starts + cnt, padded - cnt, n_used, nb)
    next_first = ends[block_e] // MOE_ROWS
    ys = moe_experts(xs_sorted, block_e, n_used, next_first, w_gu[layer], b_gu[layer], w_down[layer],
                     b_down[layer])
    y_prompt = combine(x2_p, ys, dest[:tp * TOP_K], gate_p, norm_final).reshape(bp, s, d)
    y_sample = combine(x2_s, ys, dest[tp * TOP_K:], gate_s, norm_final).reshape(bs, ts, d)

    new_k_p = k_p.reshape(1, bp, s, heads, 2 * ATT_HEAD_DIM)
    new_v_p = v_p.reshape(1, bp, s, heads, ATT_V_HEAD)
    new_k_s = k_s.reshape(1, bs, ts, heads, 2 * ATT_HEAD_DIM)
    new_v_s = v_s.reshape(1, bs, ts, heads, ATT_V_HEAD)
    ssm_p = jnp.swapaxes(hT_p, 1, 2).reshape(1, bp, SSD_N_HEADS, SSD_HEAD_DIM, SSD_D_STATE)
    ssm_s = jnp.swapaxes(hT_s, 1, 2).reshape(1, bs, SSD_N_HEADS, SSD_HEAD_DIM, SSD_D_STATE)
    keep = SSD_CONV_WIDTH - 1
    conv_p = xd_p.reshape(bp, s, -1)[:, s - keep:, :SSD_CONV_CH][None]
    raw_s = jnp.concatenate([state_conv[layer], xd_s.reshape(bs, ts, -1)[:, :, :SSD_CONV_CH]], axis=1)
    conv_s = raw_s[:, -keep:][None]
    return (y_prompt, y_sample, new_k_p, new_v_p, ssm_p, conv_p, new_k_s, new_v_s, ssm_s, conv_s)
```

```python
import functools
import math

import jax
import jax.numpy as jnp
from jax import lax
from jax.experimental import pallas as pl
from jax.experimental.pallas import tpu as pltpu

EPS = 1e-5
CHUNK = 64
D_MODEL = 1024
SSD_D_INNER = 2048
SSD_HEAD_DIM = 64
SSD_N_HEADS = 32
SSD_GROUPS = 4
SSD_D_STATE = 128
SSD_CONV_WIDTH = 4
SSD_CONV_CH = 3072
ATT_HEAD_DIM = 64
ATT_N_HEADS = 8
ATT_V_HEAD = 128
ALIBI_MAX_BIAS = 8.0
N_EXPERTS = 32
TOP_K = 4
SWIGLU_LIMIT = 7.0
SWIGLU_ALPHA = 1.702

LANES = 128
SUBLANES = 8
VMEM_LIMIT = 56 * 1024 * 1024
NEG_BIG = -1e30

BF16 = jnp.bfloat16
F32 = jnp.float32


def _cparams(sem, flags=None):
    return pltpu.CompilerParams(dimension_semantics=sem, vmem_limit_bytes=VMEM_LIMIT, flags=flags)


def _split3(x):
    h1 = x.astype(BF16)
    r1 = x - h1.astype(F32)
    h2 = r1.astype(BF16)
    h3 = (r1 - h2.astype(F32)).astype(BF16)
    return h1, h2, h3


def _dot(a, b):
    return jnp.dot(a, b, preferred_element_type=F32)


def _dot_exact_rhs(x, m):
    h1, h2, h3 = _split3(x)
    return _dot(h1, m) + _dot(h2, m) + _dot(h3, m)


def _dot_exact_lhs(m, x):
    h1, h2, h3 = _split3(x)
    return _dot(m, h1) + _dot(m, h2) + _dot(m, h3)


PROJ_TN = 256
PROJ_K = (0, D_MODEL // PROJ_TN)
PROJ_V = (PROJ_K[1], PROJ_K[1] + D_MODEL // PROJ_TN)
PROJ_ZQ_W = SSD_D_INNER + 3 * D_MODEL
PROJ_ZQ = (PROJ_V[1], PROJ_V[1] + PROJ_ZQ_W // PROJ_TN)
PROJ_XD_W = -(-(SSD_CONV_CH + SSD_N_HEADS) // PROJ_TN) * PROJ_TN
PROJ_XD = (PROJ_ZQ[1], PROJ_ZQ[1] + PROJ_XD_W // PROJ_TN)


def _in_proj_kernel(x_ref, g_ref, w_ref, k_ref, v_ref, kv16_ref, zq_ref, xd_ref, xn_ref):
    j = pl.program_id(1)

    @pl.when(j == 0)
    def _():
        x = x_ref[...]
        y = x * lax.rsqrt(jnp.mean(x * x, axis=-1, keepdims=True) + EPS)
        xn_ref[...] = (y * g_ref[...]).astype(BF16)

    @pl.when(j < PROJ_K[1])
    def _():
        acc = _dot(xn_ref[...], w_ref[...])
        k_ref[...] = acc
        kv16_ref[...] = acc.astype(BF16)

    @pl.when(jnp.logical_and(j >= PROJ_V[0], j < PROJ_V[1]))
    def _():
        acc = _dot(xn_ref[...], w_ref[...])
        v_ref[...] = acc
        kv16_ref[...] = acc.astype(BF16)

    @pl.when(jnp.logical_and(j >= PROJ_ZQ[0], j < PROJ_ZQ[1]))
    def _():
        zq_ref[...] = _dot(xn_ref[...], w_ref[...]).astype(BF16)

    @pl.when(j >= PROJ_XD[0])
    def _():
        xd_ref[...] = _dot(xn_ref[...], w_ref[...])


def in_proj(x, gain, w, tm):
    t, d = x.shape
    assert t % tm == 0 and w.shape[1] == PROJ_XD[1] * PROJ_TN
    tn = PROJ_TN

    def out_map(rng):
        return lambda i, j: (i, jnp.clip(j - rng[0], 0, rng[1] - rng[0] - 1))

    return pl.pallas_call(
        _in_proj_kernel,
        grid=(t // tm, PROJ_XD[1]),
        in_specs=[
            pl.BlockSpec((tm, d), lambda i, j: (i, 0)),
            pl.BlockSpec((1, d), lambda i, j: (0, 0)),
            pl.BlockSpec((d, tn), lambda i, j: (0, j)),
        ],
        out_specs=[
            pl.BlockSpec((tm, tn), out_map(PROJ_K)),
            pl.BlockSpec((tm, tn), out_map(PROJ_V)),
            pl.BlockSpec((tm, tn), out_map((PROJ_K[0], PROJ_V[1]))),
            pl.BlockSpec((tm, tn), out_map(PROJ_ZQ)),
            pl.BlockSpec((tm, tn), out_map(PROJ_XD)),
        ],
        out_shape=[
            jax.ShapeDtypeStruct((t, d), F32),
            jax.ShapeDtypeStruct((t, d), F32),
            jax.ShapeDtypeStruct((t, 2 * d), BF16),
            jax.ShapeDtypeStruct((t, PROJ_ZQ_W), BF16),
            jax.ShapeDtypeStruct((t, PROJ_XD_W), F32),
        ],
        scratch_shapes=[pltpu.VMEM((tm, d), BF16)],
        compiler_params=_cparams(("arbitrary", "arbitrary")),
        name="in_proj",
    )(x, gain.reshape(1, d), w)


CONV_ROWS, CONV_COLS = 64, 512


def _ssd_kernel(xbc_ref, z_ref, prev_ref, h0_ref, cw_ref, cb_ref, dtb_ref, alog_ref, dsk_ref, gn_ref,
                y_ref, hout_ref, xpad_ref, xc_ref, tail_ref, st_ref, *, tl, valid_len):
    t = pl.program_id(1)
    nt = pl.num_programs(1)
    nch = tl // CHUNK
    dinner = SSD_D_INNER
    gw = dinner // SSD_GROUPS
    ns = SSD_D_STATE

    @pl.when(t == 0)
    def _():
        tail_ref[...] = prev_ref[0]
        st_ref[...] = h0_ref[0]

    xpad_ref[0:SUBLANES, :] = tail_ref[...]
    xpad_ref[SUBLANES:SUBLANES + tl, :] = xbc_ref[:, 0:SSD_CONV_CH]
    tail_ref[...] = xbc_ref[tl - SUBLANES:tl, 0:SSD_CONV_CH]
    for r in range(0, tl, CONV_ROWS):
        for c in range(0, SSD_CONV_CH, CONV_COLS):
            acc = cb_ref[:, c:c + CONV_COLS]
            for k in range(SSD_CONV_WIDTH):
                off = r + SUBLANES - (SSD_CONV_WIDTH - 1) + k
                acc = acc + xpad_ref[off:off + CONV_ROWS, c:c + CONV_COLS] * cw_ref[k:k + 1, c:c + CONV_COLS]
            xc_ref[r:r + CONV_ROWS, c:c + CONV_COLS] = acc * jax.nn.sigmoid(acc)

    head_of_lane = lax.broadcasted_iota(jnp.int32, (SSD_N_HEADS, dinner), 1) // SSD_HEAD_DIM
    expand = (head_of_lane == lax.broadcasted_iota(jnp.int32, (SSD_N_HEADS, dinner), 0)).astype(BF16)
    ti = lax.broadcasted_iota(jnp.int32, (CHUNK, CHUNK), 0)
    si = lax.broadcasted_iota(jnp.int32, (CHUNK, CHUNK), 1)
    tril = (si <= ti).astype(BF16)
    row_c = lax.broadcasted_iota(jnp.int32, (CHUNK, dinner), 0)
    pos_in_head = lax.broadcasted_iota(jnp.int32, (CHUNK, dinner), 1) % CHUNK
    upper = row_c <= pos_in_head
    row_p = lax.broadcasted_iota(jnp.int32, (CHUNK, LANES), 0)
    lane_p = lax.broadcasted_iota(jnp.int32, (CHUNK, LANES), 1)
    causal_pair = (lane_p % CHUNK) <= row_p
    left_half = lane_p < SSD_HEAD_DIM
    a_neg_e = _dot_exact_rhs(-jnp.exp(alog_ref[...]), expand)
    dsk_e = _dot_exact_rhs(dsk_ref[...], expand)

    def chunk_body(c, carry):
        r0 = pl.multiple_of(c * CHUNK, CHUNK)
        xs = xc_ref[pl.ds(r0, CHUNK), 0:dinner]
        dt_raw = xbc_ref[pl.ds(r0, CHUNK), SSD_CONV_CH:SSD_CONV_CH + SSD_N_HEADS]
        dtv = dt_raw + dtb_ref[...]
        dt = jnp.maximum(dtv, 0.0) + jnp.log1p(jnp.exp(-jnp.abs(dtv)))
        if valid_len is not None:
            rows = t * tl + r0 + lax.broadcasted_iota(jnp.int32, (CHUNK, SSD_N_HEADS), 0)
            dt = jnp.where(rows < valid_len, dt, 0.0)
        dt_e = _dot_exact_rhs(dt, expand)
        a_e = dt_e * a_neg_e
        acs_e = _dot_exact_lhs(tril, a_e)
        rowterm = jnp.sum(jnp.where(upper, a_e, 0.0), axis=0, keepdims=True)
        acs_last = acs_e[CHUNK - 1:CHUNK, :]
        xdt = xs * dt_e
        x_dec = (xdt * jnp.exp(acs_last - acs_e)).astype(BF16)
        e_acs = jnp.exp(acs_e)
        e_last = jnp.exp(acs_last)

        y_parts = []
        for g in range(SSD_GROUPS):
            lo = g * gw
            bm = xc_ref[pl.ds(r0, CHUNK), dinner + g * ns:dinner + (g + 1) * ns].astype(BF16)
            cm = xc_ref[pl.ds(r0, CHUNK), dinner + SSD_GROUPS * ns + g * ns:
                        dinner + SSD_GROUPS * ns + (g + 1) * ns].astype(BF16)
            b2 = jnp.concatenate([bm, bm], axis=0)
            cb2 = lax.dot_general(cm, b2, (((1,), (1,)), ((), ())), preferred_element_type=F32)
            st_g = st_ref[:, lo:lo + gw]
            y_off = _dot(cm, st_g.astype(BF16)) * e_acs[:, lo:lo + gw]
            pieces = []
            for j in range(gw // LANES):
                l0 = lo + j * LANES
                diff = acs_e[:, l0:l0 + LANES] - rowterm[:, l0:l0 + LANES]
                dec = jnp.where(causal_pair, jnp.exp(diff), 0.0)
                scores = (cb2 * dec).astype(BF16)
                xp = xdt[:, l0:l0 + LANES]
                xblk = jnp.concatenate([jnp.where(left_half, xp, 0.0), jnp.where(left_half, 0.0, xp)],
                                       axis=0).astype(BF16)
                pieces.append(_dot(scores, xblk))
            y_diag = jnp.concatenate(pieces, axis=1)
            upd = lax.dot_general(bm, x_dec[:, lo:lo + gw], (((0,), (0,)), ((), ())),
                                  preferred_element_type=F32)
            st_ref[:, lo:lo + gw] = e_last[:, lo:lo + gw] * st_g + upd
            yg = y_diag + y_off + xs[:, lo:lo + gw] * dsk_e[:, lo:lo + gw]
            zg = z_ref[pl.ds(r0, CHUNK), lo:lo + gw].astype(F32)
            yg = yg * (zg * jax.nn.sigmoid(zg))
            yn = yg * lax.rsqrt(jnp.mean(yg * yg, axis=-1, keepdims=True) + EPS)
            y_parts.append((yn * gn_ref[:, lo:lo + gw]).astype(y_ref.dtype))
        y_ref[pl.ds(r0, CHUNK), :] = jnp.concatenate(y_parts, axis=1)
        return carry

    lax.fori_loop(0, nch, chunk_body, 0)

    @pl.when(t == nt - 1)
    def _():
        hout_ref[0] = st_ref[...]


def ssd_mixer(xbc, z_src, z_col, conv_prev8, h0_t, conv_w, conv_b, dt_bias, a_log, d_skip, ssd_norm,
              b, l, tl, valid_len, out_rows):
    wx = xbc.shape[1]
    assert l % tl == 0 and tl % CHUNK == 0 and CHUNK == SSD_HEAD_DIM
    nt = l // tl
    kern = functools.partial(_ssd_kernel, tl=tl, valid_len=valid_len)
    full = lambda shape: pl.BlockSpec(shape, lambda i, j: (0,) * len(shape))
    return pl.pallas_call(
        kern,
        grid=(b, nt),
        in_specs=[
            pl.BlockSpec((tl, wx), lambda i, j: (i * nt + j, 0)),
            pl.BlockSpec((tl, SSD_D_INNER), lambda i, j: (i * nt + j, z_col)),
            pl.BlockSpec((1, SUBLANES, SSD_CONV_CH), lambda i, j: (i, 0, 0)),
            pl.BlockSpec((1, SSD_D_STATE, SSD_D_INNER), lambda i, j: (i, 0, 0)),
            full((SSD_CONV_WIDTH, SSD_CONV_CH)),
            full((1, SSD_CONV_CH)),
            full((1, SSD_N_HEADS)),
            full((1, SSD_N_HEADS)),
            full((1, SSD_N_HEADS)),
            full((1, SSD_D_INNER)),
        ],
        out_specs=[
            pl.BlockSpec((tl, SSD_D_INNER), lambda i, j: (i * nt + j, 0)),
            pl.BlockSpec((1, SSD_D_STATE, SSD_D_INNER), lambda i, j: (i, 0, 0)),
        ],
        out_shape=[
            jax.ShapeDtypeStruct((out_rows, SSD_D_INNER), BF16),
            jax.ShapeDtypeStruct((b, SSD_D_STATE, SSD_D_INNER), F32),
        ],
        scratch_shapes=[
            pltpu.VMEM((tl + SUBLANES, SSD_CONV_CH), F32),
            pltpu.VMEM((tl, SSD_CONV_CH), F32),
            pltpu.VMEM((SUBLANES, SSD_CONV_CH), F32),
            pltpu.VMEM((SSD_D_STATE, SSD_D_INNER), F32),
        ],
        compiler_params=_cparams(("arbitrary", "arbitrary")),
        name="ssd_mixer",
    )(xbc, z_src, conv_prev8, h0_t, conv_w, conv_b.reshape(1, -1), dt_bias.reshape(1, -1),
      a_log.reshape(1, -1), d_skip.reshape(1, -1), ssd_norm.reshape(1, -1))


BIAS_SPLIT = 32
LOG2E = 1.4426950408889634
LOG2E_PARTS = (1.4453125, -0.00262451171875, 7.063150405883789e-06)
FLAG_FIRST, FLAG_LAST, VARIANT_SHIFT = 1, 2, 2
ATT_SUB = 256
MODE_PLAIN, MODE_MASKED, MODE_SKIP = "plain", "masked", "skip"


def _attn_sub(tk):
    nsub = tk // ATT_SUB if tk % ATT_SUB == 0 else 1
    return nsub, tk // nsub


def _attn_t_kernel(qi_ref, ki_ref, fl_ref, q_ref, k_ref, v_ref, slope_ref, lam_ref, sub_ref, o_ref,
                   m_ref, l_ref, acc_ref, kb_ref, corr_ref, s0_ref, s1_ref, mt0_ref, mt1_ref,
                   *, tq, tk, q_off, kv_len, lambda_init, variants):
    p_idx = pl.program_id(1)
    qi = qi_ref[p_idx]
    ki = ki_ref[p_idx]
    flags = fl_ref[p_idx]
    qstart = q_off + qi * tq
    kstart = ki * tk
    half = ATT_HEAD_DIM
    scale = ATT_HEAD_DIM ** -0.5

    @pl.when((flags & FLAG_FIRST) != 0)
    def _():
        m_ref[...] = jnp.full(m_ref.shape, NEG_BIG, F32)
        l_ref[...] = jnp.zeros(l_ref.shape, F32)
        acc_ref[...] = jnp.zeros(acc_ref.shape, F32)

    lane_k = lax.broadcasted_iota(jnp.int32, (tk, LANES), 1)
    rel = kstart - qstart + lax.broadcasted_iota(jnp.int32, (tk, LANES), 0)
    hi = (rel // BIAS_SPLIT) * BIAS_SPLIT
    lo = rel - hi
    pos_lane = lane_k % half
    n_parts = len(LOG2E_PARTS)
    kb_ref[...] = jnp.where(pos_lane < n_parts, hi, jnp.where(pos_lane < 2 * n_parts, lo, 0)).astype(F32).astype(BF16)

    variant = flags >> VARIANT_SHIFT
    nsub, ts = _attn_sub(tk)

    def mask_terms(modes):
        for j, mode in enumerate(modes):
            if mode != MODE_MASKED:
                continue
            r0 = j * ts
            kpos = kstart + r0 + lax.broadcasted_iota(jnp.int32, (ts, tq), 0)
            qpos = qstart + lax.broadcasted_iota(jnp.int32, (ts, tq), 1)
            allowed = jnp.logical_and(kpos // CHUNK <= qpos // CHUNK, kpos < kv_len)
            corr_ref[0, r0:r0 + ts, :] = jnp.where(kpos > qpos, (2.0 * LOG2E) * (qpos - kpos).astype(F32), 0.0)
            corr_ref[1, r0:r0 + ts, :] = jnp.where(allowed, 0.0, NEG_BIG)

    lane_q = lax.broadcasted_iota(jnp.int32, (tq, LANES), 1)
    lane_ks = lax.broadcasted_iota(jnp.int32, (ts, LANES), 1)
    part = (lax.broadcasted_iota(jnp.int32, (1, LANES), 1) % half) % n_parts
    log2e_lanes = jnp.where(part == 0, LOG2E_PARTS[0], jnp.where(part == 1, LOG2E_PARTS[1], LOG2E_PARTS[2]))

    def score_pass(h, s_ref, mt_ref, modes):
        c0 = pl.multiple_of(h * LANES, LANES)
        q = (q_ref[:, pl.ds(c0, LANES)].astype(F32) * (scale * LOG2E)).astype(BF16)
        slope = slope_ref[pl.ds(h, 1), :]
        slope_b = jnp.broadcast_to((slope * log2e_lanes).astype(BF16), (tq, LANES))
        zero_q = jnp.zeros((tq, LANES), BF16)
        for idx in range(2):
            own = (lane_q < half) if idx == 0 else (lane_q >= half)
            q_aug = jnp.where(own, q, jnp.where((lane_q % half) < 2 * n_parts, slope_b, zero_q))
            own_k = (lane_ks < half) if idx == 0 else (lane_ks >= half)
            mt = None
            for j, mode in enumerate(modes):
                if mode == MODE_SKIP:
                    continue
                r0 = j * ts
                k_aug = jnp.where(own_k, k_ref[r0:r0 + ts, pl.ds(c0, LANES)], kb_ref[r0:r0 + ts, :])
                s = lax.dot_general(k_aug, q_aug, (((1,), (1,)), ((), ())), preferred_element_type=F32)
                if mode == MODE_MASKED:
                    s = s + slope[:, 0:1] * corr_ref[0, r0:r0 + ts, :] + corr_ref[1, r0:r0 + ts, :]
                s_ref[idx, r0:r0 + ts, :] = s
                mj = jnp.max(s, axis=0, keepdims=True)
                mt = mj if mt is None else jnp.maximum(mt, mj)
                yield
            mt_ref[idx] = mt

    def value_pass(h, s_ref, mt_ref, modes):
        c0 = pl.multiple_of(h * LANES, LANES)
        for idx in range(2):
            m_prev = m_ref[idx, h]
            m_new = jnp.maximum(m_prev, mt_ref[idx])
            alpha = jnp.exp2(m_prev - m_new)
            lsum, pv = None, None
            for j, mode in enumerate(modes):
                if mode == MODE_SKIP:
                    continue
                r0 = j * ts
                p = jnp.exp2(s_ref[idx, r0:r0 + ts, :] - m_new)
                lj = jnp.sum(p, axis=0, keepdims=True)
                pj = lax.dot_general(v_ref[r0:r0 + ts, pl.ds(c0, LANES)], p.astype(BF16), (((0,), (0,)), ((), ())),
                                     preferred_element_type=F32)
                lsum = lj if lsum is None else lsum + lj
                pv = pj if pv is None else pv + pj
                yield
            l_ref[idx, h] = alpha * l_ref[idx, h] + lsum
            acc_ref[idx, h] = alpha * acc_ref[idx, h] + pv
            m_ref[idx, h] = m_new

    def run(*gens):
        live = list(gens)
        while live:
            for g in list(live):
                try:
                    next(g)
                except StopIteration:
                    live.remove(g)

    def all_heads(modes):
        bufs = ((s0_ref, mt0_ref), (s1_ref, mt1_ref))
        run(score_pass(0, *bufs[0], modes))

        def pair(g, c):
            h = 2 * g
            run(score_pass(h + 1, *bufs[1], modes), value_pass(h, *bufs[0], modes))
            run(score_pass(h + 2, *bufs[0], modes), value_pass(h + 1, *bufs[1], modes))
            return c

        lax.fori_loop(0, ATT_N_HEADS // 2 - 1, pair, 0)
        last = ATT_N_HEADS - 1
        run(score_pass(last, *bufs[1], modes), value_pass(last - 1, *bufs[0], modes))
        run(value_pass(last, *bufs[1], modes))

    for vi, modes in enumerate(variants):
        @pl.when(variant == vi)
        def _(modes=modes):
            mask_terms(modes)
            all_heads(modes)

    @pl.when((flags & FLAG_LAST) != 0)
    def _():
        lp = lam_ref[...]
        lam = (jnp.exp(jnp.sum(lp[0:1] * lp[1:2], axis=-1, keepdims=True))
               - jnp.exp(jnp.sum(lp[2:3] * lp[3:4], axis=-1, keepdims=True)) + lambda_init)

        def fin(h, c):
            c0 = pl.multiple_of(h * LANES, LANES)
            o = acc_ref[0, h] / l_ref[0, h] - lam * (acc_ref[1, h] / l_ref[1, h])
            on = o * lax.rsqrt(jnp.mean(o * o, axis=0, keepdims=True) + EPS)
            on = (on * sub_ref[...]) * (1.0 - lambda_init)
            o_ref[:, pl.ds(c0, LANES)] = on.T.astype(o_ref.dtype)
            return c

        lax.fori_loop(0, ATT_N_HEADS, fin, 0)


def _attn_pairs(nq, nk, tq, tk, q_off, kv_len):
    nsub, ts = _attn_sub(tk)
    qis, kis, fls, variants = [], [], [], []
    for qi in range(nq):
        first_q = q_off + qi * tq
        last_q = first_q + tq - 1
        kend = min((last_q // CHUNK + 1) * CHUNK, kv_len)
        nkv = -(-kend // tk)
        for ki in range(nkv):
            modes = []
            for j in range(nsub):
                ks = ki * tk + j * ts
                if ks >= kend:
                    modes.append(MODE_SKIP)
                elif ks + ts <= (first_q // CHUNK) * CHUNK and ks + ts <= kv_len:
                    modes.append(MODE_PLAIN)
                else:
                    modes.append(MODE_MASKED)
            modes = tuple(modes)
            if modes not in variants:
                variants.append(modes)
            fl = ((FLAG_FIRST if ki == 0 else 0) | (FLAG_LAST if ki == nkv - 1 else 0)
                  | (variants.index(modes) << VARIANT_SHIFT))
            qis.append(qi)
            kis.append(ki)
            fls.append(fl)
    return qis, kis, fls, tuple(variants)


def diff_attention_t(q_src, q_blk0, q_colblk, k_src, k_colblk, v_src, v_colblk, lam_rows, subln, *, b, lq, lk,
                     tq, tk, q_off, kv_len, lambda_init, out_rows):
    assert lq % tq == 0 and lk % tk == 0 and tq % LANES == 0
    assert q_off + lq <= 256 * BIAS_SPLIT + tq
    nq, nk = lq // tq, lk // tk
    width = ATT_N_HEADS * LANES
    qis, kis, fls, variants = _attn_pairs(nq, nk, tq, tk, q_off, kv_len)
    kern = functools.partial(_attn_t_kernel, tq=tq, tk=tk, q_off=q_off, kv_len=kv_len, lambda_init=lambda_init,
                             variants=variants)
    slopes = jnp.exp2(-ALIBI_MAX_BIAS * jnp.arange(1, ATT_N_HEADS + 1, dtype=F32) / ATT_N_HEADS)
    slopes = jnp.broadcast_to(slopes[:, None], (ATT_N_HEADS, LANES))
    const = lambda shape: pl.BlockSpec(shape, lambda bi, p, qt, kt, ft: (0,) * len(shape))
    return pl.pallas_call(
        kern,
        grid_spec=pltpu.PrefetchScalarGridSpec(
            num_scalar_prefetch=3,
            grid=(b, len(qis)),
            in_specs=[
                pl.BlockSpec((tq, width), lambda bi, p, qt, kt, ft: (q_blk0 + bi * nq + qt[p], q_colblk)),
                pl.BlockSpec((tk, width), lambda bi, p, qt, kt, ft: (bi * nk + kt[p], k_colblk)),
                pl.BlockSpec((tk, width), lambda bi, p, qt, kt, ft: (bi * nk + kt[p], v_colblk)),
                const((ATT_N_HEADS, LANES)),
                const((SUBLANES, ATT_HEAD_DIM)),
                const((ATT_V_HEAD, 1)),
            ],
            out_specs=pl.BlockSpec((tq, width), lambda bi, p, qt, kt, ft: (bi * nq + qt[p], 0)),
            scratch_shapes=[
                pltpu.VMEM((2, ATT_N_HEADS, 1, tq), F32),
                pltpu.VMEM((2, ATT_N_HEADS, 1, tq), F32),
                pltpu.VMEM((2, ATT_N_HEADS, ATT_V_HEAD, tq), F32),
                pltpu.VMEM((tk, LANES), BF16),
                pltpu.VMEM((2, tk, tq), F32),
                pltpu.VMEM((2, tk, tq), F32),
                pltpu.VMEM((2, tk, tq), F32),
                pltpu.VMEM((2, 1, tq), F32),
                pltpu.VMEM((2, 1, tq), F32),
            ],
        ),
        out_shape=jax.ShapeDtypeStruct((out_rows, width), BF16),
        compiler_params=_cparams(("arbitrary", "arbitrary")),
        name="diff_attention_t",
    )(jnp.asarray(qis, jnp.int32), jnp.asarray(kis, jnp.int32), jnp.asarray(fls, jnp.int32),
      q_src, k_src, v_src, slopes, lam_rows, subln.reshape(-1, 1))


def _attn_cache_kernel(q_ref, ck_ref, cv_ref, kn_ref, vn_ref, slope_ref, lam_ref, sub_ref, o_ref,
                       m_ref, l_ref, acc_ref, *, tq, tkc, ncache, past, n_new, lambda_init):
    j = pl.program_id(1)
    half = ATT_HEAD_DIM
    n_parts = len(LOG2E_PARTS)
    scale = ATT_HEAD_DIM ** -0.5

    @pl.when(j == 0)
    def _():
        m_ref[...] = jnp.full(m_ref.shape, NEG_BIG, F32)
        l_ref[...] = jnp.zeros(l_ref.shape, F32)
        acc_ref[...] = jnp.zeros(acc_ref.shape, F32)

    lane_q = lax.broadcasted_iota(jnp.int32, (tq, LANES), 1)
    part = (lax.broadcasted_iota(jnp.int32, (1, LANES), 1) % half) % n_parts
    log2e_lanes = jnp.where(part == 0, LOG2E_PARTS[0], jnp.where(part == 1, LOG2E_PARTS[1], LOG2E_PARTS[2]))

    def position_lanes(rows, kstart):
        lane = lax.broadcasted_iota(jnp.int32, (rows, LANES), 1)
        rel = kstart - past + lax.broadcasted_iota(jnp.int32, (rows, LANES), 0)
        hi = (rel // BIAS_SPLIT) * BIAS_SPLIT
        pos = lane % half
        kb = jnp.where(pos < n_parts, hi, jnp.where(pos < 2 * n_parts, rel - hi, 0)).astype(F32).astype(BF16)
        return kb, lane

    def head(h, k, v, kb, lane_k, corr):
        q = (q_ref[0, :, h * LANES:(h + 1) * LANES].astype(F32) * (scale * LOG2E)).astype(BF16)
        slope = slope_ref[h:h + 1, :]
        slope_b = jnp.broadcast_to((slope * log2e_lanes).astype(BF16), (tq, LANES))
        for idx in range(2):
            own = (lane_q < half) if idx == 0 else (lane_q >= half)
            q_aug = jnp.where(own, q, jnp.where((lane_q % half) < 2 * n_parts, slope_b, jnp.zeros_like(q)))
            own_k = (lane_k < half) if idx == 0 else (lane_k >= half)
            s = lax.dot_general(q_aug, jnp.where(own_k, k, kb), (((1,), (1,)), ((), ())),
                                preferred_element_type=F32)
            if corr is not None:
                s = s + slope[:, 0:1] * corr[0] + corr[1]
            m_prev = m_ref[idx, h]
            m_new = jnp.maximum(m_prev, jnp.max(s, axis=1, keepdims=True))
            alpha = jnp.exp2(m_prev - m_new)
            p = jnp.exp2(s - m_new)
            l_ref[idx, h] = alpha * l_ref[idx, h] + jnp.sum(p, axis=1, keepdims=True)
            acc_ref[idx, h] = alpha * acc_ref[idx, h] + _dot(p.astype(BF16), v)
            m_ref[idx, h] = m_new

    @pl.when(j < ncache)
    def _():
        kb, lane_k = position_lanes(tkc, j * tkc)
        k_heads = jnp.swapaxes(ck_ref[0], 0, 1)
        v_heads = jnp.swapaxes(cv_ref[0], 0, 1)
        for h in range(ATT_N_HEADS):
            head(h, k_heads[h].astype(BF16), v_heads[h].astype(BF16), kb, lane_k, None)

    @pl.when(j == ncache)
    def _():
        kb, lane_k = position_lanes(n_new, past)
        kpos = past + lax.broadcasted_iota(jnp.int32, (tq, n_new), 1)
        qpos = past + lax.broadcasted_iota(jnp.int32, (tq, n_new), 0)
        corr = (jnp.where(kpos > qpos, (2.0 * LOG2E) * (qpos - kpos).astype(F32), 0.0),
                jnp.where(kpos // CHUNK <= qpos // CHUNK, 0.0, NEG_BIG))
        lp = lam_ref[...]
        lam = (jnp.exp(jnp.sum(lp[0:1] * lp[1:2], axis=-1, keepdims=True))
               - jnp.exp(jnp.sum(lp[2:3] * lp[3:4], axis=-1, keepdims=True)) + lambda_init)
        for h in range(ATT_N_HEADS):
            cols = slice(h * LANES, (h + 1) * LANES)
            head(h, kn_ref[0, :, cols], vn_ref[0, :, cols], kb, lane_k, corr)
            o = acc_ref[0, h] / l_ref[0, h] - lam * (acc_ref[1, h] / l_ref[1, h])
            on = o * lax.rsqrt(jnp.mean(o * o, axis=1, keepdims=True) + EPS)
            on = (on * sub_ref[...]) * (1.0 - lambda_init)
            o_ref[0, :, cols] = on.astype(o_ref.dtype)


def diff_attention_cached(q_src, q_colblk, cache_k, cache_v, kv_new, lam_rows, subln, *, tkc, lambda_init):
    b, tq, _ = q_src.shape
    width = ATT_N_HEADS * LANES
    past = cache_k.shape[1]
    n_new = kv_new.shape[1]
    assert past % tkc == 0 and past % CHUNK == 0 and past <= 256 * BIAS_SPLIT
    ncache = past // tkc
    kern = functools.partial(_attn_cache_kernel, tq=tq, tkc=tkc, ncache=ncache, past=past, n_new=n_new,
                             lambda_init=lambda_init)
    slopes = jnp.exp2(-ALIBI_MAX_BIAS * jnp.arange(1, ATT_N_HEADS + 1, dtype=F32) / ATT_N_HEADS)
    slopes = jnp.broadcast_to(slopes[:, None], (ATT_N_HEADS, LANES))
    cache_spec = pl.BlockSpec((1, tkc, ATT_N_HEADS, LANES), lambda bi, j: (bi, jnp.minimum(j, ncache - 1), 0, 0))
    const = lambda shape: pl.BlockSpec(shape, lambda bi, j: (0,) * len(shape))
    return pl.pallas_call(
        kern,
        grid=(b, ncache + 1),
        in_specs=[
            pl.BlockSpec((1, tq, width), lambda bi, j: (bi, 0, q_colblk)),
            cache_spec,
            cache_spec,
            pl.BlockSpec((1, n_new, width), lambda bi, j: (bi, 0, 0)),
            pl.BlockSpec((1, n_new, width), lambda bi, j: (bi, 0, 1)),
            const((ATT_N_HEADS, LANES)),
            const((SUBLANES, ATT_HEAD_DIM)),
            const((1, ATT_V_HEAD)),
        ],
        out_specs=pl.BlockSpec((1, tq, width), lambda bi, j: (bi, 0, 0)),
        out_shape=jax.ShapeDtypeStruct((b, tq, width), BF16),
        scratch_shapes=[
            pltpu.VMEM((2, ATT_N_HEADS, tq, 1), F32),
            pltpu.VMEM((2, ATT_N_HEADS, tq, 1), F32),
            pltpu.VMEM((2, ATT_N_HEADS, tq, ATT_V_HEAD), F32),
        ],
        compiler_params=_cparams(("arbitrary", "arbitrary")),
        name="diff_attention_cached",
    )(q_src, cache_k, cache_v, kv_new, kv_new, slopes, lam_rows, subln.reshape(1, -1))


ROUTE_W = 2 * TOP_K


def _mixer_out_kernel(y_ref, o_ref, gs_ref, ga_ref, x_ref, wos_ref, woa_ref, wout_ref, nf_ref, rw_ref, rb_ref,
                      run0_ref, x2_ref, h_ref, gate_ref, sel_ref, cnt_ref, run_ref, *, tm):
    i = pl.program_id(0)

    @pl.when(i == 0)
    def _():
        run_ref[...] = run0_ref[...]

    o_ssd = _dot(y_ref[...], wos_ref[...])
    o_att = _dot(o_ref[...], woa_ref[...])
    merged = (jax.nn.sigmoid(gs_ref[...].astype(F32)) * o_ssd
              + jax.nn.sigmoid(ga_ref[...].astype(F32)) * o_att)
    x2 = x_ref[...] + _dot(merged.astype(BF16), wout_ref[...])
    x2_ref[...] = x2
    hn = x2 * lax.rsqrt(jnp.mean(x2 * x2, axis=-1, keepdims=True) + EPS) * nf_ref[...]
    h_ref[...] = hn

    a1, a2, _ = _split3(hn)
    w1, w2, _ = _split3(rw_ref[...])
    logits = _dot(a1, w1) + _dot(a1, w2) + _dot(a2, w1) + rb_ref[...]
    lane = lax.broadcasted_iota(jnp.int32, (tm, LANES), 1)
    work = jnp.where(lane < N_EXPERTS, logits, -jnp.inf)
    tops, idxs = [], []
    for _ in range(TOP_K):
        mx = jnp.max(work, axis=-1, keepdims=True)
        ix = jnp.min(jnp.where(work == mx, lane, LANES), axis=-1, keepdims=True)
        tops.append(mx)
        idxs.append(ix)
        work = jnp.where(lane == ix, -jnp.inf, work)
    es = [jnp.exp(tv - tops[0]) for tv in tops]
    den = es[0] + es[1] + es[2] + es[3]
    gates = jnp.zeros((tm, LANES), F32)
    for k in range(TOP_K):
        gates = jnp.where(lane == k, es[k] / den, gates)
    gate_ref[...] = gates[:, :ROUTE_W]

    chosen = jnp.zeros((tm, LANES), jnp.bool_)
    for k in range(TOP_K):
        chosen = jnp.logical_or(chosen, lane == idxs[k])
    multihot = jnp.where(chosen, 1.0, 0.0).astype(BF16)
    ri = lax.broadcasted_iota(jnp.int32, (tm, tm), 0)
    ci = lax.broadcasted_iota(jnp.int32, (tm, tm), 1)
    strict = jnp.where(ci < ri, 1.0, 0.0).astype(BF16)
    prefix = _dot(strict, multihot) + run_ref[...]
    sel = jnp.zeros((tm, LANES), jnp.int32)
    for k in range(TOP_K):
        rank = jnp.sum(jnp.where(lane == idxs[k], prefix, 0.0), axis=-1, keepdims=True)
        sel = jnp.where(lane == k, idxs[k], sel)
        sel = jnp.where(lane == TOP_K + k, rank.astype(jnp.int32), sel)
    sel_ref[...] = sel[:, :ROUTE_W]
    run_ref[...] = run_ref[...] + jnp.sum(multihot.astype(F32), axis=0, keepdims=True)
    cnt_ref[...] = run_ref[...]


def mixer_out(y_ssd, o_att, gates_src, gs_col, ga_col, x, w_o_ssd, w_o_att, w_out, norm_ffn, rw, rb, counts0, tm):
    t, d = x.shape
    assert t % tm == 0
    kern = functools.partial(_mixer_out_kernel, tm=tm)
    const = lambda shape: pl.BlockSpec(shape, lambda i: (0, 0))
    return pl.pallas_call(
        kern,
        grid=(t // tm,),
        in_specs=[
            pl.BlockSpec((tm, SSD_D_INNER), lambda i: (i, 0)),
            pl.BlockSpec((tm, d), lambda i: (i, 0)),
            pl.BlockSpec((tm, d), lambda i: (i, gs_col)),
            pl.BlockSpec((tm, d), lambda i: (i, ga_col)),
            pl.BlockSpec((tm, d), lambda i: (i, 0)),
            const((SSD_D_INNER, d)),
            const((d, d)),
            const((d, d)),
            const((1, d)),
            const((d, LANES)),
            const((1, LANES)),
            const((1, LANES)),
        ],
        out_specs=[
            pl.BlockSpec((tm, d), lambda i: (i, 0)),
            pl.BlockSpec((tm, d), lambda i: (i, 0)),
            pl.BlockSpec((tm, ROUTE_W), lambda i: (i, 0)),
            pl.BlockSpec((tm, ROUTE_W), lambda i: (i, 0)),
            pl.BlockSpec((1, LANES), lambda i: (0, 0)),
        ],
        out_shape=[
            jax.ShapeDtypeStruct((t, d), F32),
            jax.ShapeDtypeStruct((t, d), F32),
            jax.ShapeDtypeStruct((t, ROUTE_W), F32),
            jax.ShapeDtypeStruct((t, ROUTE_W), jnp.int32),
            jax.ShapeDtypeStruct((1, LANES), F32),
        ],
        scratch_shapes=[pltpu.VMEM((1, LANES), F32)],
        compiler_params=_cparams(("arbitrary",)),
        name="mixer_out",
    )(y_ssd, o_att, gates_src, gates_src, x, w_o_ssd, w_o_att, w_out, norm_ffn.reshape(1, d), rw, rb, counts0)


MOE_ROWS = 256
ROUTE_TOK = 256


def _dispatch_kernel(ps_ref, pl_ref, nu_ref, dest_ref, hp_ref, hs_ref, xs_ref, zero_ref, sem,
                     *, prompt_tiles, n_tiles, n_blocks):
    i = pl.program_id(0)

    def zero_copies(action):
        def per_expert(e, c):
            def row(r, c2):
                action(pltpu.make_async_copy(zero_ref.at[pl.ds(0, 1)], xs_ref.at[pl.ds(ps_ref[e] + r, 1)],
                                             sem.at[1]))
                return c2
            return lax.fori_loop(0, pl_ref[e], row, c)

        lax.fori_loop(0, N_EXPERTS, per_expert, 0)

        def tail(b, c):
            row0 = pl.multiple_of(b * MOE_ROWS, MOE_ROWS)
            action(pltpu.make_async_copy(zero_ref, xs_ref.at[pl.ds(row0, MOE_ROWS)], sem.at[1]))
            return c

        lax.fori_loop(nu_ref[0], n_blocks, tail, 0)

    @pl.when(i == 0)
    def _():
        zero_ref[...] = jnp.zeros(zero_ref.shape, F32)
        zero_copies(lambda cp: cp.start())

    def scatter(src_ref):
        def body(g, c):
            t0 = pl.multiple_of(g * SUBLANES, SUBLANES)
            for r in range(SUBLANES):
                for k in range(TOP_K):
                    d = dest_ref[0, 0, (t0 + r) * TOP_K + k]
                    pltpu.make_async_copy(src_ref.at[pl.ds(t0 + r, 1)], xs_ref.at[pl.ds(d, 1)], sem.at[0]).start(
                        priority=k % 2)
            return c
        lax.fori_loop(0, ROUTE_TOK // SUBLANES, body, 0)
        for k in range(TOP_K):
            pltpu.make_async_copy(src_ref, xs_ref.at[pl.ds(0, ROUTE_TOK)], sem.at[0]).wait()

    @pl.when(i < prompt_tiles)
    def _():
        scatter(hp_ref)

    @pl.when(i >= prompt_tiles)
    def _():
        scatter(hs_ref)

    @pl.when(i == n_tiles - 1)
    def _():
        zero_copies(lambda cp: cp.wait())


def moe_dispatch(dest, hn_p, hn_s, pad_start, pad_len, n_used, n_blocks):
    tp, d = hn_p.shape
    tsn = hn_s.shape[0]
    assert tp % ROUTE_TOK == 0 and tsn % ROUTE_TOK == 0
    prompt_tiles = tp // ROUTE_TOK
    n_tiles = prompt_tiles + tsn // ROUTE_TOK
    kern = functools.partial(_dispatch_kernel, prompt_tiles=prompt_tiles, n_tiles=n_tiles, n_blocks=n_blocks)
    return pl.pallas_call(
        kern,
        grid_spec=pltpu.PrefetchScalarGridSpec(
            num_scalar_prefetch=3,
            grid=(n_tiles,),
            in_specs=[
                pl.BlockSpec((1, 1, ROUTE_TOK * TOP_K), lambda i, a, b, c: (i, 0, 0), memory_space=pltpu.SMEM),
                pl.BlockSpec((ROUTE_TOK, d), lambda i, a, b, c: (jnp.minimum(i, prompt_tiles - 1), 0)),
                pl.BlockSpec((ROUTE_TOK, d), lambda i, a, b, c: (jnp.maximum(i - prompt_tiles, 0), 0)),
            ],
            out_specs=pl.BlockSpec(memory_space=pl.ANY),
            scratch_shapes=[pltpu.VMEM((MOE_ROWS, d), F32), pltpu.SemaphoreType.DMA((2,))],
        ),
        out_shape=jax.ShapeDtypeStruct((n_blocks * MOE_ROWS, d), F32),
        compiler_params=_cparams(("arbitrary",)),
        name="moe_dispatch",
    )(pad_start, pad_len, n_used, dest.reshape(n_tiles, 1, ROUTE_TOK * TOP_K), hn_p, hn_s)


def _moe_kernel(be_ref, nu_ref, nxt_ref, x_ref, wgu_hbm, bgu_ref, wd_hbm, bd_ref, o_ref,
                wgu32_ref, wd32_ref, wgu16_ref, wd16_ref, seg_ref, sem):
    b = pl.program_id(0)
    used = b < nu_ref[0]
    new_expert = jnp.logical_or(b == 0, be_ref[b] != be_ref[jnp.maximum(b - 1, 0)])

    def weight_copies(e, slot):
        return (pltpu.make_async_copy(wgu_hbm.at[e], wgu32_ref.at[slot], sem.at[slot, 0]),
                pltpu.make_async_copy(wd_hbm.at[e], wd32_ref.at[slot], sem.at[slot, 1]))

    @pl.when(b == 0)
    def _():
        seg_ref[0] = 0
        for cp in weight_copies(be_ref[0], 0):
            cp.start()

    @pl.when(jnp.logical_and(used, new_expert))
    def _():
        slot = seg_ref[0] % 2
        nxt = nxt_ref[b]

        @pl.when(nxt < nu_ref[0])
        def _():
            for cp in weight_copies(be_ref[nxt], 1 - slot):
                cp.start()

        for cp in weight_copies(be_ref[b], slot):
            cp.wait()
        wgu16_ref[...] = wgu32_ref[slot].astype(BF16)
        wd16_ref[...] = wd32_ref[slot].astype(BF16)
        seg_ref[0] = seg_ref[0] + 1

    @pl.when(used)
    def _():
        x = x_ref[...].astype(BF16)
        gu = _dot(x, wgu16_ref[...]) + bgu_ref[0]
        d_ff = gu.shape[1] // 2
        gate = jnp.minimum(gu[:, :d_ff], SWIGLU_LIMIT)
        up = jnp.clip(gu[:, d_ff:], -SWIGLU_LIMIT, SWIGLU_LIMIT)
        act = (up + 1.0) * gate * jax.nn.sigmoid(SWIGLU_ALPHA * gate)
        o_ref[...] = _dot(act.astype(BF16), wd16_ref[...]) + bd_ref[0]

    @pl.when(jnp.logical_not(used))
    def _():
        o_ref[...] = jnp.zeros(o_ref.shape, F32)


def moe_experts(xs, block_e, n_used, next_first, w_gu, b_gu, w_down, b_down):
    rows, d = xs.shape
    nb = rows // MOE_ROWS
    e, _, gu_w = w_gu.shape
    last = lambda b, nu: jnp.minimum(b, nu[0] - 1)
    return pl.pallas_call(
        _moe_kernel,
        grid_spec=pltpu.PrefetchScalarGridSpec(
            num_scalar_prefetch=3,
            grid=(nb,),
            in_specs=[
                pl.BlockSpec((MOE_ROWS, d), lambda b, be, nu, nx: (last(b, nu), 0)),
                pl.BlockSpec(memory_space=pl.ANY),
                pl.BlockSpec((1, 1, gu_w), lambda b, be, nu, nx: (be[last(b, nu)], 0, 0)),
                pl.BlockSpec(memory_space=pl.ANY),
                pl.BlockSpec((1, 1, d), lambda b, be, nu, nx: (be[last(b, nu)], 0, 0)),
            ],
            out_specs=pl.BlockSpec((MOE_ROWS, d), lambda b, be, nu, nx: (b, 0)),
            scratch_shapes=[
                pltpu.VMEM((2, d, gu_w), F32),
                pltpu.VMEM((2, gu_w // 2, d), F32),
                pltpu.VMEM((d, gu_w), BF16),
                pltpu.VMEM((gu_w // 2, d), BF16),
                pltpu.SMEM((1,), jnp.int32),
                pltpu.SemaphoreType.DMA((2, 2)),
            ],
        ),
        out_shape=jax.ShapeDtypeStruct((rows, d), F32),
        compiler_params=_cparams(("arbitrary",)),
        name="moe_experts",
    )(block_e, n_used, next_first, xs, w_gu, b_gu.reshape(e, 1, gu_w), w_down, b_down.reshape(e, 1, d))


def _combine_kernel(dfirst_ref, dnext_ref, x2_ref, gate_ref, g_ref, ys_ref, o_ref, buf_ref, sem, *, n_tiles):
    i = pl.program_id(0)
    slot = i % 2

    def issue(dref, sl):
        def body(g, c):
            t0 = pl.multiple_of(g * SUBLANES, SUBLANES)
            for r in range(SUBLANES):
                for k in range(TOP_K):
                    d = dref[0, 0, (t0 + r) * TOP_K + k]
                    pltpu.make_async_copy(ys_ref.at[pl.ds(d, 1)], buf_ref.at[sl, k, pl.ds(t0 + r, 1)],
                                          sem.at[sl]).start(priority=k % 2)
            return c
        lax.fori_loop(0, ROUTE_TOK // SUBLANES, body, 0)

    @pl.when(i == 0)
    def _():
        issue(dfirst_ref, 0)

    @pl.when(i + 1 < n_tiles)
    def _():
        issue(dnext_ref, 1 - slot)

    for k in range(TOP_K):
        pltpu.make_async_copy(ys_ref.at[pl.ds(0, ROUTE_TOK)], buf_ref.at[slot, k], sem.at[slot]).wait()

    gates = gate_ref[...]
    moe = buf_ref[slot, 0] * gates[:, 0:1]
    for k in range(1, TOP_K):
        moe = moe + buf_ref[slot, k] * gates[:, k:k + 1]
    tok = x2_ref[...] + moe
    y = tok * lax.rsqrt(jnp.mean(tok * tok, axis=-1, keepdims=True) + EPS)
    o_ref[...] = y * g_ref[...]


def combine(x2, ys, dest, gates, norm_final):
    t, d = x2.shape
    assert t % ROUTE_TOK == 0
    n_tiles = t // ROUTE_TOK
    dest3 = dest.reshape(n_tiles, 1, ROUTE_TOK * TOP_K)
    kern = functools.partial(_combine_kernel, n_tiles=n_tiles)
    smem = lambda imap: pl.BlockSpec((1, 1, ROUTE_TOK * TOP_K), imap, memory_space=pltpu.SMEM)
    return pl.pallas_call(
        kern,
        grid=(n_tiles,),
        in_specs=[
            smem(lambda i: (0, 0, 0)),
            smem(lambda i: (jnp.minimum(i + 1, n_tiles - 1), 0, 0)),
            pl.BlockSpec((ROUTE_TOK, d), lambda i: (i, 0)),
            pl.BlockSpec((ROUTE_TOK, ROUTE_W), lambda i: (i, 0)),
            pl.BlockSpec((1, d), lambda i: (0, 0)),
            pl.BlockSpec(memory_space=pl.ANY),
        ],
        out_specs=pl.BlockSpec((ROUTE_TOK, d), lambda i: (i, 0)),
        out_shape=jax.ShapeDtypeStruct((t, d), F32),
        scratch_shapes=[pltpu.VMEM((2, TOP_K, ROUTE_TOK, d), F32), pltpu.SemaphoreType.DMA((2,))],
        compiler_params=_cparams(("arbitrary",)),
        name="combine",
    )(dest3, dest3, x2, gates, norm_final.reshape(1, d), ys)


def kernel(x_prompt, x_sample, cache_k, cache_v, state_ssm, state_conv, norm_mix, w_in, conv_w, conv_b, dt_bias, a_log, d_skip, ssd_norm, w_o_ssd, lambda_q1, lambda_k1, lambda_q2, lambda_k2, subln, w_o_att, w_out, norm_ffn, router_w, router_b, w_gu, b_gu, w_down, b_down, norm_final):
    bp, s, d = x_prompt.shape
    bs, ts, _ = x_sample.shape
    past = cache_k.shape[2]
    depth = w_in.shape[0]
    assert depth == 1
    layer = 0
    lambda_init = 0.8 - 0.6 * math.exp(-0.3 * layer)
    tp, tsn = bp * s, bs * ts
    t_all = tp + tsn

    heads = ATT_N_HEADS
    xp2 = x_prompt.reshape(tp, d)
    xs2 = x_sample.reshape(tsn, d)

    sizes = (SSD_D_INNER, SSD_CONV_CH, SSD_N_HEADS, d, d, d, d, d)
    offs = [0]
    for sz in sizes:
        offs.append(offs[-1] + sz)
    w = w_in[layer].astype(BF16)
    seg = lambda i: w[:, offs[i]:offs[i + 1]]
    dt_pad = jnp.zeros((d, PROJ_XD_W - SSD_CONV_CH - SSD_N_HEADS), BF16)
    w_all = jnp.concatenate([seg(4), seg(5), seg(0), seg(3), seg(6), seg(7), seg(1), seg(2), dt_pad], axis=1)
    k_p, v_p, kv16_p, zq_p, xd_p = in_proj(xp2, norm_mix[layer], w_all, tm=2048)
    k_s, v_s, kv16_s, zq_s, xd_s = in_proj(xs2, norm_mix[layer], w_all, tm=tsn)

    prev_p = jnp.zeros((bp, SUBLANES, SSD_CONV_CH), F32)
    h0_p = jnp.zeros((bp, SSD_D_STATE, SSD_D_INNER), F32)
    ssd_w = (conv_w[layer], conv_b[layer], dt_bias[layer], a_log[layer], d_skip[layer], ssd_norm[layer])
    y_p, hT_p = ssd_mixer(xd_p, zq_p, 0, prev_p, h0_p, *ssd_w, b=bp, l=s, tl=256, valid_len=None, out_rows=tp)

    pad_rows = CHUNK - ts
    pad_seq = lambda a: jnp.pad(a.reshape(bs, ts, -1), ((0, 0), (0, pad_rows), (0, 0))).reshape(bs * CHUNK, -1)
    prev_s = jnp.pad(state_conv[layer], ((0, 0), (SUBLANES - (SSD_CONV_WIDTH - 1), 0), (0, 0)))
    h0_s = jnp.swapaxes(state_ssm[layer].reshape(bs, SSD_D_INNER, SSD_D_STATE), 1, 2)
    y_s, hT_s = ssd_mixer(pad_seq(xd_s), pad_seq(zq_s[:, :SSD_D_INNER]), 0, prev_s, h0_s, *ssd_w, b=bs, l=CHUNK,
                          tl=CHUNK, valid_len=ts, out_rows=bs * CHUNK)
    y_s = y_s.reshape(bs, CHUNK, -1)[:, :ts].reshape(tsn, -1)

    lam_rows = jnp.concatenate([lambda_q1[layer][None], lambda_k1[layer][None], lambda_q2[layer][None],
                                lambda_k2[layer][None], jnp.zeros((4, ATT_HEAD_DIM), F32)], axis=0)
    o_p = diff_attention_t(zq_p, 0, SSD_D_INNER // d, kv16_p, 0, kv16_p, 1, lam_rows, subln[layer], b=bp, lq=s,
                           lk=s, tq=512, tk=1024, q_off=0, kv_len=s, lambda_init=lambda_init, out_rows=tp)
    o_s = diff_attention_cached(zq_s.reshape(bs, ts, -1), SSD_D_INNER // d, cache_k[layer], cache_v[layer],
                                kv16_s.reshape(bs, ts, 2 * d), lam_rows, subln[layer], tkc=1024,
                                lambda_init=lambda_init).reshape(tsn, d)

    rw = jnp.zeros((d, LANES), F32).at[:, :N_EXPERTS].set(router_w[layer])
    rb = jnp.zeros((1, LANES), F32).at[0, :N_EXPERTS].set(router_b[layer])
    mix_w = (w_o_ssd[layer].astype(BF16), w_o_att[layer].astype(BF16), w_out[layer].astype(BF16), norm_ffn[layer],
             rw, rb)
    x2_p, hn_p, gate_p, sel_p, cnt_p = mixer_out(y_p, o_p, zq_p, 3, 4, xp2, *mix_w,
                                                 jnp.zeros((1, LANES), F32), tm=512)
    x2_s, hn_s, gate_s, sel_s, cnt_all = mixer_out(y_s, o_s, zq_s, 3, 4, xs2, *mix_w, cnt_p, tm=tsn)

    sel = jnp.concatenate([sel_p, sel_s], axis=0)
    top_e = sel[:, :TOP_K]
    rank = sel[:, TOP_K:]
    cnt = cnt_all[0, :N_EXPERTS].astype(jnp.int32)
    padded = (cnt + MOE_ROWS - 1) // MOE_ROWS * MOE_ROWS
    ends = jnp.cumsum(padded)
    starts = ends - padded
    dest = (starts[top_e] + rank).reshape(-1)
    nb = (t_all * TOP_K + N_EXPERTS * (MOE_ROWS - 1) + MOE_ROWS - 1) // MOE_ROWS
    block_start = jnp.arange(nb, dtype=jnp.int32) * MOE_ROWS
    block_e = jnp.minimum(jnp.sum((ends[None, :] <= block_start[:, None]).astype(jnp.int32), axis=1),
                          N_EXPERTS - 1)
    n_used = (ends[-1] // MOE_ROWS).astype(jnp.int32).reshape(1)

    xs_sorted = moe_dispatch(dest, hn_p, hn_s, starts + cnt, padded - cnt, n_used, nb)
    next_first = ends[block_e] // MOE_ROWS
    ys = moe_experts(xs_sorted, block_e, n_used, next_first, w_gu[layer], b_gu[layer], w_down[layer],
                     b_down[layer])
    y_prompt = combine(x2_p, ys, dest[:tp * TOP_K], gate_p, norm_final).reshape(bp, s, d)
    y_sample = combine(x2_s, ys, dest[tp * TOP_K:], gate_s, norm_final).reshape(bs, ts, d)

    new_k_p = k_p.reshape(1, bp, s, heads, 2 * ATT_HEAD_DIM)
    new_v_p = v_p.reshape(1, bp, s, heads, ATT_V_HEAD)
    new_k_s = k_s.reshape(1, bs, ts, heads, 2 * ATT_HEAD_DIM)
    new_v_s = v_s.reshape(1, bs, ts, heads, ATT_V_HEAD)
    ssm_p = jnp.swapaxes(hT_p, 1, 2).reshape(1, bp, SSD_N_HEADS, SSD_HEAD_DIM, SSD_D_STATE)
    ssm_s = jnp.swapaxes(hT_s, 1, 2).reshape(1, bs, SSD_N_HEADS, SSD_HEAD_DIM, SSD_D_STATE)
    keep = SSD_CONV_WIDTH - 1
    conv_p = xd_p.reshape(bp, s, -1)[:, s - keep:, :SSD_CONV_CH][None]
    raw_s = jnp.concatenate([state_conv[layer], xd_s.reshape(bs, ts, -1)[:, :, :SSD_CONV_CH]], axis=1)
    conv_s = raw_s[:, -keep:][None]
    return (y_prompt, y_sample, new_k_p, new_v_p, ssm_p, conv_p, new_k_s, new_v_s, ssm_s, conv_s)
```

```python
import functools
import math

import jax
import jax.numpy as jnp
from jax import lax
from jax.experimental import pallas as pl
from jax.experimental.pallas import tpu as pltpu

EPS = 1e-5
CHUNK = 64
D_MODEL = 1024
SSD_D_INNER = 2048
SSD_HEAD_DIM = 64
SSD_N_HEADS = 32
SSD_GROUPS = 4
SSD_D_STATE = 128
SSD_CONV_WIDTH = 4
SSD_CONV_CH = 3072
ATT_HEAD_DIM = 64
ATT_N_HEADS = 8
ATT_V_HEAD = 128
ALIBI_MAX_BIAS = 8.0
N_EXPERTS = 32
TOP_K = 4
SWIGLU_LIMIT = 7.0
SWIGLU_ALPHA = 1.702

LANES = 128
SUBLANES = 8
VMEM_LIMIT = 56 * 1024 * 1024
NEG_BIG = -1e30

BF16 = jnp.bfloat16
F32 = jnp.float32


def _cparams(sem, flags=None):
    return pltpu.CompilerParams(dimension_semantics=sem, vmem_limit_bytes=VMEM_LIMIT, flags=flags)


def _split3(x):
    h1 = x.astype(BF16)
    r1 = x - h1.astype(F32)
    h2 = r1.astype(BF16)
    h3 = (r1 - h2.astype(F32)).astype(BF16)
    return h1, h2, h3


def _dot(a, b):
    return jnp.dot(a, b, preferred_element_type=F32)


def _dot_exact_rhs(x, m):
    h1, h2, h3 = _split3(x)
    return _dot(h1, m) + _dot(h2, m) + _dot(h3, m)


def _dot_exact_lhs(m, x):
    h1, h2, h3 = _split3(x)
    return _dot(m, h1) + _dot(m, h2) + _dot(m, h3)


PROJ_TN = 256
PROJ_K = (0, D_MODEL // PROJ_TN)
PROJ_V = (PROJ_K[1], PROJ_K[1] + D_MODEL // PROJ_TN)
PROJ_ZQ_W = SSD_D_INNER + 3 * D_MODEL
PROJ_ZQ = (PROJ_V[1], PROJ_V[1] + PROJ_ZQ_W // PROJ_TN)
PROJ_XD_W = -(-(SSD_CONV_CH + SSD_N_HEADS) // PROJ_TN) * PROJ_TN
PROJ_XD = (PROJ_ZQ[1], PROJ_ZQ[1] + PROJ_XD_W // PROJ_TN)


def _in_proj_kernel(x_ref, g_ref, w_ref, k_ref, v_ref, kv16_ref, zq_ref, xd_ref, xn_ref):
    j = pl.program_id(1)

    @pl.when(j == 0)
    def _():
        x = x_ref[...]
        y = x * lax.rsqrt(jnp.mean(x * x, axis=-1, keepdims=True) + EPS)
        xn_ref[...] = (y * g_ref[...]).astype(BF16)

    @pl.when(j < PROJ_K[1])
    def _():
        acc = _dot(xn_ref[...], w_ref[...])
        k_ref[...] = acc
        kv16_ref[...] = acc.astype(BF16)

    @pl.when(jnp.logical_and(j >= PROJ_V[0], j < PROJ_V[1]))
    def _():
        acc = _dot(xn_ref[...], w_ref[...])
        v_ref[...] = acc
        kv16_ref[...] = acc.astype(BF16)

    @pl.when(jnp.logical_and(j >= PROJ_ZQ[0], j < PROJ_ZQ[1]))
    def _():
        zq_ref[...] = _dot(xn_ref[...], w_ref[...]).astype(BF16)

    @pl.when(j >= PROJ_XD[0])
    def _():
        xd_ref[...] = _dot(xn_ref[...], w_ref[...])


def in_proj(x, gain, w, tm):
    t, d = x.shape
    assert t % tm == 0 and w.shape[1] == PROJ_XD[1] * PROJ_TN
    tn = PROJ_TN

    def out_map(rng):
        return lambda i, j: (i, jnp.clip(j - rng[0], 0, rng[1] - rng[0] - 1))

    return pl.pallas_call(
        _in_proj_kernel,
        grid=(t // tm, PROJ_XD[1]),
        in_specs=[
            pl.BlockSpec((tm, d), lambda i, j: (i, 0)),
            pl.BlockSpec((1, d), lambda i, j: (0, 0)),
            pl.BlockSpec((d, tn), lambda i, j: (0, j)),
        ],
        out_specs=[
            pl.BlockSpec((tm, tn), out_map(PROJ_K)),
            pl.BlockSpec((tm, tn), out_map(PROJ_V)),
            pl.BlockSpec((tm, tn), out_map((PROJ_K[0], PROJ_V[1]))),
            pl.BlockSpec((tm, tn), out_map(PROJ_ZQ)),
            pl.BlockSpec((tm, tn), out_map(PROJ_XD)),
        ],
        out_shape=[
            jax.ShapeDtypeStruct((t, d), F32),
            jax.ShapeDtypeStruct((t, d), F32),
            jax.ShapeDtypeStruct((t, 2 * d), BF16),
            jax.ShapeDtypeStruct((t, PROJ_ZQ_W), BF16),
            jax.ShapeDtypeStruct((t, PROJ_XD_W), F32),
        ],
        scratch_shapes=[pltpu.VMEM((tm, d), BF16)],
        compiler_params=_cparams(("arbitrary", "arbitrary")),
        name="in_proj",
    )(x, gain.reshape(1, d), w)


CONV_ROWS, CONV_COLS = 64, 512


def _ssd_kernel(xbc_ref, z_ref, prev_ref, h0_ref, cw_ref, cb_ref, dtb_ref, alog_ref, dsk_ref, gn_ref,
                y_ref, hout_ref, xpad_ref, xc_ref, tail_ref, st_ref, *, tl, valid_len):
    t = pl.program_id(1)
    nt = pl.num_programs(1)
    nch = tl // CHUNK
    dinner = SSD_D_INNER
    gw = dinner // SSD_GROUPS
    ns = SSD_D_STATE

    @pl.when(t == 0)
    def _():
        tail_ref[...] = prev_ref[0]
        st_ref[...] = h0_ref[0]

    xpad_ref[0:SUBLANES, :] = tail_ref[...]
    xpad_ref[SUBLANES:SUBLANES + tl, :] = xbc_ref[:, 0:SSD_CONV_CH]
    tail_ref[...] = xbc_ref[tl - SUBLANES:tl, 0:SSD_CONV_CH]
    for r in range(0, tl, CONV_ROWS):
        for c in range(0, SSD_CONV_CH, CONV_COLS):
            acc = cb_ref[:, c:c + CONV_COLS]
            for k in range(SSD_CONV_WIDTH):
                off = r + SUBLANES - (SSD_CONV_WIDTH - 1) + k
                acc = acc + xpad_ref[off:off + CONV_ROWS, c:c + CONV_COLS] * cw_ref[k:k + 1, c:c + CONV_COLS]
            xc_ref[r:r + CONV_ROWS, c:c + CONV_COLS] = acc * jax.nn.sigmoid(acc)

    head_of_lane = lax.broadcasted_iota(jnp.int32, (SSD_N_HEADS, dinner), 1) // SSD_HEAD_DIM
    expand = (head_of_lane == lax.broadcasted_iota(jnp.int32, (SSD_N_HEADS, dinner), 0)).astype(BF16)
    ti = lax.broadcasted_iota(jnp.int32, (CHUNK, CHUNK), 0)
    si = lax.broadcasted_iota(jnp.int32, (CHUNK, CHUNK), 1)
    tril = (si <= ti).astype(BF16)
    row_c = lax.broadcasted_iota(jnp.int32, (CHUNK, dinner), 0)
    pos_in_head = lax.broadcasted_iota(jnp.int32, (CHUNK, dinner), 1) % CHUNK
    upper = row_c <= pos_in_head
    row_p = lax.broadcasted_iota(jnp.int32, (CHUNK, LANES), 0)
    lane_p = lax.broadcasted_iota(jnp.int32, (CHUNK, LANES), 1)
    causal_pair = (lane_p % CHUNK) <= row_p
    left_half = lane_p < SSD_HEAD_DIM
    a_neg_e = _dot_exact_rhs(-jnp.exp(alog_ref[...]), expand)
    dsk_e = _dot_exact_rhs(dsk_ref[...], expand)

    def chunk_body(c, carry):
        r0 = pl.multiple_of(c * CHUNK, CHUNK)
        xs = xc_ref[pl.ds(r0, CHUNK), 0:dinner]
        dt_raw = xbc_ref[pl.ds(r0, CHUNK), SSD_CONV_CH:SSD_CONV_CH + SSD_N_HEADS]
        dtv = dt_raw + dtb_ref[...]
        dt = jnp.maximum(dtv, 0.0) + jnp.log1p(jnp.exp(-jnp.abs(dtv)))
        if valid_len is not None:
            rows = t * tl + r0 + lax.broadcasted_iota(jnp.int32, (CHUNK, SSD_N_HEADS), 0)
            dt = jnp.where(rows < valid_len, dt, 0.0)
        dt_e = _dot_exact_rhs(dt, expand)
        a_e = dt_e * a_neg_e
        acs_e = _dot_exact_lhs(tril, a_e)
        rowterm = jnp.sum(jnp.where(upper, a_e, 0.0), axis=0, keepdims=True)
        acs_last = acs_e[CHUNK - 1:CHUNK, :]
        xdt = xs * dt_e
        x_dec = (xdt * jnp.exp(acs_last - acs_e)).astype(BF16)
        e_acs = jnp.exp(acs_e)
        e_last = jnp.exp(acs_last)

        y_parts = []
        for g in range(SSD_GROUPS):
            lo = g * gw
            bm = xc_ref[pl.ds(r0, CHUNK), dinner + g * ns:dinner + (g + 1) * ns].astype(BF16)
            cm = xc_ref[pl.ds(r0, CHUNK), dinner + SSD_GROUPS * ns + g * ns:
                        dinner + SSD_GROUPS * ns + (g + 1) * ns].astype(BF16)
            b2 = jnp.concatenate([bm, bm], axis=0)
            cb2 = lax.dot_general(cm, b2, (((1,), (1,)), ((), ())), preferred_element_type=F32)
            st_g = st_ref[:, lo:lo + gw]
            y_off = _dot(cm, st_g.astype(BF16)) * e_acs[:, lo:lo + gw]
            pieces = []
            for j in range(gw // LANES):
                l0 = lo + j * LANES
                diff = acs_e[:, l0:l0 + LANES] - rowterm[:, l0:l0 + LANES]
                dec = jnp.where(causal_pair, jnp.exp(diff), 0.0)
                scores = (cb2 * dec).astype(BF16)
                xp = xdt[:, l0:l0 + LANES]
                xblk = jnp.concatenate([jnp.where(left_half, xp, 0.0), jnp.where(left_half, 0.0, xp)],
                                       axis=0).astype(BF16)
                pieces.append(_dot(scores, xblk))
            y_diag = jnp.concatenate(pieces, axis=1)
            upd = lax.dot_general(bm, x_dec[:, lo:lo + gw], (((0,), (0,)), ((), ())),
                                  preferred_element_type=F32)
            st_ref[:, lo:lo + gw] = e_last[:, lo:lo + gw] * st_g + upd
            yg = y_diag + y_off + xs[:, lo:lo + gw] * dsk_e[:, lo:lo + gw]
            zg = z_ref[pl.ds(r0, CHUNK), lo:lo + gw].astype(F32)
            yg = yg * (zg * jax.nn.sigmoid(zg))
            yn = yg * lax.rsqrt(jnp.mean(yg * yg, axis=-1, keepdims=True) + EPS)
            y_parts.append((yn * gn_ref[:, lo:lo + gw]).astype(y_ref.dtype))
        y_ref[pl.ds(r0, CHUNK), :] = jnp.concatenate(y_parts, axis=1)
        return carry

    lax.fori_loop(0, nch, chunk_body, 0, unroll=True)

    @pl.when(t == nt - 1)
    def _():
        hout_ref[0] = st_ref[...]


def ssd_mixer(xbc, z_src, z_col, conv_prev8, h0_t, conv_w, conv_b, dt_bias, a_log, d_skip, ssd_norm,
              b, l, tl, valid_len, out_rows):
    wx = xbc.shape[1]
    assert l % tl == 0 and tl % CHUNK == 0 and CHUNK == SSD_HEAD_DIM
    nt = l // tl
    kern = functools.partial(_ssd_kernel, tl=tl, valid_len=valid_len)
    full = lambda shape: pl.BlockSpec(shape, lambda i, j: (0,) * len(shape))
    return pl.pallas_call(
        kern,
        grid=(b, nt),
        in_specs=[
            pl.BlockSpec((tl, wx), lambda i, j: (i * nt + j, 0)),
            pl.BlockSpec((tl, SSD_D_INNER), lambda i, j: (i * nt + j, z_col)),
            pl.BlockSpec((1, SUBLANES, SSD_CONV_CH), lambda i, j: (i, 0, 0)),
            pl.BlockSpec((1, SSD_D_STATE, SSD_D_INNER), lambda i, j: (i, 0, 0)),
            full((SSD_CONV_WIDTH, SSD_CONV_CH)),
            full((1, SSD_CONV_CH)),
            full((1, SSD_N_HEADS)),
            full((1, SSD_N_HEADS)),
            full((1, SSD_N_HEADS)),
            full((1, SSD_D_INNER)),
        ],
        out_specs=[
            pl.BlockSpec((tl, SSD_D_INNER), lambda i, j: (i * nt + j, 0)),
            pl.BlockSpec((1, SSD_D_STATE, SSD_D_INNER), lambda i, j: (i, 0, 0)),
        ],
        out_shape=[
            jax.ShapeDtypeStruct((out_rows, SSD_D_INNER), BF16),
            jax.ShapeDtypeStruct((b, SSD_D_STATE, SSD_D_INNER), F32),
        ],
        scratch_shapes=[
            pltpu.VMEM((tl + SUBLANES, SSD_CONV_CH), F32),
            pltpu.VMEM((tl, SSD_CONV_CH), F32),
            pltpu.VMEM((SUBLANES, SSD_CONV_CH), F32),
            pltpu.VMEM((SSD_D_STATE, SSD_D_INNER), F32),
        ],
        compiler_params=_cparams(("arbitrary", "arbitrary")),
        name="ssd_mixer",
    )(xbc, z_src, conv_prev8, h0_t, conv_w, conv_b.reshape(1, -1), dt_bias.reshape(1, -1),
      a_log.reshape(1, -1), d_skip.reshape(1, -1), ssd_norm.reshape(1, -1))


BIAS_SPLIT = 32
LOG2E = 1.4426950408889634
LOG2E_PARTS = (1.4453125, -0.00262451171875, 7.063150405883789e-06)
FLAG_FIRST, FLAG_LAST, VARIANT_SHIFT = 1, 2, 2
ATT_SUB = 256
MODE_PLAIN, MODE_MASKED, MODE_SKIP = "plain", "masked", "skip"


def _attn_sub(tk):
    nsub = tk // ATT_SUB if tk % ATT_SUB == 0 else 1
    return nsub, tk // nsub


def _attn_t_kernel(qi_ref, ki_ref, fl_ref, q_ref, k_ref, v_ref, slope_ref, lam_ref, sub_ref, o_ref,
                   m_ref, l_ref, acc_ref, kb_ref, corr_ref, s0_ref, s1_ref, mt0_ref, mt1_ref,
                   *, tq, tk, q_off, kv_len, lambda_init, variants):
    p_idx = pl.program_id(1)
    qi = qi_ref[p_idx]
    ki = ki_ref[p_idx]
    flags = fl_ref[p_idx]
    qstart = q_off + qi * tq
    kstart = ki * tk
    half = ATT_HEAD_DIM
    scale = ATT_HEAD_DIM ** -0.5

    @pl.when((flags & FLAG_FIRST) != 0)
    def _():
        m_ref[...] = jnp.full(m_ref.shape, NEG_BIG, F32)
        l_ref[...] = jnp.zeros(l_ref.shape, F32)
        acc_ref[...] = jnp.zeros(acc_ref.shape, F32)

    lane_k = lax.broadcasted_iota(jnp.int32, (tk, LANES), 1)
    rel = kstart - qstart + lax.broadcasted_iota(jnp.int32, (tk, LANES), 0)
    hi = (rel // BIAS_SPLIT) * BIAS_SPLIT
    lo = rel - hi
    pos_lane = lane_k % half
    n_parts = len(LOG2E_PARTS)
    kb_ref[...] = jnp.where(pos_lane < n_parts, hi, jnp.where(pos_lane < 2 * n_parts, lo, 0)).astype(F32).astype(BF16)

    variant = flags >> VARIANT_SHIFT
    nsub, ts = _attn_sub(tk)

    def mask_terms(modes):
        for j, mode in enumerate(modes):
            if mode != MODE_MASKED:
                continue
            r0 = j * ts
            kpos = kstart + r0 + lax.broadcasted_iota(jnp.int32, (ts, tq), 0)
            qpos = qstart + lax.broadcasted_iota(jnp.int32, (ts, tq), 1)
            allowed = jnp.logical_and(kpos // CHUNK <= qpos // CHUNK, kpos < kv_len)
            corr_ref[0, r0:r0 + ts, :] = jnp.where(kpos > qpos, (2.0 * LOG2E) * (qpos - kpos).astype(F32), 0.0)
            corr_ref[1, r0:r0 + ts, :] = jnp.where(allowed, 0.0, NEG_BIG)

    lane_q = lax.broadcasted_iota(jnp.int32, (tq, LANES), 1)
    lane_ks = lax.broadcasted_iota(jnp.int32, (ts, LANES), 1)
    part = (lax.broadcasted_iota(jnp.int32, (1, LANES), 1) % half) % n_parts
    log2e_lanes = jnp.where(part == 0, LOG2E_PARTS[0], jnp.where(part == 1, LOG2E_PARTS[1], LOG2E_PARTS[2]))

    def score_pass(h, s_ref, mt_ref, modes):
        c0 = pl.multiple_of(h * LANES, LANES)
        q = (q_ref[:, pl.ds(c0, LANES)].astype(F32) * (scale * LOG2E)).astype(BF16)
        slope = slope_ref[pl.ds(h, 1), :]
        slope_b = jnp.broadcast_to((slope * log2e_lanes).astype(BF16), (tq, LANES))
        zero_q = jnp.zeros((tq, LANES), BF16)
        for idx in range(2):
            own = (lane_q < half) if idx == 0 else (lane_q >= half)
            q_aug = jnp.where(own, q, jnp.where((lane_q % half) < 2 * n_parts, slope_b, zero_q))
            own_k = (lane_ks < half) if idx == 0 else (lane_ks >= half)
            mt = None
            for j, mode in enumerate(modes):
                if mode == MODE_SKIP:
                    continue
                r0 = j * ts
                k_aug = jnp.where(own_k, k_ref[r0:r0 + ts, pl.ds(c0, LANES)], kb_ref[r0:r0 + ts, :])
                s = lax.dot_general(k_aug, q_aug, (((1,), (1,)), ((), ())), preferred_element_type=F32)
                if mode == MODE_MASKED:
                    s = s + slope[:, 0:1] * corr_ref[0, r0:r0 + ts, :] + corr_ref[1, r0:r0 + ts, :]
                s_ref[idx, r0:r0 + ts, :] = s
                mj = jnp.max(s, axis=0, keepdims=True)
                mt = mj if mt is None else jnp.maximum(mt, mj)
                yield
            mt_ref[idx] = mt

    def value_pass(h, s_ref, mt_ref, modes):
        c0 = pl.multiple_of(h * LANES, LANES)
        for idx in range(2):
            m_prev = m_ref[idx, h]
            m_new = jnp.maximum(m_prev, mt_ref[idx])
            alpha = jnp.exp2(m_prev - m_new)
            lsum, pv = None, None
            for j, mode in enumerate(modes):
                if mode == MODE_SKIP:
                    continue
                r0 = j * ts
                p = jnp.exp2(s_ref[idx, r0:r0 + ts, :] - m_new)
                lj = jnp.sum(p, axis=0, keepdims=True)
                pj = lax.dot_general(v_ref[r0:r0 + ts, pl.ds(c0, LANES)], p.astype(BF16), (((0,), (0,)), ((), ())),
                                     preferred_element_type=F32)
                lsum = lj if lsum is None else lsum + lj
                pv = pj if pv is None else pv + pj
                yield
            l_ref[idx, h] = alpha * l_ref[idx, h] + lsum
            acc_ref[idx, h] = alpha * acc_ref[idx, h] + pv
            m_ref[idx, h] = m_new

    def run(*gens):
        live = list(gens)
        while live:
            for g in list(live):
                try:
                    next(g)
                except StopIteration:
                    live.remove(g)

    def all_heads(modes):
        bufs = ((s0_ref, mt0_ref), (s1_ref, mt1_ref))
        run(score_pass(0, *bufs[0], modes))

        def pair(g, c):
            h = 2 * g
            run(score_pass(h + 1, *bufs[1], modes), value_pass(h, *bufs[0], modes))
            run(score_pass(h + 2, *bufs[0], modes), value_pass(h + 1, *bufs[1], modes))
            return c

        lax.fori_loop(0, ATT_N_HEADS // 2 - 1, pair, 0)
        last = ATT_N_HEADS - 1
        run(score_pass(last, *bufs[1], modes), value_pass(last - 1, *bufs[0], modes))
        run(value_pass(last, *bufs[1], modes))

    for vi, modes in enumerate(variants):
        @pl.when(variant == vi)
        def _(modes=modes):
            mask_terms(modes)
            all_heads(modes)

    @pl.when((flags & FLAG_LAST) != 0)
    def _():
        lp = lam_ref[...]
        lam = (jnp.exp(jnp.sum(lp[0:1] * lp[1:2], axis=-1, keepdims=True))
               - jnp.exp(jnp.sum(lp[2:3] * lp[3:4], axis=-1, keepdims=True)) + lambda_init)

        def fin(h, c):
            c0 = pl.multiple_of(h * LANES, LANES)
            o = acc_ref[0, h] / l_ref[0, h] - lam * (acc_ref[1, h] / l_ref[1, h])
            on = o * lax.rsqrt(jnp.mean(o * o, axis=0, keepdims=True) + EPS)
            on = (on * sub_ref[...]) * (1.0 - lambda_init)
            o_ref[:, pl.ds(c0, LANES)] = on.T.astype(o_ref.dtype)
            return c

        lax.fori_loop(0, ATT_N_HEADS, fin, 0)


def _attn_pairs(nq, nk, tq, tk, q_off, kv_len):
    nsub, ts = _attn_sub(tk)
    qis, kis, fls, variants = [], [], [], []
    for qi in range(nq):
        first_q = q_off + qi * tq
        last_q = first_q + tq - 1
        kend = min((last_q // CHUNK + 1) * CHUNK, kv_len)
        nkv = -(-kend // tk)
        for ki in range(nkv):
            modes = []
            for j in range(nsub):
                ks = ki * tk + j * ts
                if ks >= kend:
                    modes.append(MODE_SKIP)
                elif ks + ts <= (first_q // CHUNK) * CHUNK and ks + ts <= kv_len:
                    modes.append(MODE_PLAIN)
                else:
                    modes.append(MODE_MASKED)
            modes = tuple(modes)
            if modes not in variants:
                variants.append(modes)
            fl = ((FLAG_FIRST if ki == 0 else 0) | (FLAG_LAST if ki == nkv - 1 else 0)
                  | (variants.index(modes) << VARIANT_SHIFT))
            qis.append(qi)
            kis.append(ki)
            fls.append(fl)
    return qis, kis, fls, tuple(variants)


def diff_attention_t(q_src, q_blk0, q_colblk, k_src, k_colblk, v_src, v_colblk, lam_rows, subln, *, b, lq, lk,
                     tq, tk, q_off, kv_len, lambda_init, out_rows):
    assert lq % tq == 0 and lk % tk == 0 and tq % LANES == 0
    assert q_off + lq <= 256 * BIAS_SPLIT + tq
    nq, nk = lq // tq, lk // tk
    width = ATT_N_HEADS * LANES
    qis, kis, fls, variants = _attn_pairs(nq, nk, tq, tk, q_off, kv_len)
    kern = functools.partial(_attn_t_kernel, tq=tq, tk=tk, q_off=q_off, kv_len=kv_len, lambda_init=lambda_init,
                             variants=variants)
    slopes = jnp.exp2(-ALIBI_MAX_BIAS * jnp.arange(1, ATT_N_HEADS + 1, dtype=F32) / ATT_N_HEADS)
    slopes = jnp.broadcast_to(slopes[:, None], (ATT_N_HEADS, LANES))
    const = lambda shape: pl.BlockSpec(shape, lambda bi, p, qt, kt, ft: (0,) * len(shape))
    return pl.pallas_call(
        kern,
        grid_spec=pltpu.PrefetchScalarGridSpec(
            num_scalar_prefetch=3,
            grid=(b, len(qis)),
            in_specs=[
                pl.BlockSpec((tq, width), lambda bi, p, qt, kt, ft: (q_blk0 + bi * nq + qt[p], q_colblk)),
                pl.BlockSpec((tk, width), lambda bi, p, qt, kt, ft: (bi * nk + kt[p], k_colblk)),
                pl.BlockSpec((tk, width), lambda bi, p, qt, kt, ft: (bi * nk + kt[p], v_colblk)),
                const((ATT_N_HEADS, LANES)),
                const((SUBLANES, ATT_HEAD_DIM)),
                const((ATT_V_HEAD, 1)),
            ],
            out_specs=pl.BlockSpec((tq, width), lambda bi, p, qt, kt, ft: (bi * nq + qt[p], 0)),
            scratch_shapes=[
                pltpu.VMEM((2, ATT_N_HEADS, 1, tq), F32),
                pltpu.VMEM((2, ATT_N_HEADS, 1, tq), F32),
                pltpu.VMEM((2, ATT_N_HEADS, ATT_V_HEAD, tq), F32),
                pltpu.VMEM((tk, LANES), BF16),
                pltpu.VMEM((2, tk, tq), F32),
                pltpu.VMEM((2, tk, tq), F32),
                pltpu.VMEM((2, tk, tq), F32),
                pltpu.VMEM((2, 1, tq), F32),
                pltpu.VMEM((2, 1, tq), F32),
            ],
        ),
        out_shape=jax.ShapeDtypeStruct((out_rows, width), BF16),
        compiler_params=_cparams(("arbitrary", "arbitrary")),
        name="diff_attention_t",
    )(jnp.asarray(qis, jnp.int32), jnp.asarray(kis, jnp.int32), jnp.asarray(fls, jnp.int32),
      q_src, k_src, v_src, slopes, lam_rows, subln.reshape(-1, 1))


def _attn_cache_kernel(q_ref, ck_ref, cv_ref, kn_ref, vn_ref, slope_ref, lam_ref, sub_ref, o_ref,
                       m_ref, l_ref, acc_ref, *, tq, tkc, ncache, past, n_new, lambda_init):
    j = pl.program_id(1)
    half = ATT_HEAD_DIM
    n_parts = len(LOG2E_PARTS)
    scale = ATT_HEAD_DIM ** -0.5

    @pl.when(j == 0)
    def _():
        m_ref[...] = jnp.full(m_ref.shape, NEG_BIG, F32)
        l_ref[...] = jnp.zeros(l_ref.shape, F32)
        acc_ref[...] = jnp.zeros(acc_ref.shape, F32)

    lane_q = lax.broadcasted_iota(jnp.int32, (tq, LANES), 1)
    part = (lax.broadcasted_iota(jnp.int32, (1, LANES), 1) % half) % n_parts
    log2e_lanes = jnp.where(part == 0, LOG2E_PARTS[0], jnp.where(part == 1, LOG2E_PARTS[1], LOG2E_PARTS[2]))

    def position_lanes(rows, kstart):
        lane = lax.broadcasted_iota(jnp.int32, (rows, LANES), 1)
        rel = kstart - past + lax.broadcasted_iota(jnp.int32, (rows, LANES), 0)
        hi = (rel // BIAS_SPLIT) * BIAS_SPLIT
        pos = lane % half
        kb = jnp.where(pos < n_parts, hi, jnp.where(pos < 2 * n_parts, rel - hi, 0)).astype(F32).astype(BF16)
        return kb, lane

    def head(h, k, v, kb, lane_k, corr):
        q = (q_ref[0, :, h * LANES:(h + 1) * LANES].astype(F32) * (scale * LOG2E)).astype(BF16)
        slope = slope_ref[h:h + 1, :]
        slope_b = jnp.broadcast_to((slope * log2e_lanes).astype(BF16), (tq, LANES))
        for idx in range(2):
            own = (lane_q < half) if idx == 0 else (lane_q >= half)
            q_aug = jnp.where(own, q, jnp.where((lane_q % half) < 2 * n_parts, slope_b, jnp.zeros_like(q)))
            own_k = (lane_k < half) if idx == 0 else (lane_k >= half)
            s = lax.dot_general(q_aug, jnp.where(own_k, k, kb), (((1,), (1,)), ((), ())),
                                preferred_element_type=F32)
            if corr is not None:
                s = s + slope[:, 0:1] * corr[0] + corr[1]
            m_prev = m_ref[idx, h]
            m_new = jnp.maximum(m_prev, jnp.max(s, axis=1, keepdims=True))
            alpha = jnp.exp2(m_prev - m_new)
            p = jnp.exp2(s - m_new)
            l_ref[idx, h] = alpha * l_ref[idx, h] + jnp.sum(p, axis=1, keepdims=True)
            acc_ref[idx, h] = alpha * acc_ref[idx, h] + _dot(p.astype(BF16), v)
            m_ref[idx, h] = m_new

    @pl.when(j < ncache)
    def _():
        kb, lane_k = position_lanes(tkc, j * tkc)
        k_heads = jnp.swapaxes(ck_ref[0], 0, 1)
        v_heads = jnp.swapaxes(cv_ref[0], 0, 1)
        for h in range(ATT_N_HEADS):
            head(h, k_heads[h].astype(BF16), v_heads[h].astype(BF16), kb, lane_k, None)

    @pl.when(j == ncache)
    def _():
        kb, lane_k = position_lanes(n_new, past)
        kpos = past + lax.broadcasted_iota(jnp.int32, (tq, n_new), 1)
        qpos = past + lax.broadcasted_iota(jnp.int32, (tq, n_new), 0)
        corr = (jnp.where(kpos > qpos, (2.0 * LOG2E) * (qpos - kpos).astype(F32), 0.0),
                jnp.where(kpos // CHUNK <= qpos // CHUNK, 0.0, NEG_BIG))
        lp = lam_ref[...]
        lam = (jnp.exp(jnp.sum(lp[0:1] * lp[1:2], axis=-1, keepdims=True))
               - jnp.exp(jnp.sum(lp[2:3] * lp[3:4], axis=-1, keepdims=True)) + lambda_init)
        for h in range(ATT_N_HEADS):
            cols = slice(h * LANES, (h + 1) * LANES)
            head(h, kn_ref[0, :, cols], vn_ref[0, :, cols], kb, lane_k, corr)
            o = acc_ref[0, h] / l_ref[0, h] - lam * (acc_ref[1, h] / l_ref[1, h])
            on = o * lax.rsqrt(jnp.mean(o * o, axis=1, keepdims=True) + EPS)
            on = (on * sub_ref[...]) * (1.0 - lambda_init)
            o_ref[0, :, cols] = on.astype(o_ref.dtype)


def diff_attention_cached(q_src, q_colblk, cache_k, cache_v, kv_new, lam_rows, subln, *, tkc, lambda_init):
    b, tq, _ = q_src.shape
    width = ATT_N_HEADS * LANES
    past = cache_k.shape[1]
    n_new = kv_new.shape[1]
    assert past % tkc == 0 and past % CHUNK == 0 and past <= 256 * BIAS_SPLIT
    ncache = past // tkc
    kern = functools.partial(_attn_cache_kernel, tq=tq, tkc=tkc, ncache=ncache, past=past, n_new=n_new,
                             lambda_init=lambda_init)
    slopes = jnp.exp2(-ALIBI_MAX_BIAS * jnp.arange(1, ATT_N_HEADS + 1, dtype=F32) / ATT_N_HEADS)
    slopes = jnp.broadcast_to(slopes[:, None], (ATT_N_HEADS, LANES))
    cache_spec = pl.BlockSpec((1, tkc, ATT_N_HEADS, LANES), lambda bi, j: (bi, jnp.minimum(j, ncache - 1), 0, 0))
    const = lambda shape: pl.BlockSpec(shape, lambda bi, j: (0,) * len(shape))
    return pl.pallas_call(
        kern,
        grid=(b, ncache + 1),
        in_specs=[
            pl.BlockSpec((1, tq, width), lambda bi, j: (bi, 0, q_colblk)),
            cache_spec,
            cache_spec,
            pl.BlockSpec((1, n_new, width), lambda bi, j: (bi, 0, 0)),
            pl.BlockSpec((1, n_new, width), lambda bi, j: (bi, 0, 1)),
            const((ATT_N_HEADS, LANES)),
            const((SUBLANES, ATT_HEAD_DIM)),
            const((1, ATT_V_HEAD)),
        ],
        out_specs=pl.BlockSpec((1, tq, width), lambda bi, j: (bi, 0, 0)),
        out_shape=jax.ShapeDtypeStruct((b, tq, width), BF16),
        scratch_shapes=[
            pltpu.VMEM((2, ATT_N_HEADS, tq, 1), F32),
            pltpu.VMEM((2, ATT_N_HEADS, tq, 1), F32),
            pltpu.VMEM((2, ATT_N_HEADS, tq, ATT_V_HEAD), F32),
        ],
        compiler_params=_cparams(("arbitrary", "arbitrary")),
        name="diff_attention_cached",
    )(q_src, cache_k, cache_v, kv_new, kv_new, slopes, lam_rows, subln.reshape(1, -1))


ROUTE_W = 2 * TOP_K


def _mixer_out_kernel(y_ref, o_ref, gs_ref, ga_ref, x_ref, wos_ref, woa_ref, wout_ref, nf_ref, rw_ref, rb_ref,
                      run0_ref, x2_ref, h_ref, gate_ref, sel_ref, cnt_ref, run_ref, *, tm):
    i = pl.program_id(0)

    @pl.when(i == 0)
    def _():
        run_ref[...] = run0_ref[...]

    o_ssd = _dot(y_ref[...], wos_ref[...])
    o_att = _dot(o_ref[...], woa_ref[...])
    merged = (jax.nn.sigmoid(gs_ref[...].astype(F32)) * o_ssd
              + jax.nn.sigmoid(ga_ref[...].astype(F32)) * o_att)
    x2 = x_ref[...] + _dot(merged.astype(BF16), wout_ref[...])
    x2_ref[...] = x2
    hn = x2 * lax.rsqrt(jnp.mean(x2 * x2, axis=-1, keepdims=True) + EPS) * nf_ref[...]
    h_ref[...] = hn

    a1, a2, _ = _split3(hn)
    w1, w2, _ = _split3(rw_ref[...])
    logits = _dot(a1, w1) + _dot(a1, w2) + _dot(a2, w1) + rb_ref[...]
    lane = lax.broadcasted_iota(jnp.int32, (tm, LANES), 1)
    work = jnp.where(lane < N_EXPERTS, logits, -jnp.inf)
    tops, idxs = [], []
    for _ in range(TOP_K):
        mx = jnp.max(work, axis=-1, keepdims=True)
        ix = jnp.min(jnp.where(work == mx, lane, LANES), axis=-1, keepdims=True)
        tops.append(mx)
        idxs.append(ix)
        work = jnp.where(lane == ix, -jnp.inf, work)
    es = [jnp.exp(tv - tops[0]) for tv in tops]
    den = es[0] + es[1] + es[2] + es[3]
    gates = jnp.zeros((tm, LANES), F32)
    for k in range(TOP_K):
        gates = jnp.where(lane == k, es[k] / den, gates)
    gate_ref[...] = gates[:, :ROUTE_W]

    chosen = jnp.zeros((tm, LANES), jnp.bool_)
    for k in range(TOP_K):
        chosen = jnp.logical_or(chosen, lane == idxs[k])
    multihot = jnp.where(chosen, 1.0, 0.0).astype(BF16)
    ri = lax.broadcasted_iota(jnp.int32, (tm, tm), 0)
    ci = lax.broadcasted_iota(jnp.int32, (tm, tm), 1)
    strict = jnp.where(ci < ri, 1.0, 0.0).astype(BF16)
    prefix = _dot(strict, multihot) + run_ref[...]
    sel = jnp.zeros((tm, LANES), jnp.int32)
    for k in range(TOP_K):
        rank = jnp.sum(jnp.where(lane == idxs[k], prefix, 0.0), axis=-1, keepdims=True)
        sel = jnp.where(lane == k, idxs[k], sel)
        sel = jnp.where(lane == TOP_K + k, rank.astype(jnp.int32), sel)
    sel_ref[...] = sel[:, :ROUTE_W]
    run_ref[...] = run_ref[...] + jnp.sum(multihot.astype(F32), axis=0, keepdims=True)
    cnt_ref[...] = run_ref[...]


def mixer_out(y_ssd, o_att, gates_src, gs_col, ga_col, x, w_o_ssd, w_o_att, w_out, norm_ffn, rw, rb, counts0, tm):
    t, d = x.shape
    assert t % tm == 0
    kern = functools.partial(_mixer_out_kernel, tm=tm)
    const = lambda shape: pl.BlockSpec(shape, lambda i: (0, 0))
    return pl.pallas_call(
        kern,
        grid=(t // tm,),
        in_specs=[
            pl.BlockSpec((tm, SSD_D_INNER), lambda i: (i, 0)),
            pl.BlockSpec((tm, d), lambda i: (i, 0)),
            pl.BlockSpec((tm, d), lambda i: (i, gs_col)),
            pl.BlockSpec((tm, d), lambda i: (i, ga_col)),
            pl.BlockSpec((tm, d), lambda i: (i, 0)),
            const((SSD_D_INNER, d)),
            const((d, d)),
            const((d, d)),
            const((1, d)),
            const((d, LANES)),
            const((1, LANES)),
            const((1, LANES)),
        ],
        out_specs=[
            pl.BlockSpec((tm, d), lambda i: (i, 0)),
            pl.BlockSpec((tm, d), lambda i: (i, 0)),
            pl.BlockSpec((tm, ROUTE_W), lambda i: (i, 0)),
            pl.BlockSpec((tm, ROUTE_W), lambda i: (i, 0)),
            pl.BlockSpec((1, LANES), lambda i: (0, 0)),
        ],
        out_shape=[
            jax.ShapeDtypeStruct((t, d), F32),
            jax.ShapeDtypeStruct((t, d), F32),
            jax.ShapeDtypeStruct((t, ROUTE_W), F32),
            jax.ShapeDtypeStruct((t, ROUTE_W), jnp.int32),
            jax.ShapeDtypeStruct((1, LANES), F32),
        ],
        scratch_shapes=[pltpu.VMEM((1, LANES), F32)],
        compiler_params=_cparams(("arbitrary",)),
        name="mixer_out",
    )(y_ssd, o_att, gates_src, gates_src, x, w_o_ssd, w_o_att, w_out, norm_ffn.reshape(1, d), rw, rb, counts0)


MOE_ROWS = 256
ROUTE_TOK = 256


def _dispatch_kernel(ps_ref, pl_ref, nu_ref, dest_ref, hp_ref, hs_ref, xs_ref, zero_ref, sem,
                     *, prompt_tiles, n_tiles, n_blocks):
    i = pl.program_id(0)

    def zero_copies(action):
        def per_expert(e, c):
            def row(r, c2):
                action(pltpu.make_async_copy(zero_ref.at[pl.ds(0, 1)], xs_ref.at[pl.ds(ps_ref[e] + r, 1)],
                                             sem.at[1]))
                return c2
            return lax.fori_loop(0, pl_ref[e], row, c)

        lax.fori_loop(0, N_EXPERTS, per_expert, 0)

        def tail(b, c):
            row0 = pl.multiple_of(b * MOE_ROWS, MOE_ROWS)
            action(pltpu.make_async_copy(zero_ref, xs_ref.at[pl.ds(row0, MOE_ROWS)], sem.at[1]))
            return c

        lax.fori_loop(nu_ref[0], n_blocks, tail, 0)

    @pl.when(i == 0)
    def _():
        zero_ref[...] = jnp.zeros(zero_ref.shape, F32)
        zero_copies(lambda cp: cp.start())

    def scatter(src_ref):
        def body(g, c):
            t0 = pl.multiple_of(g * SUBLANES, SUBLANES)
            for r in range(SUBLANES):
                for k in range(TOP_K):
                    d = dest_ref[0, 0, (t0 + r) * TOP_K + k]
                    pltpu.make_async_copy(src_ref.at[pl.ds(t0 + r, 1)], xs_ref.at[pl.ds(d, 1)], sem.at[0]).start(
                        priority=k % 2)
            return c
        lax.fori_loop(0, ROUTE_TOK // SUBLANES, body, 0)
        for k in range(TOP_K):
            pltpu.make_async_copy(src_ref, xs_ref.at[pl.ds(0, ROUTE_TOK)], sem.at[0]).wait()

    @pl.when(i < prompt_tiles)
    def _():
        scatter(hp_ref)

    @pl.when(i >= prompt_tiles)
    def _():
        scatter(hs_ref)

    @pl.when(i == n_tiles - 1)
    def _():
        zero_copies(lambda cp: cp.wait())


def moe_dispatch(dest, hn_p, hn_s, pad_start, pad_len, n_used, n_blocks):
    tp, d = hn_p.shape
    tsn = hn_s.shape[0]
    assert tp % ROUTE_TOK == 0 and tsn % ROUTE_TOK == 0
    prompt_tiles = tp // ROUTE_TOK
    n_tiles = prompt_tiles + tsn // ROUTE_TOK
    kern = functools.partial(_dispatch_kernel, prompt_tiles=prompt_tiles, n_tiles=n_tiles, n_blocks=n_blocks)
    return pl.pallas_call(
        kern,
        grid_spec=pltpu.PrefetchScalarGridSpec(
            num_scalar_prefetch=3,
            grid=(n_tiles,),
            in_specs=[
                pl.BlockSpec((1, 1, ROUTE_TOK * TOP_K), lambda i, a, b, c: (i, 0, 0), memory_space=pltpu.SMEM),
                pl.BlockSpec((ROUTE_TOK, d), lambda i, a, b, c: (jnp.minimum(i, prompt_tiles - 1), 0)),
                pl.BlockSpec((ROUTE_TOK, d), lambda i, a, b, c: (jnp.maximum(i - prompt_tiles, 0), 0)),
            ],
            out_specs=pl.BlockSpec(memory_space=pl.ANY),
            scratch_shapes=[pltpu.VMEM((MOE_ROWS, d), F32), pltpu.SemaphoreType.DMA((2,))],
        ),
        out_shape=jax.ShapeDtypeStruct((n_blocks * MOE_ROWS, d), F32),
        compiler_params=_cparams(("arbitrary",)),
        name="moe_dispatch",
    )(pad_start, pad_len, n_used, dest.reshape(n_tiles, 1, ROUTE_TOK * TOP_K), hn_p, hn_s)


def _moe_kernel(be_ref, nu_ref, nxt_ref, x_ref, wgu_hbm, bgu_ref, wd_hbm, bd_ref, o_ref,
                wgu32_ref, wd32_ref, wgu16_ref, wd16_ref, seg_ref, sem):
    b = pl.program_id(0)
    used = b < nu_ref[0]
    new_expert = jnp.logical_or(b == 0, be_ref[b] != be_ref[jnp.maximum(b - 1, 0)])

    def weight_copies(e, slot):
        return (pltpu.make_async_copy(wgu_hbm.at[e], wgu32_ref.at[slot], sem.at[slot, 0]),
                pltpu.make_async_copy(wd_hbm.at[e], wd32_ref.at[slot], sem.at[slot, 1]))

    @pl.when(b == 0)
    def _():
        seg_ref[0] = 0
        for cp in weight_copies(be_ref[0], 0):
            cp.start()

    @pl.when(jnp.logical_and(used, new_expert))
    def _():
        slot = seg_ref[0] % 2
        nxt = nxt_ref[b]

        @pl.when(nxt < nu_ref[0])
        def _():
            for cp in weight_copies(be_ref[nxt], 1 - slot):
                cp.start()

        for cp in weight_copies(be_ref[b], slot):
            cp.wait()
        wgu16_ref[...] = wgu32_ref[slot].astype(BF16)
        wd16_ref[...] = wd32_ref[slot].astype(BF16)
        seg_ref[0] = seg_ref[0] + 1

    @pl.when(used)
    def _():
        x = x_ref[...].astype(BF16)
        gu = _dot(x, wgu16_ref[...]) + bgu_ref[0]
        d_ff = gu.shape[1] // 2
        gate = jnp.minimum(gu[:, :d_ff], SWIGLU_LIMIT)
        up = jnp.clip(gu[:, d_ff:], -SWIGLU_LIMIT, SWIGLU_LIMIT)
        act = (up + 1.0) * gate * jax.nn.sigmoid(SWIGLU_ALPHA * gate)
        o_ref[...] = _dot(act.astype(BF16), wd16_ref[...]) + bd_ref[0]

    @pl.when(jnp.logical_not(used))
    def _():
        o_ref[...] = jnp.zeros(o_ref.shape, F32)


def moe_experts(xs, block_e, n_used, next_first, w_gu, b_gu, w_down, b_down):
    rows, d = xs.shape
    nb = rows // MOE_ROWS
    e, _, gu_w = w_gu.shape
    last = lambda b, nu: jnp.minimum(b, nu[0] - 1)
    return pl.pallas_call(
        _moe_kernel,
        grid_spec=pltpu.PrefetchScalarGridSpec(
            num_scalar_prefetch=3,
            grid=(nb,),
            in_specs=[
                pl.BlockSpec((MOE_ROWS, d), lambda b, be, nu, nx: (last(b, nu), 0)),
                pl.BlockSpec(memory_space=pl.ANY),
                pl.BlockSpec((1, 1, gu_w), lambda b, be, nu, nx: (be[last(b, nu)], 0, 0)),
                pl.BlockSpec(memory_space=pl.ANY),
                pl.BlockSpec((1, 1, d), lambda b, be, nu, nx: (be[last(b, nu)], 0, 0)),
            ],
            out_specs=pl.BlockSpec((MOE_ROWS, d), lambda b, be, nu, nx: (b, 0)),
            scratch_shapes=[
                pltpu.VMEM((2, d, gu_w), F32),
                pltpu.VMEM((2, gu_w // 2, d), F32),
                pltpu.VMEM((d, gu_w), BF16),
                pltpu.VMEM((gu_w // 2, d), BF16),
                pltpu.SMEM((1,), jnp.int32),
                pltpu.SemaphoreType.DMA((2, 2)),
            ],
        ),
        out_shape=jax.ShapeDtypeStruct((rows, d), F32),
        compiler_params=_cparams(("arbitrary",)),
        name="moe_experts",
    )(block_e, n_used, next_first, xs, w_gu, b_gu.reshape(e, 1, gu_w), w_down, b_down.reshape(e, 1, d))


def _combine_kernel(dfirst_ref, dnext_ref, x2_ref, gate_ref, g_ref, ys_ref, o_ref, buf_ref, sem, *, n_tiles):
    i = pl.program_id(0)
    slot = i % 2

    def issue(dref, sl):
        def body(g, c):
            t0 = pl.multiple_of(g * SUBLANES, SUBLANES)
            for r in range(SUBLANES):
                for k in range(TOP_K):
                    d = dref[0, 0, (t0 + r) * TOP_K + k]
                    pltpu.make_async_copy(ys_ref.at[pl.ds(d, 1)], buf_ref.at[sl, k, pl.ds(t0 + r, 1)],
                                          sem.at[sl]).start(priority=k % 2)
            return c
        lax.fori_loop(0, ROUTE_TOK // SUBLANES, body, 0)

    @pl.when(i == 0)
    def _():
        issue(dfirst_ref, 0)

    @pl.when(i + 1 < n_tiles)
    def _():
        issue(dnext_ref, 1 - slot)

    for k in range(TOP_K):
        pltpu.make_async_copy(ys_ref.at[pl.ds(0, ROUTE_TOK)], buf_ref.at[slot, k], sem.at[slot]).wait()

    gates = gate_ref[...]
    moe = buf_ref[slot, 0] * gates[:, 0:1]
    for k in range(1, TOP_K):
        moe = moe + buf_ref[slot, k] * gates[:, k:k + 1]
    tok = x2_ref[...] + moe
    y = tok * lax.rsqrt(jnp.mean(tok * tok, axis=-1, keepdims=True) + EPS)
    o_ref[...] = y * g_ref[...]


def combine(x2, ys, dest, gates, norm_final):
    t, d = x2.shape
    assert t % ROUTE_TOK == 0
    n_tiles = t // ROUTE_TOK
    dest3 = dest.reshape(n_tiles, 1, ROUTE_TOK * TOP_K)
    kern = functools.partial(_combine_kernel, n_tiles=n_tiles)
    smem = lambda imap: pl.BlockSpec((1, 1, ROUTE_TOK * TOP_K), imap, memory_space=pltpu.SMEM)
    return pl.pallas_call(
        kern,
        grid=(n_tiles,),
        in_specs=[
            smem(lambda i: (0, 0, 0)),
            smem(lambda i: (jnp.minimum(i + 1, n_tiles - 1), 0, 0)),
            pl.BlockSpec((ROUTE_TOK, d), lambda i: (i, 0)),
            pl.BlockSpec((ROUTE_TOK, ROUTE_W), lambda i: (i, 0)),
            pl.BlockSpec((1, d), lambda i: (0, 0)),
            pl.BlockSpec(memory_space=pl.ANY),
        ],
        out_specs=pl.BlockSpec((ROUTE_TOK, d), lambda i: (i, 0)),
        out_shape=jax.ShapeDtypeStruct((t, d), F32),
        scratch_shapes=[pltpu.VMEM((2, TOP_K, ROUTE_TOK, d), F32), pltpu.SemaphoreType.DMA((2,))],
        compiler_params=_cparams(("arbitrary",)),
        name="combine",
    )(dest3, dest3, x2, gates, norm_final.reshape(1, d), ys)


def kernel(x_prompt, x_sample, cache_k, cache_v, state_ssm, state_conv, norm_mix, w_in, conv_w, conv_b, dt_bias, a_log, d_skip, ssd_norm, w_o_ssd, lambda_q1, lambda_k1, lambda_q2, lambda_k2, subln, w_o_att, w_out, norm_ffn, router_w, router_b, w_gu, b_gu, w_down, b_down, norm_final):
    bp, s, d = x_prompt.shape
    bs, ts, _ = x_sample.shape
    past = cache_k.shape[2]
    depth = w_in.shape[0]
    assert depth == 1
    layer = 0
    lambda_init = 0.8 - 0.6 * math.exp(-0.3 * layer)
    tp, tsn = bp * s, bs * ts
    t_all = tp + tsn

    heads = ATT_N_HEADS
    xp2 = x_prompt.reshape(tp, d)
    xs2 = x_sample.reshape(tsn, d)

    sizes = (SSD_D_INNER, SSD_CONV_CH, SSD_N_HEADS, d, d, d, d, d)
    offs = [0]
    for sz in sizes:
        offs.append(offs[-1] + sz)
    w = w_in[layer].astype(BF16)
    seg = lambda i: w[:, offs[i]:offs[i + 1]]
    dt_pad = jnp.zeros((d, PROJ_XD_W - SSD_CONV_CH - SSD_N_HEADS), BF16)
    w_all = jnp.concatenate([seg(4), seg(5), seg(0), seg(3), seg(6), seg(7), seg(1), seg(2), dt_pad], axis=1)
    k_p, v_p, kv16_p, zq_p, xd_p = in_proj(xp2, norm_mix[layer], w_all, tm=2048)
    k_s, v_s, kv16_s, zq_s, xd_s = in_proj(xs2, norm_mix[layer], w_all, tm=tsn)

    prev_p = jnp.zeros((bp, SUBLANES, SSD_CONV_CH), F32)
    h0_p = jnp.zeros((bp, SSD_D_STATE, SSD_D_INNER), F32)
    ssd_w = (conv_w[layer], conv_b[layer], dt_bias[layer], a_log[layer], d_skip[layer], ssd_norm[layer])
    y_p, hT_p = ssd_mixer(xd_p, zq_p, 0, prev_p, h0_p, *ssd_w, b=bp, l=s, tl=256, valid_len=None, out_rows=tp)

    pad_rows = CHUNK - ts
    pad_seq = lambda a: jnp.pad(a.reshape(bs, ts, -1), ((0, 0), (0, pad_rows), (0, 0))).reshape(bs * CHUNK, -1)
    prev_s = jnp.pad(state_conv[layer], ((0, 0), (SUBLANES - (SSD_CONV_WIDTH - 1), 0), (0, 0)))
    h0_s = jnp.swapaxes(state_ssm[layer].reshape(bs, SSD_D_INNER, SSD_D_STATE), 1, 2)
    y_s, hT_s = ssd_mixer(pad_seq(xd_s), pad_seq(zq_s[:, :SSD_D_INNER]), 0, prev_s, h0_s, *ssd_w, b=bs, l=CHUNK,
                          tl=CHUNK, valid_len=ts, out_rows=bs * CHUNK)
    y_s = y_s.reshape(bs, CHUNK, -1)[:, :ts].reshape(tsn, -1)

    lam_rows = jnp.concatenate([lambda_q1[layer][None], lambda_k1[layer][None], lambda_q2[layer][None],
                                lambda_k2[layer][None], jnp.zeros((4, ATT_HEAD_DIM), F32)], axis=0)
    o_p = diff_attention_t(zq_p, 0, SSD_D_INNER // d, kv16_p, 0, kv16_p, 1, lam_rows, subln[layer], b=bp, lq=s,
                           lk=s, tq=512, tk=1024, q_off=0, kv_len=s, lambda_init=lambda_init, out_rows=tp)
    o_s = diff_attention_cached(zq_s.reshape(bs, ts, -1), SSD_D_INNER // d, cache_k[layer], cache_v[layer],
                                kv16_s.reshape(bs, ts, 2 * d), lam_rows, subln[layer], tkc=1024,
                                lambda_init=lambda_init).reshape(tsn, d)

    rw = jnp.zeros((d, LANES), F32).at[:, :N_EXPERTS].set(router_w[layer])
    rb = jnp.zeros((1, LANES), F32).at[0, :N_EXPERTS].set(router_b[layer])
    mix_w = (w_o_ssd[layer].astype(BF16), w_o_att[layer].astype(BF16), w_out[layer].astype(BF16), norm_ffn[layer],
             rw, rb)
    x2_p, hn_p, gate_p, sel_p, cnt_p = mixer_out(y_p, o_p, zq_p, 3, 4, xp2, *mix_w,
                                                 jnp.zeros((1, LANES), F32), tm=512)
    x2_s, hn_s, gate_s, sel_s, cnt_all = mixer_out(y_s, o_s, zq_s, 3, 4, xs2, *mix_w, cnt_p, tm=tsn)

    sel = jnp.concatenate([sel_p, sel_s], axis=0)
    top_e = sel[:, :TOP_K]
    rank = sel[:, TOP_K:]
    cnt = cnt_all[0, :N_EXPERTS].astype(jnp.int32)
    padded = (cnt + MOE_ROWS - 1) // MOE_ROWS * MOE_ROWS
    ends = jnp.cumsum(padded)
    starts = ends - padded
    dest = (starts[top_e] + rank).reshape(-1)
    nb = (t_all * TOP_K + N_EXPERTS * (MOE_ROWS - 1) + MOE_ROWS - 1) // MOE_ROWS
    block_start = jnp.arange(nb, dtype=jnp.int32) * MOE_ROWS
    block_e = jnp.minimum(jnp.sum((ends[None, :] <= block_start[:, None]).astype(jnp.int32), axis=1),
                          N_EXPERTS - 1)
    n_used = (ends[-1] // MOE_ROWS).astype(jnp.int32).reshape(1)

    xs_sorted = moe_dispatch(dest, hn_p, hn_s, starts + cnt, padded - cnt, n_used, nb)
    next_first = ends[block_e] // MOE_ROWS
    ys = moe_experts(xs_sorted, block_e, n_used, next_first, w_gu[layer], b_gu[layer], w_down[layer],
                     b_down[layer])
    y_prompt = combine(x2_p, ys, dest[:tp * TOP_K], gate_p, norm_final).reshape(bp, s, d)
    y_sample = combine(x2_s, ys, dest[tp * TOP_K:], gate_s, norm_final).reshape(bs, ts, d)

    new_k_p = k_p.reshape(1, bp, s, heads, 2 * ATT_HEAD_DIM)
    new_v_p = v_p.reshape(1, bp, s, heads, ATT_V_HEAD)
    new_k_s = k_s.reshape(1, bs, ts, heads, 2 * ATT_HEAD_DIM)
    new_v_s = v_s.reshape(1, bs, ts, heads, ATT_V_HEAD)
    ssm_p = jnp.swapaxes(hT_p, 1, 2).reshape(1, bp, SSD_N_HEADS, SSD_HEAD_DIM, SSD_D_STATE)
    ssm_s = jnp.swapaxes(hT_s, 1, 2).reshape(1, bs, SSD_N_HEADS, SSD_HEAD_DIM, SSD_D_STATE)
    keep = SSD_CONV_WIDTH - 1
    conv_p = xd_p.reshape(bp, s, -1)[:, s - keep:, :SSD_CONV_CH][None]
    raw_s = jnp.concatenate([state_conv[layer], xd_s.reshape(bs, ts, -1)[:, :, :SSD_CONV_CH]], axis=1)
    conv_s = raw_s[:, -keep:][None]
    return (y_prompt, y_sample, new_k_p, new_v_p, ssm_p, conv_p, new_k_s, new_v_s, ssm_s, conv_s)
```

```python
import functools
import math

import jax
import jax.numpy as jnp
from jax import lax
from jax.experimental import pallas as pl
from jax.experimental.pallas import tpu as pltpu

EPS = 1e-5
CHUNK = 64
D_MODEL = 1024
SSD_D_INNER = 2048
SSD_HEAD_DIM = 64
SSD_N_HEADS = 32
SSD_GROUPS = 4
SSD_D_STATE = 128
SSD_CONV_WIDTH = 4
SSD_CONV_CH = 3072
ATT_HEAD_DIM = 64
ATT_N_HEADS = 8
ATT_V_HEAD = 128
ALIBI_MAX_BIAS = 8.0
N_EXPERTS = 32
TOP_K = 4
SWIGLU_LIMIT = 7.0
SWIGLU_ALPHA = 1.702

LANES = 128
SUBLANES = 8
VMEM_LIMIT = 56 * 1024 * 1024
NEG_BIG = -1e30

BF16 = jnp.bfloat16
F32 = jnp.float32


def _cparams(sem, flags=None):
    return pltpu.CompilerParams(dimension_semantics=sem, vmem_limit_bytes=VMEM_LIMIT, flags=flags)


def _split3(x):
    h1 = x.astype(BF16)
    r1 = x - h1.astype(F32)
    h2 = r1.astype(BF16)
    h3 = (r1 - h2.astype(F32)).astype(BF16)
    return h1, h2, h3


def _dot(a, b):
    return jnp.dot(a, b, preferred_element_type=F32)


def _dot_exact_rhs(x, m):
    h1, h2, h3 = _split3(x)
    return _dot(h1, m) + _dot(h2, m) + _dot(h3, m)


def _dot_exact_lhs(m, x):
    h1, h2, h3 = _split3(x)
    return _dot(m, h1) + _dot(m, h2) + _dot(m, h3)


PROJ_TN = 256
PROJ_K = (0, D_MODEL // PROJ_TN)
PROJ_V = (PROJ_K[1], PROJ_K[1] + D_MODEL // PROJ_TN)
PROJ_ZQ_W = SSD_D_INNER + 3 * D_MODEL
PROJ_ZQ = (PROJ_V[1], PROJ_V[1] + PROJ_ZQ_W // PROJ_TN)
PROJ_XD_W = -(-(SSD_CONV_CH + SSD_N_HEADS) // PROJ_TN) * PROJ_TN
PROJ_XD = (PROJ_ZQ[1], PROJ_ZQ[1] + PROJ_XD_W // PROJ_TN)


def _in_proj_kernel(x_ref, g_ref, w_ref, k_ref, v_ref, kv16_ref, zq_ref, xd_ref, xn_ref):
    j = pl.program_id(1)

    @pl.when(j == 0)
    def _():
        x = x_ref[...]
        y = x * lax.rsqrt(jnp.mean(x * x, axis=-1, keepdims=True) + EPS)
        xn_ref[...] = (y * g_ref[...]).astype(BF16)

    @pl.when(j < PROJ_K[1])
    def _():
        acc = _dot(xn_ref[...], w_ref[...])
        k_ref[...] = acc
        kv16_ref[...] = acc.astype(BF16)

    @pl.when(jnp.logical_and(j >= PROJ_V[0], j < PROJ_V[1]))
    def _():
        acc = _dot(xn_ref[...], w_ref[...])
        v_ref[...] = acc
        kv16_ref[...] = acc.astype(BF16)

    @pl.when(jnp.logical_and(j >= PROJ_ZQ[0], j < PROJ_ZQ[1]))
    def _():
        zq_ref[...] = _dot(xn_ref[...], w_ref[...]).astype(BF16)

    @pl.when(j >= PROJ_XD[0])
    def _():
        xd_ref[...] = _dot(xn_ref[...], w_ref[...])


def in_proj(x, gain, w, tm):
    t, d = x.shape
    assert t % tm == 0 and w.shape[1] == PROJ_XD[1] * PROJ_TN
    tn = PROJ_TN

    def out_map(rng):
        return lambda i, j: (i, jnp.clip(j - rng[0], 0, rng[1] - rng[0] - 1))

    return pl.pallas_call(
        _in_proj_kernel,
        grid=(t // tm, PROJ_XD[1]),
        in_specs=[
            pl.BlockSpec((tm, d), lambda i, j: (i, 0)),
            pl.BlockSpec((1, d), lambda i, j: (0, 0)),
            pl.BlockSpec((d, tn), lambda i, j: (0, j)),
        ],
        out_specs=[
            pl.BlockSpec((tm, tn), out_map(PROJ_K)),
            pl.BlockSpec((tm, tn), out_map(PROJ_V)),
            pl.BlockSpec((tm, tn), out_map((PROJ_K[0], PROJ_V[1]))),
            pl.BlockSpec((tm, tn), out_map(PROJ_ZQ)),
            pl.BlockSpec((tm, tn), out_map(PROJ_XD)),
        ],
        out_shape=[
            jax.ShapeDtypeStruct((t, d), F32),
            jax.ShapeDtypeStruct((t, d), F32),
            jax.ShapeDtypeStruct((t, 2 * d), BF16),
            jax.ShapeDtypeStruct((t, PROJ_ZQ_W), BF16),
            jax.ShapeDtypeStruct((t, PROJ_XD_W), F32),
        ],
        scratch_shapes=[pltpu.VMEM((tm, d), BF16)],
        compiler_params=_cparams(("arbitrary", "arbitrary")),
        name="in_proj",
    )(x, gain.reshape(1, d), w)


CONV_ROWS, CONV_COLS = 64, 512


def _ssd_kernel(xbc_ref, z_ref, prev_ref, h0_ref, cw_ref, cb_ref, dtb_ref, alog_ref, dsk_ref, gn_ref,
                y_ref, hout_ref, xpad_ref, xc_ref, tail_ref, st_ref, *, tl, valid_len):
    t = pl.program_id(1)
    nt = pl.num_programs(1)
    nch = tl // CHUNK
    dinner = SSD_D_INNER
    gw = dinner // SSD_GROUPS
    ns = SSD_D_STATE

    @pl.when(t == 0)
    def _():
        tail_ref[...] = prev_ref[0]
        st_ref[...] = h0_ref[0]

    xpad_ref[0:SUBLANES, :] = tail_ref[...]
    xpad_ref[SUBLANES:SUBLANES + tl, :] = xbc_ref[:, 0:SSD_CONV_CH]
    tail_ref[...] = xbc_ref[tl - SUBLANES:tl, 0:SSD_CONV_CH]
    for r in range(0, tl, CONV_ROWS):
        for c in range(0, SSD_CONV_CH, CONV_COLS):
            acc = cb_ref[:, c:c + CONV_COLS]
            for k in range(SSD_CONV_WIDTH):
                off = r + SUBLANES - (SSD_CONV_WIDTH - 1) + k
                acc = acc + xpad_ref[off:off + CONV_ROWS, c:c + CONV_COLS] * cw_ref[k:k + 1, c:c + CONV_COLS]
            xc_ref[r:r + CONV_ROWS, c:c + CONV_COLS] = acc * jax.nn.sigmoid(acc)

    head_of_lane = lax.broadcasted_iota(jnp.int32, (SSD_N_HEADS, dinner), 1) // SSD_HEAD_DIM
    expand = (head_of_lane == lax.broadcasted_iota(jnp.int32, (SSD_N_HEADS, dinner), 0)).astype(BF16)
    ti = lax.broadcasted_iota(jnp.int32, (CHUNK, CHUNK), 0)
    si = lax.broadcasted_iota(jnp.int32, (CHUNK, CHUNK), 1)
    tril = (si <= ti).astype(BF16)
    row_c = lax.broadcasted_iota(jnp.int32, (CHUNK, dinner), 0)
    pos_in_head = lax.broadcasted_iota(jnp.int32, (CHUNK, dinner), 1) % CHUNK
    upper = row_c <= pos_in_head
    row_p = lax.broadcasted_iota(jnp.int32, (CHUNK, LANES), 0)
    lane_p = lax.broadcasted_iota(jnp.int32, (CHUNK, LANES), 1)
    causal_pair = (lane_p % CHUNK) <= row_p
    left_half = lane_p < SSD_HEAD_DIM
    a_neg_e = _dot_exact_rhs(-jnp.exp(alog_ref[...]), expand)
    dsk_e = _dot_exact_rhs(dsk_ref[...], expand)

    def chunk_body(c, carry):
        r0 = pl.multiple_of(c * CHUNK, CHUNK)
        xs = xc_ref[pl.ds(r0, CHUNK), 0:dinner]
        dt_raw = xbc_ref[pl.ds(r0, CHUNK), SSD_CONV_CH:SSD_CONV_CH + SSD_N_HEADS]
        dtv = dt_raw + dtb_ref[...]
        dt = jnp.maximum(dtv, 0.0) + jnp.log1p(jnp.exp(-jnp.abs(dtv)))
        if valid_len is not None:
            rows = t * tl + r0 + lax.broadcasted_iota(jnp.int32, (CHUNK, SSD_N_HEADS), 0)
            dt = jnp.where(rows < valid_len, dt, 0.0)
        dt_e = _dot_exact_rhs(dt, expand)
        a_e = dt_e * a_neg_e
        acs_e = _dot_exact_lhs(tril, a_e)
        rowterm = jnp.sum(jnp.where(upper, a_e, 0.0), axis=0, keepdims=True)
        acs_last = acs_e[CHUNK - 1:CHUNK, :]
        xdt = xs * dt_e
        x_dec = (xdt * jnp.exp(acs_last - acs_e)).astype(BF16)
        e_acs = jnp.exp(acs_e)
        e_last = jnp.exp(acs_last)

        y_parts = []
        for g in range(SSD_GROUPS):
            lo = g * gw
            bm = xc_ref[pl.ds(r0, CHUNK), dinner + g * ns:dinner + (g + 1) * ns].astype(BF16)
            cm = xc_ref[pl.ds(r0, CHUNK), dinner + SSD_GROUPS * ns + g * ns:
                        dinner + SSD_GROUPS * ns + (g + 1) * ns].astype(BF16)
            b2 = jnp.concatenate([bm, bm], axis=0)
            cb2 = lax.dot_general(cm, b2, (((1,), (1,)), ((), ())), preferred_element_type=F32)
            st_g = st_ref[:, lo:lo + gw]
            y_off = _dot(cm, st_g.astype(BF16)) * e_acs[:, lo:lo + gw]
            pieces = []
            for j in range(gw // LANES):
                l0 = lo + j * LANES
                diff = acs_e[:, l0:l0 + LANES] - rowterm[:, l0:l0 + LANES]
                dec = jnp.where(causal_pair, jnp.exp(diff), 0.0)
                scores = (cb2 * dec).astype(BF16)
                xp = xdt[:, l0:l0 + LANES]
                xblk = jnp.concatenate([jnp.where(left_half, xp, 0.0), jnp.where(left_half, 0.0, xp)],
                                       axis=0).astype(BF16)
                pieces.append(_dot(scores, xblk))
            y_diag = jnp.concatenate(pieces, axis=1)
            upd = lax.dot_general(bm, x_dec[:, lo:lo + gw], (((0,), (0,)), ((), ())),
                                  preferred_element_type=F32)
            st_ref[:, lo:lo + gw] = e_last[:, lo:lo + gw] * st_g + upd
            yg = y_diag + y_off + xs[:, lo:lo + gw] * dsk_e[:, lo:lo + gw]
            zg = z_ref[pl.ds(r0, CHUNK), lo:lo + gw].astype(F32)
            yg = yg * (zg * jax.nn.sigmoid(zg))
            yn = yg * lax.rsqrt(jnp.mean(yg * yg, axis=-1, keepdims=True) + EPS)
            y_parts.append((yn * gn_ref[:, lo:lo + gw]).astype(y_ref.dtype))
        y_ref[pl.ds(r0, CHUNK), :] = jnp.concatenate(y_parts, axis=1)
        return carry

    lax.fori_loop(0, nch, chunk_body, 0, unroll=True)

    @pl.when(t == nt - 1)
    def _():
        hout_ref[0] = st_ref[...]


def ssd_mixer(xbc, z_src, z_col, conv_prev8, h0_t, conv_w, conv_b, dt_bias, a_log, d_skip, ssd_norm,
              b, l, tl, valid_len, out_rows):
    wx = xbc.shape[1]
    assert l % tl == 0 and tl % CHUNK == 0 and CHUNK == SSD_HEAD_DIM
    nt = l // tl
    kern = functools.partial(_ssd_kernel, tl=tl, valid_len=valid_len)
    full = lambda shape: pl.BlockSpec(shape, lambda i, j: (0,) * len(shape))
    return pl.pallas_call(
        kern,
        grid=(b, nt),
        in_specs=[
            pl.BlockSpec((tl, wx), lambda i, j: (i * nt + j, 0)),
            pl.BlockSpec((tl, SSD_D_INNER), lambda i, j: (i * nt + j, z_col)),
            pl.BlockSpec((1, SUBLANES, SSD_CONV_CH), lambda i, j: (i, 0, 0)),
            pl.BlockSpec((1, SSD_D_STATE, SSD_D_INNER), lambda i, j: (i, 0, 0)),
            full((SSD_CONV_WIDTH, SSD_CONV_CH)),
            full((1, SSD_CONV_CH)),
            full((1, SSD_N_HEADS)),
            full((1, SSD_N_HEADS)),
            full((1, SSD_N_HEADS)),
            full((1, SSD_D_INNER)),
        ],
        out_specs=[
            pl.BlockSpec((tl, SSD_D_INNER), lambda i, j: (i * nt + j, 0)),
            pl.BlockSpec((1, SSD_D_STATE, SSD_D_INNER), lambda i, j: (i, 0, 0)),
        ],
        out_shape=[
            jax.ShapeDtypeStruct((out_rows, SSD_D_INNER), BF16),
            jax.ShapeDtypeStruct((b, SSD_D_STATE, SSD_D_INNER), F32),
        ],
        scratch_shapes=[
            pltpu.VMEM((tl + SUBLANES, SSD_CONV_CH), F32),
            pltpu.VMEM((tl, SSD_CONV_CH), F32),
            pltpu.VMEM((SUBLANES, SSD_CONV_CH), F32),
            pltpu.VMEM((SSD_D_STATE, SSD_D_INNER), F32),
        ],
        compiler_params=_cparams(("arbitrary", "arbitrary")),
        name="ssd_mixer",
    )(xbc, z_src, conv_prev8, h0_t, conv_w, conv_b.reshape(1, -1), dt_bias.reshape(1, -1),
      a_log.reshape(1, -1), d_skip.reshape(1, -1), ssd_norm.reshape(1, -1))


BIAS_SPLIT = 32
LOG2E = 1.4426950408889634
LOG2E_PARTS = (1.4453125, -0.00262451171875, 7.063150405883789e-06)
FLAG_FIRST, FLAG_LAST, VARIANT_SHIFT = 1, 2, 2
ATT_SUB = 256
MODE_PLAIN, MODE_MASKED, MODE_SKIP = "plain", "masked", "skip"


def _attn_sub(tk):
    nsub = tk // ATT_SUB if tk % ATT_SUB == 0 else 1
    return nsub, tk // nsub


def _attn_t_kernel(qi_ref, ki_ref, fl_ref, q_ref, k_ref, v_ref, slope_ref, lam_ref, sub_ref, o_ref,
                   m_ref, l_ref, acc_ref, kb_ref, corr_ref, s0_ref, s1_ref, mt0_ref, mt1_ref,
                   *, tq, tk, q_off, kv_len, lambda_init, variants):
    p_idx = pl.program_id(1)
    qi = qi_ref[p_idx]
    ki = ki_ref[p_idx]
    flags = fl_ref[p_idx]
    qstart = q_off + qi * tq
    kstart = ki * tk
    half = ATT_HEAD_DIM
    scale = ATT_HEAD_DIM ** -0.5

    @pl.when((flags & FLAG_FIRST) != 0)
    def _():
        m_ref[...] = jnp.full(m_ref.shape, NEG_BIG, F32)
        l_ref[...] = jnp.zeros(l_ref.shape, F32)
        acc_ref[...] = jnp.zeros(acc_ref.shape, F32)

    lane_k = lax.broadcasted_iota(jnp.int32, (tk, LANES), 1)
    rel = kstart - qstart + lax.broadcasted_iota(jnp.int32, (tk, LANES), 0)
    hi = (rel // BIAS_SPLIT) * BIAS_SPLIT
    lo = rel - hi
    pos_lane = lane_k % half
    n_parts = len(LOG2E_PARTS)
    kb_ref[...] = jnp.where(pos_lane < n_parts, hi, jnp.where(pos_lane < 2 * n_parts, lo, 0)).astype(F32).astype(BF16)

    variant = flags >> VARIANT_SHIFT
    nsub, ts = _attn_sub(tk)

    def mask_terms(modes):
        for j, mode in enumerate(modes):
            if mode != MODE_MASKED:
                continue
            r0 = j * ts
            kpos = kstart + r0 + lax.broadcasted_iota(jnp.int32, (ts, tq), 0)
            qpos = qstart + lax.broadcasted_iota(jnp.int32, (ts, tq), 1)
            allowed = jnp.logical_and(kpos // CHUNK <= qpos // CHUNK, kpos < kv_len)
            corr_ref[0, r0:r0 + ts, :] = jnp.where(kpos > qpos, (2.0 * LOG2E) * (qpos - kpos).astype(F32), 0.0)
            corr_ref[1, r0:r0 + ts, :] = jnp.where(allowed, 0.0, NEG_BIG)

    lane_q = lax.broadcasted_iota(jnp.int32, (tq, LANES), 1)
    lane_ks = lax.broadcasted_iota(jnp.int32, (ts, LANES), 1)
    part = (lax.broadcasted_iota(jnp.int32, (1, LANES), 1) % half) % n_parts
    log2e_lanes = jnp.where(part == 0, LOG2E_PARTS[0], jnp.where(part == 1, LOG2E_PARTS[1], LOG2E_PARTS[2]))

    def score_pass(h, s_ref, mt_ref, modes):
        c0 = pl.multiple_of(h * LANES, LANES)
        q = (q_ref[:, pl.ds(c0, LANES)].astype(F32) * (scale * LOG2E)).astype(BF16)
        slope = slope_ref[pl.ds(h, 1), :]
        slope_b = jnp.broadcast_to((slope * log2e_lanes).astype(BF16), (tq, LANES))
        zero_q = jnp.zeros((tq, LANES), BF16)
        for idx in range(2):
            own = (lane_q < half) if idx == 0 else (lane_q >= half)
            q_aug = jnp.where(own, q, jnp.where((lane_q % half) < 2 * n_parts, slope_b, zero_q))
            own_k = (lane_ks < half) if idx == 0 else (lane_ks >= half)
            mt = None
            for j, mode in enumerate(modes):
                if mode == MODE_SKIP:
                    continue
                r0 = j * ts
                k_aug = jnp.where(own_k, k_ref[r0:r0 + ts, pl.ds(c0, LANES)], kb_ref[r0:r0 + ts, :])
                s = lax.dot_general(k_aug, q_aug, (((1,), (1,)), ((), ())), preferred_element_type=F32)
                if mode == MODE_MASKED:
                    s = s + slope[:, 0:1] * corr_ref[0, r0:r0 + ts, :] + corr_ref[1, r0:r0 + ts, :]
                s_ref[idx, r0:r0 + ts, :] = s
                mj = jnp.max(s, axis=0, keepdims=True)
                mt = mj if mt is None else jnp.maximum(mt, mj)
                yield
            mt_ref[idx] = mt

    def value_pass(h, s_ref, mt_ref, modes):
        c0 = pl.multiple_of(h * LANES, LANES)
        for idx in range(2):
            m_prev = m_ref[idx, h]
            m_new = jnp.maximum(m_prev, mt_ref[idx])
            alpha = jnp.exp2(m_prev - m_new)
            lsum, pv = None, None
            for j, mode in enumerate(modes):
                if mode == MODE_SKIP:
                    continue
                r0 = j * ts
                p = jnp.exp2(s_ref[idx, r0:r0 + ts, :] - m_new)
                lj = jnp.sum(p, axis=0, keepdims=True)
                pj = lax.dot_general(v_ref[r0:r0 + ts, pl.ds(c0, LANES)], p.astype(BF16), (((0,), (0,)), ((), ())),
                                     preferred_element_type=F32)
                lsum = lj if lsum is None else lsum + lj
                pv = pj if pv is None else pv + pj
                yield
            l_ref[idx, h] = alpha * l_ref[idx, h] + lsum
            acc_ref[idx, h] = alpha * acc_ref[idx, h] + pv
            m_ref[idx, h] = m_new

    def run(*gens):
        live = list(gens)
        while live:
            for g in list(live):
                try:
                    next(g)
                except StopIteration:
                    live.remove(g)

    def all_heads(modes):
        bufs = ((s0_ref, mt0_ref), (s1_ref, mt1_ref))
        run(score_pass(0, *bufs[0], modes))

        def pair(g, c):
            h = 2 * g
            run(score_pass(h + 1, *bufs[1], modes), value_pass(h, *bufs[0], modes))
            run(score_pass(h + 2, *bufs[0], modes), value_pass(h + 1, *bufs[1], modes))
            return c

        lax.fori_loop(0, ATT_N_HEADS // 2 - 1, pair, 0, unroll=True)
        last = ATT_N_HEADS - 1
        run(score_pass(last, *bufs[1], modes), value_pass(last - 1, *bufs[0], modes))
        run(value_pass(last, *bufs[1], modes))

    for vi, modes in enumerate(variants):
        @pl.when(variant == vi)
        def _(modes=modes):
            mask_terms(modes)
            all_heads(modes)

    @pl.when((flags & FLAG_LAST) != 0)
    def _():
        lp = lam_ref[...]
        lam = (jnp.exp(jnp.sum(lp[0:1] * lp[1:2], axis=-1, keepdims=True))
               - jnp.exp(jnp.sum(lp[2:3] * lp[3:4], axis=-1, keepdims=True)) + lambda_init)

        def fin(h, c):
            c0 = pl.multiple_of(h * LANES, LANES)
            o = acc_ref[0, h] / l_ref[0, h] - lam * (acc_ref[1, h] / l_ref[1, h])
            on = o * lax.rsqrt(jnp.mean(o * o, axis=0, keepdims=True) + EPS)
            on = (on * sub_ref[...]) * (1.0 - lambda_init)
            o_ref[:, pl.ds(c0, LANES)] = on.T.astype(o_ref.dtype)
            return c

        lax.fori_loop(0, ATT_N_HEADS, fin, 0)


def _attn_pairs(nq, nk, tq, tk, q_off, kv_len):
    nsub, ts = _attn_sub(tk)
    qis, kis, fls, variants = [], [], [], []
    for qi in range(nq):
        first_q = q_off + qi * tq
        last_q = first_q + tq - 1
        kend = min((last_q // CHUNK + 1) * CHUNK, kv_len)
        nkv = -(-kend // tk)
        for ki in range(nkv):
            modes = []
            for j in range(nsub):
                ks = ki * tk + j * ts
                if ks >= kend:
                    modes.append(MODE_SKIP)
                elif ks + ts <= (first_q // CHUNK) * CHUNK and ks + ts <= kv_len:
                    modes.append(MODE_PLAIN)
                else:
                    modes.append(MODE_MASKED)
            modes = tuple(modes)
            if modes not in variants:
                variants.append(modes)
            fl = ((FLAG_FIRST if ki == 0 else 0) | (FLAG_LAST if ki == nkv - 1 else 0)
                  | (variants.index(modes) << VARIANT_SHIFT))
            qis.append(qi)
            kis.append(ki)
            fls.append(fl)
    return qis, kis, fls, tuple(variants)


def diff_attention_t(q_src, q_blk0, q_colblk, k_src, k_colblk, v_src, v_colblk, lam_rows, subln, *, b, lq, lk,
                     tq, tk, q_off, kv_len, lambda_init, out_rows):
    assert lq % tq == 0 and lk % tk == 0 and tq % LANES == 0
    assert q_off + lq <= 256 * BIAS_SPLIT + tq
    nq, nk = lq // tq, lk // tk
    width = ATT_N_HEADS * LANES
    qis, kis, fls, variants = _attn_pairs(nq, nk, tq, tk, q_off, kv_len)
    kern = functools.partial(_attn_t_kernel, tq=tq, tk=tk, q_off=q_off, kv_len=kv_len, lambda_init=lambda_init,
                             variants=variants)
    slopes = jnp.exp2(-ALIBI_MAX_BIAS * jnp.arange(1, ATT_N_HEADS + 1, dtype=F32) / ATT_N_HEADS)
    slopes = jnp.broadcast_to(slopes[:, None], (ATT_N_HEADS, LANES))
    const = lambda shape: pl.BlockSpec(shape, lambda bi, p, qt, kt, ft: (0,) * len(shape))
    return pl.pallas_call(
        kern,
        grid_spec=pltpu.PrefetchScalarGridSpec(
            num_scalar_prefetch=3,
            grid=(b, len(qis)),
            in_specs=[
                pl.BlockSpec((tq, width), lambda bi, p, qt, kt, ft: (q_blk0 + bi * nq + qt[p], q_colblk)),
                pl.BlockSpec((tk, width), lambda bi, p, qt, kt, ft: (bi * nk + kt[p], k_colblk)),
                pl.BlockSpec((tk, width), lambda bi, p, qt, kt, ft: (bi * nk + kt[p], v_colblk)),
                const((ATT_N_HEADS, LANES)),
                const((SUBLANES, ATT_HEAD_DIM)),
                const((ATT_V_HEAD, 1)),
            ],
            out_specs=pl.BlockSpec((tq, width), lambda bi, p, qt, kt, ft: (bi * nq + qt[p], 0)),
            scratch_shapes=[
                pltpu.VMEM((2, ATT_N_HEADS, 1, tq), F32),
                pltpu.VMEM((2, ATT_N_HEADS, 1, tq), F32),
                pltpu.VMEM((2, ATT_N_HEADS, ATT_V_HEAD, tq), F32),
                pltpu.VMEM((tk, LANES), BF16),
                pltpu.VMEM((2, tk, tq), F32),
                pltpu.VMEM((2, tk, tq), F32),
                pltpu.VMEM((2, tk, tq), F32),
                pltpu.VMEM((2, 1, tq), F32),
                pltpu.VMEM((2, 1, tq), F32),
            ],
        ),
        out_shape=jax.ShapeDtypeStruct((out_rows, width), BF16),
        compiler_params=_cparams(("arbitrary", "arbitrary")),
        name="diff_attention_t",
    )(jnp.asarray(qis, jnp.int32), jnp.asarray(kis, jnp.int32), jnp.asarray(fls, jnp.int32),
      q_src, k_src, v_src, slopes, lam_rows, subln.reshape(-1, 1))


def _attn_cache_kernel(q_ref, ck_ref, cv_ref, kn_ref, vn_ref, slope_ref, lam_ref, sub_ref, o_ref,
                       m_ref, l_ref, acc_ref, *, tq, tkc, ncache, past, n_new, lambda_init):
    j = pl.program_id(1)
    half = ATT_HEAD_DIM
    n_parts = len(LOG2E_PARTS)
    scale = ATT_HEAD_DIM ** -0.5

    @pl.when(j == 0)
    def _():
        m_ref[...] = jnp.full(m_ref.shape, NEG_BIG, F32)
        l_ref[...] = jnp.zeros(l_ref.shape, F32)
        acc_ref[...] = jnp.zeros(acc_ref.shape, F32)

    lane_q = lax.broadcasted_iota(jnp.int32, (tq, LANES), 1)
    part = (lax.broadcasted_iota(jnp.int32, (1, LANES), 1) % half) % n_parts
    log2e_lanes = jnp.where(part == 0, LOG2E_PARTS[0], jnp.where(part == 1, LOG2E_PARTS[1], LOG2E_PARTS[2]))

    def position_lanes(rows, kstart):
        lane = lax.broadcasted_iota(jnp.int32, (rows, LANES), 1)
        rel = kstart - past + lax.broadcasted_iota(jnp.int32, (rows, LANES), 0)
        hi = (rel // BIAS_SPLIT) * BIAS_SPLIT
        pos = lane % half
        kb = jnp.where(pos < n_parts, hi, jnp.where(pos < 2 * n_parts, rel - hi, 0)).astype(F32).astype(BF16)
        return kb, lane

    def head(h, k, v, kb, lane_k, corr):
        q = (q_ref[0, :, h * LANES:(h + 1) * LANES].astype(F32) * (scale * LOG2E)).astype(BF16)
        slope = slope_ref[h:h + 1, :]
        slope_b = jnp.broadcast_to((slope * log2e_lanes).astype(BF16), (tq, LANES))
        for idx in range(2):
            own = (lane_q < half) if idx == 0 else (lane_q >= half)
            q_aug = jnp.where(own, q, jnp.where((lane_q % half) < 2 * n_parts, slope_b, jnp.zeros_like(q)))
            own_k = (lane_k < half) if idx == 0 else (lane_k >= half)
            s = lax.dot_general(q_aug, jnp.where(own_k, k, kb), (((1,), (1,)), ((), ())),
                                preferred_element_type=F32)
            if corr is not None:
                s = s + slope[:, 0:1] * corr[0] + corr[1]
            m_prev = m_ref[idx, h]
            m_new = jnp.maximum(m_prev, jnp.max(s, axis=1, keepdims=True))
            alpha = jnp.exp2(m_prev - m_new)
            p = jnp.exp2(s - m_new)
            l_ref[idx, h] = alpha * l_ref[idx, h] + jnp.sum(p, axis=1, keepdims=True)
            acc_ref[idx, h] = alpha * acc_ref[idx, h] + _dot(p.astype(BF16), v)
            m_ref[idx, h] = m_new

    @pl.when(j < ncache)
    def _():
        kb, lane_k = position_lanes(tkc, j * tkc)
        k_heads = jnp.swapaxes(ck_ref[0], 0, 1)
        v_heads = jnp.swapaxes(cv_ref[0], 0, 1)
        for h in range(ATT_N_HEADS):
            head(h, k_heads[h].astype(BF16), v_heads[h].astype(BF16), kb, lane_k, None)

    @pl.when(j == ncache)
    def _():
        kb, lane_k = position_lanes(n_new, past)
        kpos = past + lax.broadcasted_iota(jnp.int32, (tq, n_new), 1)
        qpos = past + lax.broadcasted_iota(jnp.int32, (tq, n_new), 0)
        corr = (jnp.where(kpos > qpos, (2.0 * LOG2E) * (qpos - kpos).astype(F32), 0.0),
                jnp.where(kpos // CHUNK <= qpos // CHUNK, 0.0, NEG_BIG))
        lp = lam_ref[...]
        lam = (jnp.exp(jnp.sum(lp[0:1] * lp[1:2], axis=-1, keepdims=True))
               - jnp.exp(jnp.sum(lp[2:3] * lp[3:4], axis=-1, keepdims=True)) + lambda_init)
        for h in range(ATT_N_HEADS):
            cols = slice(h * LANES, (h + 1) * LANES)
            head(h, kn_ref[0, :, cols], vn_ref[0, :, cols], kb, lane_k, corr)
            o = acc_ref[0, h] / l_ref[0, h] - lam * (acc_ref[1, h] / l_ref[1, h])
            on = o * lax.rsqrt(jnp.mean(o * o, axis=1, keepdims=True) + EPS)
            on = (on * sub_ref[...]) * (1.0 - lambda_init)
            o_ref[0, :, cols] = on.astype(o_ref.dtype)


def diff_attention_cached(q_src, q_colblk, cache_k, cache_v, kv_new, lam_rows, subln, *, tkc, lambda_init):
    b, tq, _ = q_src.shape
    width = ATT_N_HEADS * LANES
    past = cache_k.shape[1]
    n_new = kv_new.shape[1]
    assert past % tkc == 0 and past % CHUNK == 0 and past <= 256 * BIAS_SPLIT
    ncache = past // tkc
    kern = functools.partial(_attn_cache_kernel, tq=tq, tkc=tkc, ncache=ncache, past=past, n_new=n_new,
                             lambda_init=lambda_init)
    slopes = jnp.exp2(-ALIBI_MAX_BIAS * jnp.arange(1, ATT_N_HEADS + 1, dtype=F32) / ATT_N_HEADS)
    slopes = jnp.broadcast_to(slopes[:, None], (ATT_N_HEADS, LANES))
    cache_spec = pl.BlockSpec((1, tkc, ATT_N_HEADS, LANES), lambda bi, j: (bi, jnp.minimum(j, ncache - 1), 0, 0))
    const = lambda shape: pl.BlockSpec(shape, lambda bi, j: (0,) * len(shape))
    return pl.pallas_call(
        kern,
        grid=(b, ncache + 1),
        in_specs=[
            pl.BlockSpec((1, tq, width), lambda bi, j: (bi, 0, q_colblk)),
            cache_spec,
            cache_spec,
            pl.BlockSpec((1, n_new, width), lambda bi, j: (bi, 0, 0)),
            pl.BlockSpec((1, n_new, width), lambda bi, j: (bi, 0, 1)),
            const((ATT_N_HEADS, LANES)),
            const((SUBLANES, ATT_HEAD_DIM)),
            const((1, ATT_V_HEAD)),
        ],
        out_specs=pl.BlockSpec((1, tq, width), lambda bi, j: (bi, 0, 0)),
        out_shape=jax.ShapeDtypeStruct((b, tq, width), BF16),
        scratch_shapes=[
            pltpu.VMEM((2, ATT_N_HEADS, tq, 1), F32),
            pltpu.VMEM((2, ATT_N_HEADS, tq, 1), F32),
            pltpu.VMEM((2, ATT_N_HEADS, tq, ATT_V_HEAD), F32),
        ],
        compiler_params=_cparams(("arbitrary", "arbitrary")),
        name="diff_attention_cached",
    )(q_src, cache_k, cache_v, kv_new, kv_new, slopes, lam_rows, subln.reshape(1, -1))


ROUTE_W = 2 * TOP_K


def _mixer_out_kernel(y_ref, o_ref, gs_ref, ga_ref, x_ref, wos_ref, woa_ref, wout_ref, nf_ref, rw_ref, rb_ref,
                      run0_ref, x2_ref, h_ref, gate_ref, sel_ref, cnt_ref, run_ref, *, tm):
    i = pl.program_id(0)

    @pl.when(i == 0)
    def _():
        run_ref[...] = run0_ref[...]

    o_ssd = _dot(y_ref[...], wos_ref[...])
    o_att = _dot(o_ref[...], woa_ref[...])
    merged = (jax.nn.sigmoid(gs_ref[...].astype(F32)) * o_ssd
              + jax.nn.sigmoid(ga_ref[...].astype(F32)) * o_att)
    x2 = x_ref[...] + _dot(merged.astype(BF16), wout_ref[...])
    x2_ref[...] = x2
    hn = x2 * lax.rsqrt(jnp.mean(x2 * x2, axis=-1, keepdims=True) + EPS) * nf_ref[...]
    h_ref[...] = hn

    a1, a2, _ = _split3(hn)
    w1, w2, _ = _split3(rw_ref[...])
    logits = _dot(a1, w1) + _dot(a1, w2) + _dot(a2, w1) + rb_ref[...]
    lane = lax.broadcasted_iota(jnp.int32, (tm, LANES), 1)
    work = jnp.where(lane < N_EXPERTS, logits, -jnp.inf)
    tops, idxs = [], []
    for _ in range(TOP_K):
        mx = jnp.max(work, axis=-1, keepdims=True)
        ix = jnp.min(jnp.where(work == mx, lane, LANES), axis=-1, keepdims=True)
        tops.append(mx)
        idxs.append(ix)
        work = jnp.where(lane == ix, -jnp.inf, work)
    es = [jnp.exp(tv - tops[0]) for tv in tops]
    den = es[0] + es[1] + es[2] + es[3]
    gates = jnp.zeros((tm, LANES), F32)
    for k in range(TOP_K):
        gates = jnp.where(lane == k, es[k] / den, gates)
    gate_ref[...] = gates[:, :ROUTE_W]

    chosen = jnp.zeros((tm, LANES), jnp.bool_)
    for k in range(TOP_K):
        chosen = jnp.logical_or(chosen, lane == idxs[k])
    multihot = jnp.where(chosen, 1.0, 0.0).astype(BF16)
    ri = lax.broadcasted_iota(jnp.int32, (tm, tm), 0)
    ci = lax.broadcasted_iota(jnp.int32, (tm, tm), 1)
    strict = jnp.where(ci < ri, 1.0, 0.0).astype(BF16)
    prefix = _dot(strict, multihot) + run_ref[...]
    sel = jnp.zeros((tm, LANES), jnp.int32)
    for k in range(TOP_K):
        rank = jnp.sum(jnp.where(lane == idxs[k], prefix, 0.0), axis=-1, keepdims=True)
        sel = jnp.where(lane == k, idxs[k], sel)
        sel = jnp.where(lane == TOP_K + k, rank.astype(jnp.int32), sel)
    sel_ref[...] = sel[:, :ROUTE_W]
    run_ref[...] = run_ref[...] + jnp.sum(multihot.astype(F32), axis=0, keepdims=True)
    cnt_ref[...] = run_ref[...]


def mixer_out(y_ssd, o_att, gates_src, gs_col, ga_col, x, w_o_ssd, w_o_att, w_out, norm_ffn, rw, rb, counts0, tm):
    t, d = x.shape
    assert t % tm == 0
    kern = functools.partial(_mixer_out_kernel, tm=tm)
    const = lambda shape: pl.BlockSpec(shape, lambda i: (0, 0))
    return pl.pallas_call(
        kern,
        grid=(t // tm,),
        in_specs=[
            pl.BlockSpec((tm, SSD_D_INNER), lambda i: (i, 0)),
            pl.BlockSpec((tm, d), lambda i: (i, 0)),
            pl.BlockSpec((tm, d), lambda i: (i, gs_col)),
            pl.BlockSpec((tm, d), lambda i: (i, ga_col)),
            pl.BlockSpec((tm, d), lambda i: (i, 0)),
            const((SSD_D_INNER, d)),
            const((d, d)),
            const((d, d)),
            const((1, d)),
            const((d, LANES)),
            const((1, LANES)),
            const((1, LANES)),
        ],
        out_specs=[
            pl.BlockSpec((tm, d), lambda i: (i, 0)),
            pl.BlockSpec((tm, d), lambda i: (i, 0)),
            pl.BlockSpec((tm, ROUTE_W), lambda i: (i, 0)),
            pl.BlockSpec((tm, ROUTE_W), lambda i: (i, 0)),
            pl.BlockSpec((1, LANES), lambda i: (0, 0)),
        ],
        out_shape=[
            jax.ShapeDtypeStruct((t, d), F32),
            jax.ShapeDtypeStruct((t, d), F32),
            jax.ShapeDtypeStruct((t, ROUTE_W), F32),
            jax.ShapeDtypeStruct((t, ROUTE_W), jnp.int32),
            jax.ShapeDtypeStruct((1, LANES), F32),
        ],
        scratch_shapes=[pltpu.VMEM((1, LANES), F32)],
        compiler_params=_cparams(("arbitrary",)),
        name="mixer_out",
    )(y_ssd, o_att, gates_src, gates_src, x, w_o_ssd, w_o_att, w_out, norm_ffn.reshape(1, d), rw, rb, counts0)


MOE_ROWS = 256
ROUTE_TOK = 256


def _dispatch_kernel(ps_ref, pl_ref, nu_ref, dest_ref, hp_ref, hs_ref, xs_ref, zero_ref, sem,
                     *, prompt_tiles, n_tiles, n_blocks):
    i = pl.program_id(0)

    def zero_copies(action):
        def per_expert(e, c):
            def row(r, c2):
                action(pltpu.make_async_copy(zero_ref.at[pl.ds(0, 1)], xs_ref.at[pl.ds(ps_ref[e] + r, 1)],
                                             sem.at[1]))
                return c2
            return lax.fori_loop(0, pl_ref[e], row, c)

        lax.fori_loop(0, N_EXPERTS, per_expert, 0)

        def tail(b, c):
            row0 = pl.multiple_of(b * MOE_ROWS, MOE_ROWS)
            action(pltpu.make_async_copy(zero_ref, xs_ref.at[pl.ds(row0, MOE_ROWS)], sem.at[1]))
            return c

        lax.fori_loop(nu_ref[0], n_blocks, tail, 0)

    @pl.when(i == 0)
    def _():
        zero_ref[...] = jnp.zeros(zero_ref.shape, F32)
        zero_copies(lambda cp: cp.start())

    def scatter(src_ref):
        def body(g, c):
            t0 = pl.multiple_of(g * SUBLANES, SUBLANES)
            for r in range(SUBLANES):
                for k in range(TOP_K):
                    d = dest_ref[0, 0, (t0 + r) * TOP_K + k]
                    pltpu.make_async_copy(src_ref.at[pl.ds(t0 + r, 1)], xs_ref.at[pl.ds(d, 1)], sem.at[0]).start(
                        priority=k % 2)
            return c
        lax.fori_loop(0, ROUTE_TOK // SUBLANES, body, 0)
        for k in range(TOP_K):
            pltpu.make_async_copy(src_ref, xs_ref.at[pl.ds(0, ROUTE_TOK)], sem.at[0]).wait()

    @pl.when(i < prompt_tiles)
    def _():
        scatter(hp_ref)

    @pl.when(i >= prompt_tiles)
    def _():
        scatter(hs_ref)

    @pl.when(i == n_tiles - 1)
    def _():
        zero_copies(lambda cp: cp.wait())


def moe_dispatch(dest, hn_p, hn_s, pad_start, pad_len, n_used, n_blocks):
    tp, d = hn_p.shape
    tsn = hn_s.shape[0]
    assert tp % ROUTE_TOK == 0 and tsn % ROUTE_TOK == 0
    prompt_tiles = tp // ROUTE_TOK
    n_tiles = prompt_tiles + tsn // ROUTE_TOK
    kern = functools.partial(_dispatch_kernel, prompt_tiles=prompt_tiles, n_tiles=n_tiles, n_blocks=n_blocks)
    return pl.pallas_call(
        kern,
        grid_spec=pltpu.PrefetchScalarGridSpec(
            num_scalar_prefetch=3,
            grid=(n_tiles,),
            in_specs=[
                pl.BlockSpec((1, 1, ROUTE_TOK * TOP_K), lambda i, a, b, c: (i, 0, 0), memory_space=pltpu.SMEM),
                pl.BlockSpec((ROUTE_TOK, d), lambda i, a, b, c: (jnp.minimum(i, prompt_tiles - 1), 0)),
                pl.BlockSpec((ROUTE_TOK, d), lambda i, a, b, c: (jnp.maximum(i - prompt_tiles, 0), 0)),
            ],
            out_specs=pl.BlockSpec(memory_space=pl.ANY),
            scratch_shapes=[pltpu.VMEM((MOE_ROWS, d), F32), pltpu.SemaphoreType.DMA((2,))],
        ),
        out_shape=jax.ShapeDtypeStruct((n_blocks * MOE_ROWS, d), F32),
        compiler_params=_cparams(("arbitrary",)),
        name="moe_dispatch",
    )(pad_start, pad_len, n_used, dest.reshape(n_tiles, 1, ROUTE_TOK * TOP_K), hn_p, hn_s)


def _moe_kernel(be_ref, nu_ref, nxt_ref, x_ref, wgu_hbm, bgu_ref, wd_hbm, bd_ref, o_ref,
                wgu32_ref, wd32_ref, wgu16_ref, wd16_ref, seg_ref, sem):
    b = pl.program_id(0)
    used = b < nu_ref[0]
    new_expert = jnp.logical_or(b == 0, be_ref[b] != be_ref[jnp.maximum(b - 1, 0)])

    def weight_copies(e, slot):
        return (pltpu.make_async_copy(wgu_hbm.at[e], wgu32_ref.at[slot], sem.at[slot, 0]),
                pltpu.make_async_copy(wd_hbm.at[e], wd32_ref.at[slot], sem.at[slot, 1]))

    @pl.when(b == 0)
    def _():
        seg_ref[0] = 0
        for cp in weight_copies(be_ref[0], 0):
            cp.start()

    @pl.when(jnp.logical_and(used, new_expert))
    def _():
        slot = seg_ref[0] % 2
        nxt = nxt_ref[b]

        @pl.when(nxt < nu_ref[0])
        def _():
            for cp in weight_copies(be_ref[nxt], 1 - slot):
                cp.start()

        for cp in weight_copies(be_ref[b], slot):
            cp.wait()
        wgu16_ref[...] = wgu32_ref[slot].astype(BF16)
        wd16_ref[...] = wd32_ref[slot].astype(BF16)
        seg_ref[0] = seg_ref[0] + 1

    @pl.when(used)
    def _():
        x = x_ref[...].astype(BF16)
        gu = _dot(x, wgu16_ref[...]) + bgu_ref[0]
        d_ff = gu.shape[1] // 2
        gate = jnp.minimum(gu[:, :d_ff], SWIGLU_LIMIT)
        up = jnp.clip(gu[:, d_ff:], -SWIGLU_LIMIT, SWIGLU_LIMIT)
        act = (up + 1.0) * gate * jax.nn.sigmoid(SWIGLU_ALPHA * gate)
        o_ref[...] = _dot(act.astype(BF16), wd16_ref[...]) + bd_ref[0]

    @pl.when(jnp.logical_not(used))
    def _():
        o_ref[...] = jnp.zeros(o_ref.shape, F32)


def moe_experts(xs, block_e, n_used, next_first, w_gu, b_gu, w_down, b_down):
    rows, d = xs.shape
    nb = rows // MOE_ROWS
    e, _, gu_w = w_gu.shape
    last = lambda b, nu: jnp.minimum(b, nu[0] - 1)
    return pl.pallas_call(
        _moe_kernel,
        grid_spec=pltpu.PrefetchScalarGridSpec(
            num_scalar_prefetch=3,
            grid=(nb,),
            in_specs=[
                pl.BlockSpec((MOE_ROWS, d), lambda b, be, nu, nx: (last(b, nu), 0)),
                pl.BlockSpec(memory_space=pl.ANY),
                pl.BlockSpec((1, 1, gu_w), lambda b, be, nu, nx: (be[last(b, nu)], 0, 0)),
                pl.BlockSpec(memory_space=pl.ANY),
                pl.BlockSpec((1, 1, d), lambda b, be, nu, nx: (be[last(b, nu)], 0, 0)),
            ],
            out_specs=pl.BlockSpec((MOE_ROWS, d), lambda b, be, nu, nx: (b, 0)),
            scratch_shapes=[
                pltpu.VMEM((2, d, gu_w), F32),
                pltpu.VMEM((2, gu_w // 2, d), F32),
                pltpu.VMEM((d, gu_w), BF16),
                pltpu.VMEM((gu_w // 2, d), BF16),
                pltpu.SMEM((1,), jnp.int32),
                pltpu.SemaphoreType.DMA((2, 2)),
            ],
        ),
        out_shape=jax.ShapeDtypeStruct((rows, d), F32),
        compiler_params=_cparams(("arbitrary",)),
        name="moe_experts",
    )(block_e, n_used, next_first, xs, w_gu, b_gu.reshape(e, 1, gu_w), w_down, b_down.reshape(e, 1, d))


def _combine_kernel(dfirst_ref, dnext_ref, x2_ref, gate_ref, g_ref, ys_ref, o_ref, buf_ref, sem, *, n_tiles):
    i = pl.program_id(0)
    slot = i % 2

    def issue(dref, sl):
        def body(g, c):
            t0 = pl.multiple_of(g * SUBLANES, SUBLANES)
            for r in range(SUBLANES):
                for k in range(TOP_K):
                    d = dref[0, 0, (t0 + r) * TOP_K + k]
                    pltpu.make_async_copy(ys_ref.at[pl.ds(d, 1)], buf_ref.at[sl, k, pl.ds(t0 + r, 1)],
                                          sem.at[sl]).start(priority=k % 2)
            return c
        lax.fori_loop(0, ROUTE_TOK // SUBLANES, body, 0)

    @pl.when(i == 0)
    def _():
        issue(dfirst_ref, 0)

    @pl.when(i + 1 < n_tiles)
    def _():
        issue(dnext_ref, 1 - slot)

    for k in range(TOP_K):
        pltpu.make_async_copy(ys_ref.at[pl.ds(0, ROUTE_TOK)], buf_ref.at[slot, k], sem.at[slot]).wait()

    gates = gate_ref[...]
    moe = buf_ref[slot, 0] * gates[:, 0:1]
    for k in range(1, TOP_K):
        moe = moe + buf_ref[slot, k] * gates[:, k:k + 1]
    tok = x2_ref[...] + moe
    y = tok * lax.rsqrt(jnp.mean(tok * tok, axis=-1, keepdims=True) + EPS)
    o_ref[...] = y * g_ref[...]


def combine(x2, ys, dest, gates, norm_final):
    t, d = x2.shape
    assert t % ROUTE_TOK == 0
    n_tiles = t // ROUTE_TOK
    dest3 = dest.reshape(n_tiles, 1, ROUTE_TOK * TOP_K)
    kern = functools.partial(_combine_kernel, n_tiles=n_tiles)
    smem = lambda imap: pl.BlockSpec((1, 1, ROUTE_TOK * TOP_K), imap, memory_space=pltpu.SMEM)
    return pl.pallas_call(
        kern,
        grid=(n_tiles,),
        in_specs=[
            smem(lambda i: (0, 0, 0)),
            smem(lambda i: (jnp.minimum(i + 1, n_tiles - 1), 0, 0)),
            pl.BlockSpec((ROUTE_TOK, d), lambda i: (i, 0)),
            pl.BlockSpec((ROUTE_TOK, ROUTE_W), lambda i: (i, 0)),
            pl.BlockSpec((1, d), lambda i: (0, 0)),
            pl.BlockSpec(memory_space=pl.ANY),
        ],
        out_specs=pl.BlockSpec((ROUTE_TOK, d), lambda i: (i, 0)),
        out_shape=jax.ShapeDtypeStruct((t, d), F32),
        scratch_shapes=[pltpu.VMEM((2, TOP_K, ROUTE_TOK, d), F32), pltpu.SemaphoreType.DMA((2,))],
        compiler_params=_cparams(("arbitrary",)),
        name="combine",
    )(dest3, dest3, x2, gates, norm_final.reshape(1, d), ys)


def kernel(x_prompt, x_sample, cache_k, cache_v, state_ssm, state_conv, norm_mix, w_in, conv_w, conv_b, dt_bias, a_log, d_skip, ssd_norm, w_o_ssd, lambda_q1, lambda_k1, lambda_q2, lambda_k2, subln, w_o_att, w_out, norm_ffn, router_w, router_b, w_gu, b_gu, w_down, b_down, norm_final):
    bp, s, d = x_prompt.shape
    bs, ts, _ = x_sample.shape
    past = cache_k.shape[2]
    depth = w_in.shape[0]
    assert depth == 1
    layer = 0
    lambda_init = 0.8 - 0.6 * math.exp(-0.3 * layer)
    tp, tsn = bp * s, bs * ts
    t_all = tp + tsn

    heads = ATT_N_HEADS
    xp2 = x_prompt.reshape(tp, d)
    xs2 = x_sample.reshape(tsn, d)

    sizes = (SSD_D_INNER, SSD_CONV_CH, SSD_N_HEADS, d, d, d, d, d)
    offs = [0]
    for sz in sizes:
        offs.append(offs[-1] + sz)
    w = w_in[layer].astype(BF16)
    seg = lambda i: w[:, offs[i]:offs[i + 1]]
    dt_pad = jnp.zeros((d, PROJ_XD_W - SSD_CONV_CH - SSD_N_HEADS), BF16)
    w_all = jnp.concatenate([seg(4), seg(5), seg(0), seg(3), seg(6), seg(7), seg(1), seg(2), dt_pad], axis=1)
    k_p, v_p, kv16_p, zq_p, xd_p = in_proj(xp2, norm_mix[layer], w_all, tm=2048)
    k_s, v_s, kv16_s, zq_s, xd_s = in_proj(xs2, norm_mix[layer], w_all, tm=tsn)

    prev_p = jnp.zeros((bp, SUBLANES, SSD_CONV_CH), F32)
    h0_p = jnp.zeros((bp, SSD_D_STATE, SSD_D_INNER), F32)
    ssd_w = (conv_w[layer], conv_b[layer], dt_bias[layer], a_log[layer], d_skip[layer], ssd_norm[layer])
    y_p, hT_p = ssd_mixer(xd_p, zq_p, 0, prev_p, h0_p, *ssd_w, b=bp, l=s, tl=256, valid_len=None, out_rows=tp)

    pad_rows = CHUNK - ts
    pad_seq = lambda a: jnp.pad(a.reshape(bs, ts, -1), ((0, 0), (0, pad_rows), (0, 0))).reshape(bs * CHUNK, -1)
    prev_s = jnp.pad(state_conv[layer], ((0, 0), (SUBLANES - (SSD_CONV_WIDTH - 1), 0), (0, 0)))
    h0_s = jnp.swapaxes(state_ssm[layer].reshape(bs, SSD_D_INNER, SSD_D_STATE), 1, 2)
    y_s, hT_s = ssd_mixer(pad_seq(xd_s), pad_seq(zq_s[:, :SSD_D_INNER]), 0, prev_s, h0_s, *ssd_w, b=bs, l=CHUNK,
                          tl=CHUNK, valid_len=ts, out_rows=bs * CHUNK)
    y_s = y_s.reshape(bs, CHUNK, -1)[:, :ts].reshape(tsn, -1)

    lam_rows = jnp.concatenate([lambda_q1[layer][None], lambda_k1[layer][None], lambda_q2[layer][None],
                                lambda_k2[layer][None], jnp.zeros((4, ATT_HEAD_DIM), F32)], axis=0)
    o_p = diff_attention_t(zq_p, 0, SSD_D_INNER // d, kv16_p, 0, kv16_p, 1, lam_rows, subln[layer], b=bp, lq=s,
                           lk=s, tq=512, tk=1024, q_off=0, kv_len=s, lambda_init=lambda_init, out_rows=tp)
    o_s = diff_attention_cached(zq_s.reshape(bs, ts, -1), SSD_D_INNER // d, cache_k[layer], cache_v[layer],
                                kv16_s.reshape(bs, ts, 2 * d), lam_rows, subln[layer], tkc=1024,
                                lambda_init=lambda_init).reshape(tsn, d)

    rw = jnp.zeros((d, LANES), F32).at[:, :N_EXPERTS].set(router_w[layer])
    rb = jnp.zeros((1, LANES), F32).at[0, :N_EXPERTS].set(router_b[layer])
    mix_w = (w_o_ssd[layer].astype(BF16), w_o_att[layer].astype(BF16), w_out[layer].astype(BF16), norm_ffn[layer],
             rw, rb)
    x2_p, hn_p, gate_p, sel_p, cnt_p = mixer_out(y_p, o_p, zq_p, 3, 4, xp2, *mix_w,
                                                 jnp.zeros((1, LANES), F32), tm=512)
    x2_s, hn_s, gate_s, sel_s, cnt_all = mixer_out(y_s, o_s, zq_s, 3, 4, xs2, *mix_w, cnt_p, tm=tsn)

    sel = jnp.concatenate([sel_p, sel_s], axis=0)
    top_e = sel[:, :TOP_K]
    rank = sel[:, TOP_K:]
    cnt = cnt_all[0, :N_EXPERTS].astype(jnp.int32)
    padded = (cnt + MOE_ROWS - 1) // MOE_ROWS * MOE_ROWS
    ends = jnp.cumsum(padded)
    starts = ends - padded
    dest = (starts[top_e] + rank).reshape(-1)
    nb = (t_all * TOP_K + N_EXPERTS * (MOE_ROWS - 1) + MOE_ROWS - 1) // MOE_ROWS
    block_start = jnp.arange(nb, dtype=jnp.int32) * MOE_ROWS
    block_e = jnp.minimum(jnp.sum((ends[None, :] <= block_start[:, None]).astype(jnp.int32), axis=1),
                          N_EXPERTS - 1)
    n_used = (ends[-1] // MOE_ROWS).astype(jnp.int32).reshape(1)

    xs_sorted = moe_dispatch(dest, hn_p, hn_s, starts + cnt, padded - cnt, n_used, nb)
    next_first = ends[block_e] // MOE_ROWS
    ys = moe_experts(xs_sorted, block_e, n_used, next_first, w_gu[layer], b_gu[layer], w_down[layer],
                     b_down[layer])
    y_prompt = combine(x2_p, ys, dest[:tp * TOP_K], gate_p, norm_final).reshape(bp, s, d)
    y_sample = combine(x2_s, ys, dest[tp * TOP_K:], gate_s, norm_final).reshape(bs, ts, d)

    new_k_p = k_p.reshape(1, bp, s, heads, 2 * ATT_HEAD_DIM)
    new_v_p = v_p.reshape(1, bp, s, heads, ATT_V_HEAD)
    new_k_s = k_s.reshape(1, bs, ts, heads, 2 * ATT_HEAD_DIM)
    new_v_s = v_s.reshape(1, bs, ts, heads, ATT_V_HEAD)
    ssm_p = jnp.swapaxes(hT_p, 1, 2).reshape(1, bp, SSD_N_HEADS, SSD_HEAD_DIM, SSD_D_STATE)
    ssm_s = jnp.swapaxes(hT_s, 1, 2).reshape(1, bs, SSD_N_HEADS, SSD_HEAD_DIM, SSD_D_STATE)
    keep = SSD_CONV_WIDTH - 1
    conv_p = xd_p.reshape(bp, s, -1)[:, s - keep:, :SSD_CONV_CH][None]
    raw_s = jnp.concatenate([state_conv[layer], xd_s.reshape(bs, ts, -1)[:, :, :SSD_CONV_CH]], axis=1)
    conv_s = raw_s[:, -keep:][None]
    return (y_prompt, y_sample, new_k_p, new_v_p, ssm_p, conv_p, new_k_s, new_v_s, ssm_s, conv_s)
```
